```python
import math
import jax, jax.numpy as jnp
from jax import lax
import numpy as np

D_MODEL = 1024
BATCH = 16
SEQ = 2048
DEPTH = 4

CTX_LEN = 256
GRID_W = 64

N_HEADS = 8
HEAD_DIM = 64
V_DIM = 2 * HEAD_DIM
QK_W = N_HEADS * 2 * HEAD_DIM
ATTN_W = N_HEADS * V_DIM
CONV_W = D_MODEL
CONV_K = 3
N_FGROUPS = 8
FGROUP_DIM = 128
FOURIER_W = N_FGROUPS * FGROUP_DIM
N_BRANCH = 3
ROPE_BASE = 10000.0
AXIS_DIM = HEAD_DIM // 2
Q_BLOCK = 128
NORM_EPS = 1e-6
SUBLN_EPS = 1e-5

SPLIT_SIZES = (QK_W, QK_W, ATTN_W, ATTN_W,
               CONV_W, CONV_W, CONV_W, CONV_W,
               FOURIER_W, FOURIER_W,
               N_BRANCH * D_MODEL)
PROJ_W = 2 * QK_W + 2 * ATTN_W + 4 * CONV_W + 2 * FOURIER_W + N_BRANCH * D_MODEL

kernel_name = "hybrid_diffattn_shortconv_fourier_dit"


def rmsnorm(x, w, eps=NORM_EPS):
    xf = x.astype(jnp.float32)
    y = xf * lax.rsqrt(jnp.mean(xf * xf, axis=-1, keepdims=True) + eps)
    return y.astype(x.dtype) * w


def adaln(cond, w_mod, b_mod):
    mod = jax.nn.silu(cond) @ w_mod + b_mod
    shift, scale, gate = jnp.split(mod, 3, axis=-1)
    return (jnp.expand_dims(shift, -2), jnp.expand_dims(scale, -2), jnp.expand_dims(gate, -2))


def split_proj(p):
    idx = [int(i) for i in np.cumsum(SPLIT_SIZES)[:-1]]
    return jnp.split(p, idx, axis=-1)


def axial_angles(n_tokens):
    rows = n_tokens // GRID_W
    row = jnp.repeat(jnp.arange(rows), GRID_W).astype(jnp.float32)
    col = jnp.tile(jnp.arange(GRID_W), rows).astype(jnp.float32)
    inv_freq = ROPE_BASE ** (-jnp.arange(0, AXIS_DIM, 2, dtype=jnp.float32) / AXIS_DIM)
    return row[:, None] * inv_freq, col[:, None] * inv_freq


def rotate(x, ang):
    cos = jnp.cos(ang).astype(x.dtype)
    sin = jnp.sin(ang).astype(x.dtype)
    x1, x2 = jnp.split(x, 2, axis=-1)
    return jnp.concatenate([x1 * cos - x2 * sin, x2 * cos + x1 * sin], axis=-1)


def axial_rope(x, ang_r, ang_c):
    xr, xc = jnp.split(x, 2, axis=-1)
    return jnp.concatenate([rotate(xr, ang_r), rotate(xc, ang_c)], axis=-1)


def qk_heads(t):
    b, n, _ = t.shape
    return t.reshape(b, n, N_HEADS, 2, HEAD_DIM).transpose(0, 2, 3, 1, 4)


def v_heads(t):
    b, n, _ = t.shape
    return t.reshape(b, n, N_HEADS, V_DIM).transpose(0, 2, 1, 3)


def diff_weights(q, k, lam):
    s = jnp.einsum('bhmqd,bhmkd->bhmqk', q, k).astype(jnp.float32) / math.sqrt(HEAD_DIM)
    p = jax.nn.softmax(s, axis=-1)
    return p[:, :, 0] - lam * p[:, :, 1]


def latent_diff_attn(q, k, v, lam):
    b, h, _, s, d = q.shape
    nblk = s // Q_BLOCK
    qb = jnp.moveaxis(q.reshape(b, h, 2, nblk, Q_BLOCK, d), 3, 0)

    def one_block(qblk):
        a = diff_weights(qblk, k, lam).astype(v.dtype)
        return jnp.einsum('bhqk,bhkv->bhqv', a, v)

    o = lax.map(one_block, qb)
    return jnp.moveaxis(o, 0, 2).reshape(b, h, s, V_DIM)


def context_diff_attn(q, k, v, lam):
    a = diff_weights(q, k, lam).astype(v.dtype)
    return jnp.einsum('bhqk,bhkv->bhqv', a, v)


def subln(o, w, lam_init):
    b, h, n, vd = o.shape
    y = rmsnorm(o, w, SUBLN_EPS) * (1.0 - lam_init)
    return y.transpose(0, 2, 1, 3).reshape(b, n, h * vd)


def short_conv(xin, b_g, c_g, conv_w):
    u = c_g * xin
    up = jnp.pad(u, ((0, 0), (1, 1), (0, 0)))
    y = conv_w[0] * up[:, :-2] + conv_w[1] * up[:, 1:-1] + conv_w[2] * up[:, 2:]
    return b_g * y


def fourier_mix(u):
    b, n, _ = u.shape
    g = u.astype(jnp.float32).reshape(b, n, N_FGROUPS, FGROUP_DIM)
    y = jnp.fft.fft2(g, axes=(1, 3), norm="ortho").real
    return y.reshape(b, n, FOURIER_W).astype(u.dtype)


def other_branches_and_merge(y_a, z_a, xin, b_g, c_g, z_c, u_f, z_f, gate_logits,
                             conv_w, w_a, w_c, w_f, w_o):
    y_c = short_conv(xin, b_g, c_g, conv_w)
    y_f = fourier_mix(u_f)
    g = jax.nn.sigmoid(gate_logits.astype(jnp.float32)).astype(y_a.dtype)
    g_a, g_c, g_f = jnp.split(g, 3, axis=-1)
    merged = (g_a * ((y_a * jax.nn.silu(z_a)) @ w_a)
              + g_c * ((y_c * jax.nn.silu(z_c)) @ w_c)
              + g_f * ((y_f * jax.nn.silu(z_f)) @ w_f))
    return merged @ w_o


def setup_inputs(seed: int = 0) -> dict:
    key = jax.random.key(seed)
    ks = jax.random.split(key, 16)
    f32 = jnp.float32
    nrm = lambda k, shp: jax.random.normal(k, shp, f32)
    return {
        "x": nrm(ks[0], (BATCH, SEQ, D_MODEL)),
        "c": nrm(ks[1], (BATCH, D_MODEL)),
        "ctx": nrm(ks[2], (BATCH, CTX_LEN, D_MODEL)),
        "c_ctx": nrm(ks[3], (D_MODEL,)),
        "norm_w": 1.0 + 0.02 * nrm(ks[4], (DEPTH, D_MODEL)),
        "w_mod": nrm(ks[5], (DEPTH, D_MODEL, 3 * D_MODEL)) * (0.5 * D_MODEL ** -0.5),
        "b_mod": 0.01 * nrm(ks[6], (DEPTH, 3 * D_MODEL)),
        "w_in": nrm(ks[7], (DEPTH, D_MODEL, PROJ_W)) * D_MODEL ** -0.5,
        "lambda_qk": 0.1 * nrm(ks[8], (DEPTH, 4, HEAD_DIM)),
        "subln_w": 1.0 + 0.02 * nrm(ks[9], (DEPTH, V_DIM)),
        "conv_w": nrm(ks[10], (DEPTH, CONV_K, CONV_W)) * CONV_K ** -0.5,
        "w_attn_o": nrm(ks[11], (DEPTH, ATTN_W, D_MODEL)) * ATTN_W ** -0.5,
        "w_conv_o": nrm(ks[12], (DEPTH, CONV_W, D_MODEL)) * CONV_W ** -0.5,
        "w_four_o": nrm(ks[13], (DEPTH, FOURIER_W, D_MODEL)) * FOURIER_W ** -0.5,
        "w_out": nrm(ks[14], (DEPTH, D_MODEL, D_MODEL)) * D_MODEL ** -0.5,
        "final_norm_w": 1.0 + 0.02 * nrm(ks[15], (D_MODEL,)),
    }


def reference(x, c, ctx, c_ctx, norm_w, w_mod, b_mod, w_in, lambda_qk, subln_w, conv_w,
              w_attn_o, w_conv_o, w_four_o, w_out, final_norm_w):
    ang_r, ang_c = axial_angles(x.shape[1])
    for l in range(DEPTH):
        last = l == DEPTH - 1
        lam_init = 0.8 - 0.6 * math.exp(-0.3 * l)
        lq = lambda_qk[l].astype(jnp.float32)
        lam = jnp.exp(jnp.sum(lq[0] * lq[1])) - jnp.exp(jnp.sum(lq[2] * lq[3])) + lam_init

        sh, sc, gt = adaln(c, w_mod[l], b_mod[l])
        shc, scc, gtc = adaln(c_ctx, w_mod[l], b_mod[l])
        h = rmsnorm(x, norm_w[l]) * (1.0 + sc) + sh
        hc = rmsnorm(ctx, norm_w[l]) * (1.0 + scc) + shc

        (q, k, v, z_a, xin, b_g, c_g, z_c, u_f, z_f, gl) = split_proj(h @ w_in[l])
        (qc, kc, vc, z_ac, xinc, b_gc, c_gc, z_cc, u_fc, z_fc, glc) = split_proj(hc @ w_in[l])

        q_h = axial_rope(qk_heads(q), ang_r, ang_c)
        k_h = axial_rope(qk_heads(k), ang_r, ang_c)
        qc_h, kc_h, vc_h = qk_heads(qc), qk_heads(kc), v_heads(vc)
        k_all = jnp.concatenate([k_h, kc_h], axis=3)
        v_all = jnp.concatenate([v_heads(v), vc_h], axis=2)
        y_a = subln(latent_diff_attn(q_h, k_all, v_all, lam), subln_w[l], lam_init)

        out = other_branches_and_merge(y_a, z_a, xin, b_g, c_g, z_c, u_f, z_f, gl,
                                       conv_w[l], w_attn_o[l], w_conv_o[l], w_four_o[l], w_out[l])

        if not last:
            y_ac = subln(context_diff_attn(qc_h, kc_h, vc_h, lam), subln_w[l], lam_init)
            out_c = other_branches_and_merge(y_ac, z_ac, xinc, b_gc, c_gc, z_cc, u_fc, z_fc, glc,
                                             conv_w[l], w_attn_o[l], w_conv_o[l], w_four_o[l], w_out[l])
            ctx = ctx + gtc * out_c

        x = x + gt * out
    return rmsnorm(x, final_norm_w)
```

```python
import functools
import math

import jax
import jax.numpy as jnp
from jax import lax
from jax.experimental import pallas as pl
from jax.experimental.pallas import tpu as pltpu

F32 = jnp.float32
BF16 = jnp.bfloat16

N_HEADS = 8
HEAD_DIM = 64
V_DIM = 2 * HEAD_DIM
FGROUP_DIM = 128
GRID_W = 64
ROPE_BASE = 10000.0
NORM_EPS = 1e-6
SUBLN_EPS = 1e-5
N_PROJ_BLOCKS = 13
COL_Q, COL_K, COL_V, COL_ZA, COL_XIN, COL_BG, COL_CG, COL_ZC, COL_UF, COL_ZF, COL_GL = range(11)

LANES = 128
SUBLANES = 8
VMEM_LIMIT_BYTES = 56 * 1024 * 1024
MOD_ROWS_PAD = 8


def _cparams(sem):
    return pltpu.CompilerParams(dimension_semantics=sem, vmem_limit_bytes=VMEM_LIMIT_BYTES)


def _mod_kernel(cond_ref, w_ref, b_ref, o_ref):
    cond = cond_ref[...]
    a = cond * jax.nn.sigmoid(cond)
    o_ref[...] = jnp.dot(a, w_ref[...], preferred_element_type=F32,
                         precision=lax.Precision.HIGHEST) + b_ref[...]


def _modulation(cond, w_mod, b_mod):
    depth, d, w3 = w_mod.shape
    rows = cond.shape[0]
    tn = d
    return pl.pallas_call(
        _mod_kernel,
        grid=(depth, w3 // tn),
        in_specs=[
            pl.BlockSpec((rows, d), lambda l, j: (0, 0)),
            pl.BlockSpec((None, d, tn), lambda l, j: (l, 0, j)),
            pl.BlockSpec((None, 1, tn), lambda l, j: (l, 0, j)),
        ],
        out_specs=pl.BlockSpec((None, rows, tn), lambda l, j: (l, 0, j)),
        out_shape=jax.ShapeDtypeStruct((depth, rows, w3), F32),
        compiler_params=_cparams(("parallel", "parallel")),
        name="modulation",
    )(cond, w_mod, b_mod.reshape(depth, 1, w3))


def _swap_halves16(a):
    lane = lax.broadcasted_iota(jnp.int32, a.shape, 1)
    first = (lane % 32) < 16
    return jnp.where(first, pltpu.roll(a, LANES - 16, axis=1), pltpu.roll(a, 16, axis=1))


def _proj_kernel(x_ref, sh_ref, sc_ref, nw_ref, w_ref, *rest, rope, n_rope_blocks):
    if rope:
        cos_ref, sin_ref, o_ref, h_ref = rest
    else:
        o_ref, h_ref = rest
    j = pl.program_id(2)

    @pl.when(j == 0)
    def _():
        x = x_ref[0]
        y = x * lax.rsqrt(jnp.mean(x * x, axis=-1, keepdims=True) + NORM_EPS)
        h = (y * nw_ref[...]) * (1.0 + sc_ref[0]) + sh_ref[0]
        h_ref[...] = h.astype(BF16)

    acc = jnp.dot(h_ref[...], w_ref[...], preferred_element_type=F32)
    tn = acc.shape[1]

    if rope:
        @pl.when(j < n_rope_blocks)
        def _():
            cos = cos_ref[...]
            sin = sin_ref[...]
            for hh in range(tn // LANES):
                a = acc[:, hh * LANES:(hh + 1) * LANES]
                r = a * cos + _swap_halves16(a) * sin
                o_ref[0, :, hh * LANES:(hh + 1) * LANES] = r.astype(BF16)

        @pl.when(j >= n_rope_blocks)
        def _():
            o_ref[0] = acc.astype(BF16)
    else:
        scale = jnp.where(j == 0, 1.0 / math.sqrt(HEAD_DIM), 1.0).astype(F32)
        o_ref[0] = (acc * scale).astype(BF16)


def _proj(x, mod3, mod_row, norm_w, w_in_b, layer, rope_tabs, tm):
    bx, sx, d = x.shape
    tn = d
    nj = w_in_b.shape[2] // tn
    rope = rope_tabs is not None
    in_specs = [
        pl.BlockSpec((1, tm, d), lambda b, i, j: (b, i, 0)),
        pl.BlockSpec((1, 1, d), lambda b, i, j: (mod_row(b), 0, 0)),
        pl.BlockSpec((1, 1, d), lambda b, i, j: (mod_row(b), 0, 1)),
        pl.BlockSpec((None, 1, d), lambda b, i, j: (layer, 0, 0)),
        pl.BlockSpec((None, d, tn), lambda b, i, j: (layer, 0, j)),
    ]
    args = [x, mod3, mod3, norm_w, w_in_b]
    n_rope_blocks = 2 * (N_HEADS * V_DIM) // tn
    if rope:
        cos_t, sin_t = rope_tabs
        tab_spec = pl.BlockSpec(
            (None, tm, LANES),
            lambda b, i, j: (jnp.minimum(j // (n_rope_blocks // 2), 1), i, 0))
        in_specs += [tab_spec, tab_spec]
        args += [cos_t, sin_t]
    return pl.pallas_call(
        functools.partial(_proj_kernel, rope=rope, n_rope_blocks=n_rope_blocks),
        grid=(bx, sx // tm, nj),
        in_specs=in_specs,
        out_specs=pl.BlockSpec((1, tm, tn), lambda b, i, j: (b, i, j)),
        out_shape=jax.ShapeDtypeStruct((bx, sx, nj * tn), BF16),
        scratch_shapes=[pltpu.VMEM((tm, d), BF16)],
        compiler_params=_cparams(("parallel", "parallel", "arbitrary")),
        name="proj_rope" if rope else "proj",
    )(*args)


def _attn_kernel(lq_ref, sw_ref, q_ref, k_ref, v_ref, *rest, has_ctx, lam_init):
    if has_ctx:
        kc_ref, vc_ref, o_ref = rest
    else:
        (o_ref,) = rest
    lq = lq_ref[...].astype(F32)
    lam = (jnp.exp(jnp.sum(lq[0:1] * lq[1:2], axis=-1, keepdims=True))
           - jnp.exp(jnp.sum(lq[2:3] * lq[3:4], axis=-1, keepdims=True)) + lam_init)

    q = q_ref[0]
    tq = q.shape[0]
    lane = lax.broadcasted_iota(jnp.int32, q.shape, 1)
    zero = jnp.zeros_like(q)
    qq = jnp.concatenate([jnp.where(lane < HEAD_DIM, q, zero),
                          jnp.where(lane >= HEAD_DIM, q, zero)], axis=0)
    nt = (((1,), (1,)), ((), ()))
    s = lax.dot_general(qq, k_ref[0], nt, preferred_element_type=F32)
    m = jnp.max(s, axis=-1, keepdims=True)
    if has_ctx:
        sc = lax.dot_general(qq, kc_ref[0], nt, preferred_element_type=F32)
        m = jnp.maximum(m, jnp.max(sc, axis=-1, keepdims=True))
    p = jnp.exp(s - m)
    denom = jnp.sum(p, axis=-1, keepdims=True)
    if has_ctx:
        pc = jnp.exp(sc - m)
        denom = denom + jnp.sum(pc, axis=-1, keepdims=True)
    r = 1.0 / denom
    r1 = r[:tq]
    r2 = r[tq:] * lam
    a = (p[:tq] * r1 - p[tq:] * r2).astype(BF16)
    o = jnp.dot(a, v_ref[0], preferred_element_type=F32)
    if has_ctx:
        ac = (pc[:tq] * r1 - pc[tq:] * r2).astype(BF16)
        o = o + jnp.dot(ac, vc_ref[0], preferred_element_type=F32)
    y = o * lax.rsqrt(jnp.mean(o * o, axis=-1, keepdims=True) + SUBLN_EPS)
    o_ref[0] = ((y * sw_ref[...]) * (1.0 - lam_init)).astype(BF16)


def _attention(lambda_qk, subln_w, layer, lam_init, p_q, p_ctx, tq):
    b, s, _ = p_q.shape
    nh = N_HEADS
    d_blocks = nh
    has_ctx = p_ctx is not None
    in_specs = [
        pl.BlockSpec((None, 4, HEAD_DIM), lambda bb, h, i: (layer, 0, 0)),
        pl.BlockSpec((None, 1, V_DIM), lambda bb, h, i: (layer, 0, 0)),
        pl.BlockSpec((1, tq, V_DIM), lambda bb, h, i: (bb, i, COL_Q * d_blocks + h)),
        pl.BlockSpec((1, s, V_DIM), lambda bb, h, i: (bb, 0, COL_K * d_blocks + h)),
        pl.BlockSpec((1, s, V_DIM), lambda bb, h, i: (bb, 0, COL_V * d_blocks + h)),
    ]
    args = [lambda_qk, subln_w, p_q, p_q, p_q]
    if has_ctx:
        lc = p_ctx.shape[1]
        in_specs += [
            pl.BlockSpec((1, lc, V_DIM), lambda bb, h, i: (bb, 0, COL_K * d_blocks + h)),
            pl.BlockSpec((1, lc, V_DIM), lambda bb, h, i: (bb, 0, COL_V * d_blocks + h)),
        ]
        args += [p_ctx, p_ctx]
    return pl.pallas_call(
        functools.partial(_attn_kernel, has_ctx=has_ctx, lam_init=lam_init),
        grid=(b, nh, s // tq),
        in_specs=in_specs,
        out_specs=pl.BlockSpec((1, tq, V_DIM), lambda bb, h, i: (bb, i, h)),
        out_shape=jax.ShapeDtypeStruct((b, s, nh * V_DIM), BF16),
        compiler_params=_cparams(("parallel", "parallel", "parallel")),
        name="attn_ctx" if has_ctx else "attn",
    )(*args)


def _fourier_kernel(u_ref, cs_ref, w_ref, o_ref, ab_ref):
    r = pl.program_id(1)
    n = u_ref.shape[1]

    @pl.when(r == 0)
    def _():
        for g in range(u_ref.shape[2] // FGROUP_DIM):
            cols = slice(g * FGROUP_DIM, (g + 1) * FGROUP_DIM)
            ab = jnp.dot(u_ref[0, :, cols], cs_ref[...], preferred_element_type=F32)
            ab_ref[0:n, cols] = ab[:, :FGROUP_DIM].astype(BF16)
            ab_ref[n:2 * n, cols] = ab[:, FGROUP_DIM:].astype(BF16)

    o_ref[0] = jnp.dot(w_ref[...], ab_ref[...], preferred_element_type=F32).astype(BF16)


def _dft_cos_sin(n):
    k = jnp.arange(n, dtype=jnp.int32)
    ang = ((k[:, None] * k[None, :]) % n).astype(F32) * (2.0 * math.pi / n)
    scale = 1.0 / math.sqrt(n)
    return jnp.cos(ang) * scale, jnp.sin(ang) * scale


def _fourier(p, cs_chan, w_pos, tr):
    b, n, _ = p.shape
    d = p.shape[2] // N_PROJ_BLOCKS
    return pl.pallas_call(
        _fourier_kernel,
        grid=(b, n // tr),
        in_specs=[
            pl.BlockSpec((1, n, d), lambda bb, r: (bb, 0, COL_UF)),
            pl.BlockSpec((FGROUP_DIM, 2 * FGROUP_DIM), lambda bb, r: (0, 0)),
            pl.BlockSpec((tr, 2 * n), lambda bb, r: (r, 0)),
        ],
        out_specs=pl.BlockSpec((1, tr, d), lambda bb, r: (bb, r, 0)),
        out_shape=jax.ShapeDtypeStruct((b, n, d), BF16),
        scratch_shapes=[pltpu.VMEM((2 * n, d), BF16)],
        compiler_params=_cparams(("parallel", "arbitrary")),
        name="fourier",
    )(p, cs_chan, w_pos)


def _silu(z):
    return z * jax.nn.sigmoid(z)


def _merge_kernel(ya_ref, yf_ref, za_ref, xin_ref, bg_ref, cg_ref, zc_ref, zf_ref,
                  ga_ref, gc_ref, gf_ref, xin_p_ref, cg_p_ref, xin_n_ref, cg_n_ref,
                  x_ref, gate_ref, cw_ref, wa_ref, wc_ref, wf_ref, wo_ref, o_ref, u_ref):
    i = pl.program_id(1)
    tm = x_ref.shape[1]
    pad = SUBLANES

    u_ref[pad:pad + tm, :] = cg_ref[0].astype(F32) * xin_ref[0].astype(F32)
    up = cg_p_ref[0].astype(F32) * xin_p_ref[0].astype(F32)
    un = cg_n_ref[0].astype(F32) * xin_n_ref[0].astype(F32)
    u_ref[0:pad, :] = jnp.where(i == 0, 0.0, up)
    u_ref[pad + tm:2 * pad + tm, :] = jnp.where(i == pl.num_programs(1) - 1, 0.0, un)
    cw = cw_ref[...]
    conv = (cw[0:1] * u_ref[pad - 1:pad - 1 + tm, :] + cw[1:2] * u_ref[pad:pad + tm, :]
            + cw[2:3] * u_ref[pad + 1:pad + 1 + tm, :])
    y_c = bg_ref[0].astype(F32) * conv

    def branch(y, z_ref, w_ref, g_ref):
        t = jnp.dot((y * _silu(z_ref[0].astype(F32))).astype(BF16), w_ref[...],
                    preferred_element_type=F32)
        return jax.nn.sigmoid(g_ref[0].astype(F32)) * t

    merged = (branch(ya_ref[0].astype(F32), za_ref, wa_ref, ga_ref)
              + branch(y_c, zc_ref, wc_ref, gc_ref)
              + branch(yf_ref[0].astype(F32), zf_ref, wf_ref, gf_ref))
    out = jnp.dot(merged.astype(BF16), wo_ref[...], preferred_element_type=F32)
    o_ref[0] = x_ref[0] + gate_ref[0] * out


def _merge(x, p, ya, yf, mod3, mod_row, conv_w, wa, wc, wf, wo, layer, tm):
    bx, sx, d = x.shape
    nb = tm // SUBLANES
    last_halo = sx // SUBLANES - 1

    def col(cb):
        return pl.BlockSpec((1, tm, d), lambda b, i: (b, i, cb))

    def halo_prev(cb):
        return pl.BlockSpec((1, SUBLANES, d), lambda b, i: (b, jnp.maximum(i * nb - 1, 0), cb))

    def halo_next(cb):
        return pl.BlockSpec((1, SUBLANES, d), lambda b, i: (b, jnp.minimum((i + 1) * nb, last_halo), cb))

    def weight():
        return pl.BlockSpec((None, d, d), lambda b, i: (layer, 0, 0))

    in_specs = [
        col(0), col(0),
        col(COL_ZA), col(COL_XIN), col(COL_BG), col(COL_CG), col(COL_ZC), col(COL_ZF),
        col(COL_GL), col(COL_GL + 1), col(COL_GL + 2),
        halo_prev(COL_XIN), halo_prev(COL_CG), halo_next(COL_XIN), halo_next(COL_CG),
        col(0),
        pl.BlockSpec((1, 1, d), lambda b, i: (mod_row(b), 0, 2)),
        pl.BlockSpec((None, 3, d), lambda b, i: (layer, 0, 0)),
        weight(), weight(), weight(), weight(),
    ]
    return pl.pallas_call(
        _merge_kernel,
        grid=(bx, sx // tm),
        in_specs=in_specs,
        out_specs=pl.BlockSpec((1, tm, d), lambda b, i: (b, i, 0)),
        out_shape=jax.ShapeDtypeStruct((bx, sx, d), F32),
        scratch_shapes=[pltpu.VMEM((tm + 2 * SUBLANES, d), F32)],
        compiler_params=_cparams(("parallel", "parallel")),
        name="merge",
    )(ya, yf, p, p, p, p, p, p, p, p, p, p, p, p, p, x, mod3, conv_w, wa, wc, wf, wo)


def _final_norm_kernel(x_ref, w_ref, o_ref):
    x = x_ref[...]
    y = x * lax.rsqrt(jnp.mean(x * x, axis=-1, keepdims=True) + NORM_EPS)
    o_ref[...] = y * w_ref[...]


def _final_norm(x, w, tm):
    b, s, d = x.shape
    x2 = x.reshape(b * s, d)
    out = pl.pallas_call(
        _final_norm_kernel,
        grid=(b * s // tm,),
        in_specs=[pl.BlockSpec((tm, d), lambda i: (i, 0)), pl.BlockSpec((1, d), lambda i: (0, 0))],
        out_specs=pl.BlockSpec((tm, d), lambda i: (i, 0)),
        out_shape=jax.ShapeDtypeStruct((b * s, d), F32),
        compiler_params=_cparams(("parallel",)),
        name="final_norm",
    )(x2, w.reshape(1, d))
    return out.reshape(b, s, d)


def _rope_tables(n_tokens):
    axis_dim = HEAD_DIM // 2
    rows = n_tokens // GRID_W
    row = jnp.repeat(jnp.arange(rows), GRID_W).astype(F32)
    col = jnp.tile(jnp.arange(GRID_W), rows).astype(F32)
    inv_freq = ROPE_BASE ** (-jnp.arange(0, axis_dim, 2, dtype=F32) / axis_dim)
    ang_r = row[:, None] * inv_freq
    ang_c = col[:, None] * inv_freq
    ang = jnp.concatenate([ang_r, ang_r, ang_c, ang_c], axis=-1)
    ang = jnp.concatenate([ang, ang], axis=-1)
    half = axis_dim // 2
    sign = jnp.where((jnp.arange(V_DIM) % axis_dim) < half, -1.0, 1.0).astype(F32)
    cos = jnp.cos(ang)
    sin = jnp.sin(ang) * sign
    qs = 1.0 / math.sqrt(HEAD_DIM)
    return jnp.stack([cos * qs, cos]), jnp.stack([sin * qs, sin])


def _fourier_tables(n):
    cn, sn = _dft_cos_sin(n)
    return jnp.concatenate([cn, -sn], axis=1).astype(BF16)


def _pick_tile(n, target):
    t = min(n, target)
    while n % t:
        t //= 2
    return t


def kernel(x, c, ctx, c_ctx, norm_w, w_mod, b_mod, w_in, lambda_qk, subln_w, conv_w,
           w_attn_o, w_conv_o, w_four_o, w_out, final_norm_w):
    b, s, d = x.shape
    lc = ctx.shape[1]
    depth = w_in.shape[0]
    assert d == N_HEADS * V_DIM and w_in.shape[2] == N_PROJ_BLOCKS * d
    assert s % GRID_W == 0 and s % LANES == 0 and lc % SUBLANES == 0

    pad = (-(b + 1)) % MOD_ROWS_PAD
    cond = jnp.concatenate([c, c_ctx[None, :], jnp.zeros((pad, d), F32)], axis=0)
    mod = _modulation(cond, w_mod, b_mod)
    n_rows = cond.shape[0]

    norm_w = norm_w.reshape(depth, 1, d)
    subln_w = subln_w.reshape(depth, 1, V_DIM)
    w_in_b = w_in.astype(BF16)
    wa_b, wc_b, wf_b, wo_b = (w.astype(BF16) for w in (w_attn_o, w_conv_o, w_four_o, w_out))

    rope_tabs = _rope_tables(s)
    cc, sc = _dft_cos_sin(FGROUP_DIM)
    cs_chan = jnp.concatenate([cc, sc], axis=1).astype(BF16)
    w_pos_lat = _fourier_tables(s)
    w_pos_ctx = _fourier_tables(lc)

    ctx_flat = ctx.reshape(1, b * lc, d)
    lat_row = lambda bb: bb
    ctx_row = lambda bb: b

    tm_lat = _pick_tile(s, 1024)
    tm_ctx = _pick_tile(b * lc, 1024)
    tq_lat = _pick_tile(s, 256)
    tr_lat = _pick_tile(s, 512)
    tmm_lat = _pick_tile(s, 256)
    tmm_ctx = _pick_tile(lc, 256)

    for l in range(depth):
        last = l == depth - 1
        lam_init = 0.8 - 0.6 * math.exp(-0.3 * l)
        mod3 = mod[l].reshape(n_rows, 1, 3 * d)

        p_lat = _proj(x, mod3, lat_row, norm_w, w_in_b, l, rope_tabs, tm_lat)
        p_ctx = _proj(ctx_flat, mod3, ctx_row, norm_w, w_in_b, l, None, tm_ctx).reshape(b, lc, -1)

        ya = _attention(lambda_qk, subln_w, l, lam_init, p_lat, p_ctx, tq_lat)
        yf = _fourier(p_lat, cs_chan, w_pos_lat, tr_lat)
        x_new = _merge(x, p_lat, ya, yf, mod3, lat_row, conv_w, wa_b, wc_b, wf_b, wo_b, l, tmm_lat)

        if not last:
            yac = _attention(lambda_qk, subln_w, l, lam_init, p_ctx, None, lc)
            yfc = _fourier(p_ctx, cs_chan, w_pos_ctx, lc)
            ctx3 = ctx_flat.reshape(b, lc, d)
            ctx_flat = _merge(ctx3, p_ctx, yac, yfc, mod3, ctx_row, conv_w, wa_b, wc_b, wf_b, wo_b,
                              l, tmm_ctx).reshape(1, b * lc, d)
        x = x_new

    return _final_norm(x, final_norm_w, _pick_tile(b * s, 1024))
```

```python
import functools
import math

import jax
import jax.numpy as jnp
from jax import lax
from jax.experimental import pallas as pl
from jax.experimental.pallas import tpu as pltpu

F32 = jnp.float32
BF16 = jnp.bfloat16

N_HEADS = 8
HEAD_DIM = 64
V_DIM = 2 * HEAD_DIM
FGROUP_DIM = 128
GRID_W = 64
ROPE_BASE = 10000.0
NORM_EPS = 1e-6
SUBLN_EPS = 1e-5
N_PROJ_BLOCKS = 13
COL_Q, COL_K, COL_V, COL_ZA, COL_XIN, COL_BG, COL_CG, COL_ZC, COL_UF, COL_ZF, COL_GL = range(11)

LANES = 128
SUBLANES = 8
VMEM_LIMIT_BYTES = 56 * 1024 * 1024
MOD_ROWS_PAD = 8
ATTN_SUB_ROWS = 256


def _cparams(sem):
    return pltpu.CompilerParams(dimension_semantics=sem, vmem_limit_bytes=VMEM_LIMIT_BYTES)


def _mod_kernel(cond_ref, w_ref, b_ref, o_ref):
    cond = cond_ref[...]
    a = cond * jax.nn.sigmoid(cond)
    o_ref[...] = jnp.dot(a, w_ref[...], preferred_element_type=F32,
                         precision=lax.Precision.HIGHEST) + b_ref[...]


def _modulation(cond, w_mod, b_mod):
    depth, d, w3 = w_mod.shape
    rows = cond.shape[0]
    tn = d
    return pl.pallas_call(
        _mod_kernel,
        grid=(depth, w3 // tn),
        in_specs=[
            pl.BlockSpec((rows, d), lambda l, j: (0, 0)),
            pl.BlockSpec((None, d, tn), lambda l, j: (l, 0, j)),
            pl.BlockSpec((None, 1, tn), lambda l, j: (l, 0, j)),
        ],
        out_specs=pl.BlockSpec((None, rows, tn), lambda l, j: (l, 0, j)),
        out_shape=jax.ShapeDtypeStruct((depth, rows, w3), F32),
        compiler_params=_cparams(("parallel", "parallel")),
        name="modulation",
    )(cond, w_mod, b_mod.reshape(depth, 1, w3))


def _swap_halves16(a):
    lane = lax.broadcasted_iota(jnp.int32, a.shape, 1)
    first = (lane % 32) < 16
    return jnp.where(first, pltpu.roll(a, LANES - 16, axis=1), pltpu.roll(a, 16, axis=1))


def _proj_kernel(x_ref, sh_ref, sc_ref, nw_ref, w_ref, *rest, rope, n_rope_blocks):
    if rope:
        cos_ref, sin_ref, o_ref, h_ref = rest
    else:
        o_ref, h_ref = rest
    j = pl.program_id(2)

    @pl.when(j == 0)
    def _():
        x = x_ref[0]
        y = x * lax.rsqrt(jnp.mean(x * x, axis=-1, keepdims=True) + NORM_EPS)
        h = (y * nw_ref[...]) * (1.0 + sc_ref[0]) + sh_ref[0]
        h_ref[...] = h.astype(BF16)

    acc = jnp.dot(h_ref[...], w_ref[...], preferred_element_type=F32)
    tn = acc.shape[1]

    if rope:
        @pl.when(j < n_rope_blocks)
        def _():
            cos = cos_ref[...]
            sin = sin_ref[...]
            for hh in range(tn // LANES):
                a = acc[:, hh * LANES:(hh + 1) * LANES]
                r = a * cos + _swap_halves16(a) * sin
                o_ref[0, :, hh * LANES:(hh + 1) * LANES] = r.astype(BF16)

        @pl.when(j >= n_rope_blocks)
        def _():
            o_ref[0] = acc.astype(BF16)
    else:
        scale = jnp.where(j == 0, 1.0 / math.sqrt(HEAD_DIM), 1.0).astype(F32)
        o_ref[0] = (acc * scale).astype(BF16)


def _proj(x, mod3, mod_row, norm_w, w_in_b, layer, rope_tabs, tm):
    bx, sx, d = x.shape
    tn = d
    nj = w_in_b.shape[2] // tn
    rope = rope_tabs is not None
    in_specs = [
        pl.BlockSpec((1, tm, d), lambda b, i, j: (b, i, 0)),
        pl.BlockSpec((1, 1, d), lambda b, i, j: (mod_row(b), 0, 0)),
        pl.BlockSpec((1, 1, d), lambda b, i, j: (mod_row(b), 0, 1)),
        pl.BlockSpec((None, 1, d), lambda b, i, j: (layer, 0, 0)),
        pl.BlockSpec((None, d, tn), lambda b, i, j: (layer, 0, j)),
    ]
    args = [x, mod3, mod3, norm_w, w_in_b]
    n_rope_blocks = 2 * (N_HEADS * V_DIM) // tn
    if rope:
        cos_t, sin_t = rope_tabs
        tab_spec = pl.BlockSpec(
            (None, tm, LANES),
            lambda b, i, j: (jnp.minimum(j // (n_rope_blocks // 2), 1), i, 0))
        in_specs += [tab_spec, tab_spec]
        args += [cos_t, sin_t]
    return pl.pallas_call(
        functools.partial(_proj_kernel, rope=rope, n_rope_blocks=n_rope_blocks),
        grid=(bx, sx // tm, nj),
        in_specs=in_specs,
        out_specs=pl.BlockSpec((1, tm, tn), lambda b, i, j: (b, i, j)),
        out_shape=jax.ShapeDtypeStruct((bx, sx, nj * tn), BF16),
        scratch_shapes=[pltpu.VMEM((tm, d), BF16)],
        compiler_params=_cparams(("parallel", "parallel", "arbitrary")),
        name="proj_rope" if rope else "proj",
    )(*args)


def _attn_kernel(lq_ref, sw_ref, q_ref, k_ref, v_ref, *rest, has_ctx, lam_init):
    if has_ctx:
        kc_ref, vc_ref, o_ref, kall_ref, vext_ref = rest
    else:
        o_ref, kall_ref, vext_ref = rest
    n_lat = k_ref.shape[1]
    n_all = kall_ref.shape[0]

    @pl.when(pl.program_id(2) == 0)
    def _():
        kall_ref[0:n_lat, :] = k_ref[0]
        vext_ref[0:n_lat, 0:V_DIM] = v_ref[0]
        if has_ctx:
            kall_ref[n_lat:n_all, :] = kc_ref[0]
            vext_ref[n_lat:n_all, 0:V_DIM] = vc_ref[0]
        ones_lane = lax.broadcasted_iota(jnp.int32, (n_all, V_DIM), 1) == 0
        vext_ref[:, V_DIM:2 * V_DIM] = jnp.where(ones_lane, 1.0, 0.0).astype(BF16)

    lq = lq_ref[...].astype(F32)
    lam = (jnp.exp(jnp.sum(lq[0:1] * lq[1:2], axis=-1, keepdims=True))
           - jnp.exp(jnp.sum(lq[2:3] * lq[3:4], axis=-1, keepdims=True)) + lam_init)

    tq = q_ref.shape[1]
    sub = min(tq, ATTN_SUB_ROWS)
    lane = lax.broadcasted_iota(jnp.int32, (sub, V_DIM), 1)

    def scores(j):
        q = q_ref[0, j * sub:(j + 1) * sub, :]
        zero = jnp.zeros_like(q)
        qq = jnp.concatenate([jnp.where(lane < HEAD_DIM, q, zero),
                              jnp.where(lane >= HEAD_DIM, q, zero)], axis=0)
        return lax.dot_general(qq, kall_ref[...], (((1,), (1,)), ((), ())),
                               preferred_element_type=F32)

    def finish(j, s):
        m = jnp.max(s, axis=-1, keepdims=True)
        p = jnp.exp(s - m).astype(BF16)
        nd = jnp.dot(p, vext_ref[...], preferred_element_type=F32)
        o = nd[:, 0:V_DIM] / nd[:, V_DIM:V_DIM + 1]
        o = o[:sub] - lam * o[sub:]
        y = o * lax.rsqrt(jnp.mean(o * o, axis=-1, keepdims=True) + SUBLN_EPS)
        o_ref[0, j * sub:(j + 1) * sub, :] = ((y * sw_ref[...]) * (1.0 - lam_init)).astype(BF16)

    n_sub = tq // sub
    s_cur = scores(0)
    for j in range(n_sub):
        s_next = scores(j + 1) if j + 1 < n_sub else None
        finish(j, s_cur)
        s_cur = s_next


def _attention(lambda_qk, subln_w, layer, lam_init, p_q, p_ctx, tq):
    b, s, _ = p_q.shape
    nh = N_HEADS
    d_blocks = nh
    has_ctx = p_ctx is not None
    in_specs = [
        pl.BlockSpec((None, 4, HEAD_DIM), lambda bb, h, i: (layer, 0, 0)),
        pl.BlockSpec((None, 1, V_DIM), lambda bb, h, i: (layer, 0, 0)),
        pl.BlockSpec((1, tq, V_DIM), lambda bb, h, i: (bb, i, COL_Q * d_blocks + h)),
        pl.BlockSpec((1, s, V_DIM), lambda bb, h, i: (bb, 0, COL_K * d_blocks + h)),
        pl.BlockSpec((1, s, V_DIM), lambda bb, h, i: (bb, 0, COL_V * d_blocks + h)),
    ]
    args = [lambda_qk, subln_w, p_q, p_q, p_q]
    n_keys = s
    if has_ctx:
        lc = p_ctx.shape[1]
        n_keys = s + lc
        in_specs += [
            pl.BlockSpec((1, lc, V_DIM), lambda bb, h, i: (bb, 0, COL_K * d_blocks + h)),
            pl.BlockSpec((1, lc, V_DIM), lambda bb, h, i: (bb, 0, COL_V * d_blocks + h)),
        ]
        args += [p_ctx, p_ctx]
    return pl.pallas_call(
        functools.partial(_attn_kernel, has_ctx=has_ctx, lam_init=lam_init),
        grid=(b, nh, s // tq),
        in_specs=in_specs,
        out_specs=pl.BlockSpec((1, tq, V_DIM), lambda bb, h, i: (bb, i, h)),
        out_shape=jax.ShapeDtypeStruct((b, s, nh * V_DIM), BF16),
        scratch_shapes=[pltpu.VMEM((n_keys, V_DIM), BF16), pltpu.VMEM((n_keys, 2 * V_DIM), BF16)],
        compiler_params=_cparams(("parallel", "parallel", "arbitrary")),
        name="attn_ctx" if has_ctx else "attn",
    )(*args)


def _fourier_kernel(u_ref, cs_ref, w_ref, o_ref, ab_ref):
    r = pl.program_id(1)
    n = u_ref.shape[1]

    @pl.when(r == 0)
    def _():
        for g in range(u_ref.shape[2] // FGROUP_DIM):
            cols = slice(g * FGROUP_DIM, (g + 1) * FGROUP_DIM)
            ab = jnp.dot(u_ref[0, :, cols], cs_ref[...], preferred_element_type=F32)
            ab_ref[0:n, cols] = ab[:, :FGROUP_DIM].astype(BF16)
            ab_ref[n:2 * n, cols] = ab[:, FGROUP_DIM:].astype(BF16)

    o_ref[0] = jnp.dot(w_ref[...], ab_ref[...], preferred_element_type=F32).astype(BF16)


def _dft_cos_sin(n):
    k = jnp.arange(n, dtype=jnp.int32)
    ang = ((k[:, None] * k[None, :]) % n).astype(F32) * (2.0 * math.pi / n)
    scale = 1.0 / math.sqrt(n)
    return jnp.cos(ang) * scale, jnp.sin(ang) * scale


def _fourier(p, cs_chan, w_pos, tr):
    b, n, _ = p.shape
    d = p.shape[2] // N_PROJ_BLOCKS
    return pl.pallas_call(
        _fourier_kernel,
        grid=(b, n // tr),
        in_specs=[
            pl.BlockSpec((1, n, d), lambda bb, r: (bb, 0, COL_UF)),
            pl.BlockSpec((FGROUP_DIM, 2 * FGROUP_DIM), lambda bb, r: (0, 0)),
            pl.BlockSpec((tr, 2 * n), lambda bb, r: (r, 0)),
        ],
        out_specs=pl.BlockSpec((1, tr, d), lambda bb, r: (bb, r, 0)),
        out_shape=jax.ShapeDtypeStruct((b, n, d), BF16),
        scratch_shapes=[pltpu.VMEM((2 * n, d), BF16)],
        compiler_params=_cparams(("parallel", "arbitrary")),
        name="fourier",
    )(p, cs_chan, w_pos)


def _silu(z):
    return z * jax.nn.sigmoid(z)


def _merge_kernel(ya_ref, yf_ref, za_ref, xin_ref, bg_ref, cg_ref, zc_ref, zf_ref,
                  ga_ref, gc_ref, gf_ref, xin_p_ref, cg_p_ref, xin_n_ref, cg_n_ref,
                  x_ref, gate_ref, cw_ref, wa_ref, wc_ref, wf_ref, wo_ref, o_ref, u_ref):
    i = pl.program_id(1)
    tm = x_ref.shape[1]
    pad = SUBLANES

    u_ref[pad:pad + tm, :] = cg_ref[0].astype(F32) * xin_ref[0].astype(F32)
    up = cg_p_ref[0].astype(F32) * xin_p_ref[0].astype(F32)
    un = cg_n_ref[0].astype(F32) * xin_n_ref[0].astype(F32)
    u_ref[0:pad, :] = jnp.where(i == 0, 0.0, up)
    u_ref[pad + tm:2 * pad + tm, :] = jnp.where(i == pl.num_programs(1) - 1, 0.0, un)
    cw = cw_ref[...]
    conv = (cw[0:1] * u_ref[pad - 1:pad - 1 + tm, :] + cw[1:2] * u_ref[pad:pad + tm, :]
            + cw[2:3] * u_ref[pad + 1:pad + 1 + tm, :])
    y_c = bg_ref[0].astype(F32) * conv

    def branch(y, z_ref, w_ref, g_ref):
        t = jnp.dot((y * _silu(z_ref[0].astype(F32))).astype(BF16), w_ref[...],
                    preferred_element_type=F32)
        return jax.nn.sigmoid(g_ref[0].astype(F32)) * t

    merged = (branch(ya_ref[0].astype(F32), za_ref, wa_ref, ga_ref)
              + branch(y_c, zc_ref, wc_ref, gc_ref)
              + branch(yf_ref[0].astype(F32), zf_ref, wf_ref, gf_ref))
    out = jnp.dot(merged.astype(BF16), wo_ref[...], preferred_element_type=F32)
    o_ref[0] = x_ref[0] + gate_ref[0] * out


def _merge(x, p, ya, yf, mod3, mod_row, conv_w, wa, wc, wf, wo, layer, tm):
    bx, sx, d = x.shape
    nb = tm // SUBLANES
    last_halo = sx // SUBLANES - 1

    def col(cb):
        return pl.BlockSpec((1, tm, d), lambda b, i: (b, i, cb))

    def halo_prev(cb):
        return pl.BlockSpec((1, SUBLANES, d), lambda b, i: (b, jnp.maximum(i * nb - 1, 0), cb))

    def halo_next(cb):
        return pl.BlockSpec((1, SUBLANES, d), lambda b, i: (b, jnp.minimum((i + 1) * nb, last_halo), cb))

    def weight():
        return pl.BlockSpec((None, d, d), lambda b, i: (layer, 0, 0))

    in_specs = [
        col(0), col(0),
        col(COL_ZA), col(COL_XIN), col(COL_BG), col(COL_CG), col(COL_ZC), col(COL_ZF),
        col(COL_GL), col(COL_GL + 1), col(COL_GL + 2),
        halo_prev(COL_XIN), halo_prev(COL_CG), halo_next(COL_XIN), halo_next(COL_CG),
        col(0),
        pl.BlockSpec((1, 1, d), lambda b, i: (mod_row(b), 0, 2)),
        pl.BlockSpec((None, 3, d), lambda b, i: (layer, 0, 0)),
        weight(), weight(), weight(), weight(),
    ]
    return pl.pallas_call(
        _merge_kernel,
        grid=(bx, sx // tm),
        in_specs=in_specs,
        out_specs=pl.BlockSpec((1, tm, d), lambda b, i: (b, i, 0)),
        out_shape=jax.ShapeDtypeStruct((bx, sx, d), F32),
        scratch_shapes=[pltpu.VMEM((tm + 2 * SUBLANES, d), F32)],
        compiler_params=_cparams(("parallel", "parallel")),
        name="merge",
    )(ya, yf, p, p, p, p, p, p, p, p, p, p, p, p, p, x, mod3, conv_w, wa, wc, wf, wo)


def _final_norm_kernel(x_ref, w_ref, o_ref):
    x = x_ref[...]
    y = x * lax.rsqrt(jnp.mean(x * x, axis=-1, keepdims=True) + NORM_EPS)
    o_ref[...] = y * w_ref[...]


def _final_norm(x, w, tm):
    b, s, d = x.shape
    x2 = x.reshape(b * s, d)
    out = pl.pallas_call(
        _final_norm_kernel,
        grid=(b * s // tm,),
        in_specs=[pl.BlockSpec((tm, d), lambda i: (i, 0)), pl.BlockSpec((1, d), lambda i: (0, 0))],
        out_specs=pl.BlockSpec((tm, d), lambda i: (i, 0)),
        out_shape=jax.ShapeDtypeStruct((b * s, d), F32),
        compiler_params=_cparams(("parallel",)),
        name="final_norm",
    )(x2, w.reshape(1, d))
    return out.reshape(b, s, d)


def _rope_tables(n_tokens):
    axis_dim = HEAD_DIM // 2
    rows = n_tokens // GRID_W
    row = jnp.repeat(jnp.arange(rows), GRID_W).astype(F32)
    col = jnp.tile(jnp.arange(GRID_W), rows).astype(F32)
    inv_freq = ROPE_BASE ** (-jnp.arange(0, axis_dim, 2, dtype=F32) / axis_dim)
    ang_r = row[:, None] * inv_freq
    ang_c = col[:, None] * inv_freq
    ang = jnp.concatenate([ang_r, ang_r, ang_c, ang_c], axis=-1)
    ang = jnp.concatenate([ang, ang], axis=-1)
    half = axis_dim // 2
    sign = jnp.where((jnp.arange(V_DIM) % axis_dim) < half, -1.0, 1.0).astype(F32)
    cos = jnp.cos(ang)
    sin = jnp.sin(ang) * sign
    qs = 1.0 / math.sqrt(HEAD_DIM)
    return jnp.stack([cos * qs, cos]), jnp.stack([sin * qs, sin])


def _fourier_tables(n):
    cn, sn = _dft_cos_sin(n)
    return jnp.concatenate([cn, -sn], axis=1).astype(BF16)


def _pick_tile(n, target):
    t = min(n, target)
    while n % t:
        t //= 2
    return t


def kernel(x, c, ctx, c_ctx, norm_w, w_mod, b_mod, w_in, lambda_qk, subln_w, conv_w,
           w_attn_o, w_conv_o, w_four_o, w_out, final_norm_w):
    b, s, d = x.shape
    lc = ctx.shape[1]
    depth = w_in.shape[0]
    assert d == N_HEADS * V_DIM and w_in.shape[2] == N_PROJ_BLOCKS * d
    assert s % GRID_W == 0 and s % LANES == 0 and lc % SUBLANES == 0

    pad = (-(b + 1)) % MOD_ROWS_PAD
    cond = jnp.concatenate([c, c_ctx[None, :], jnp.zeros((pad, d), F32)], axis=0)
    mod = _modulation(cond, w_mod, b_mod)
    n_rows = cond.shape[0]

    norm_w = norm_w.reshape(depth, 1, d)
    subln_w = subln_w.reshape(depth, 1, V_DIM)
    w_in_b = w_in.astype(BF16)
    wa_b, wc_b, wf_b, wo_b = (w.astype(BF16) for w in (w_attn_o, w_conv_o, w_four_o, w_out))

    rope_tabs = _rope_tables(s)
    cc, sc = _dft_cos_sin(FGROUP_DIM)
    cs_chan = jnp.concatenate([cc, sc], axis=1).astype(BF16)
    w_pos_lat = _fourier_tables(s)
    w_pos_ctx = _fourier_tables(lc)

    ctx_flat = ctx.reshape(1, b * lc, d)
    lat_row = lambda bb: bb
    ctx_row = lambda bb: b

    tm_lat = _pick_tile(s, 1024)
    tm_ctx = _pick_tile(b * lc, 1024)
    tq_lat = _pick_tile(s, 1024)
    tr_lat = _pick_tile(s, 512)
    tmm_lat = _pick_tile(s, 256)
    tmm_ctx = _pick_tile(lc, 256)

    for l in range(depth):
        last = l == depth - 1
        lam_init = 0.8 - 0.6 * math.exp(-0.3 * l)
        mod3 = mod[l].reshape(n_rows, 1, 3 * d)

        p_lat = _proj(x, mod3, lat_row, norm_w, w_in_b, l, rope_tabs, tm_lat)
        p_ctx = _proj(ctx_flat, mod3, ctx_row, norm_w, w_in_b, l, None, tm_ctx).reshape(b, lc, -1)

        ya = _attention(lambda_qk, subln_w, l, lam_init, p_lat, p_ctx, tq_lat)
        yf = _fourier(p_lat, cs_chan, w_pos_lat, tr_lat)
        x_new = _merge(x, p_lat, ya, yf, mod3, lat_row, conv_w, wa_b, wc_b, wf_b, wo_b, l, tmm_lat)

        if not last:
            yac = _attention(lambda_qk, subln_w, l, lam_init, p_ctx, None, lc)
            yfc = _fourier(p_ctx, cs_chan, w_pos_ctx, lc)
            ctx3 = ctx_flat.reshape(b, lc, d)
            ctx_flat = _merge(ctx3, p_ctx, yac, yfc, mod3, ctx_row, conv_w, wa_b, wc_b, wf_b, wo_b,
                              l, tmm_ctx).reshape(1, b * lc, d)
        x = x_new

    return _final_norm(x, final_norm_w, _pick_tile(b * s, 1024))
```

```python
import functools
import math

import jax
import jax.numpy as jnp
from jax import lax
from jax.experimental import pallas as pl
from jax.experimental.pallas import tpu as pltpu

F32 = jnp.float32
BF16 = jnp.bfloat16

N_HEADS = 8
HEAD_DIM = 64
V_DIM = 2 * HEAD_DIM
FGROUP_DIM = 128
GRID_W = 64
ROPE_BASE = 10000.0
NORM_EPS = 1e-6
SUBLN_EPS = 1e-5
N_PROJ_BLOCKS = 13
COL_Q, COL_K, COL_V, COL_ZA, COL_XIN, COL_BG, COL_CG, COL_ZC, COL_UF, COL_ZF, COL_GL = range(11)

LANES = 128
SUBLANES = 8
VMEM_LIMIT_BYTES = 56 * 1024 * 1024
MOD_ROWS_PAD = 8
ATTN_SUB_ROWS = 256


def _cparams(sem):
    return pltpu.CompilerParams(dimension_semantics=sem, vmem_limit_bytes=VMEM_LIMIT_BYTES)


def _mod_kernel(cond_ref, w_ref, b_ref, o_ref):
    cond = cond_ref[...]
    a = cond * jax.nn.sigmoid(cond)
    o_ref[...] = jnp.dot(a, w_ref[...], preferred_element_type=F32,
                         precision=lax.Precision.HIGHEST) + b_ref[...]


def _modulation(cond, w_mod, b_mod):
    depth, d, w3 = w_mod.shape
    rows = cond.shape[0]
    tn = d
    return pl.pallas_call(
        _mod_kernel,
        grid=(depth, w3 // tn),
        in_specs=[
            pl.BlockSpec((rows, d), lambda l, j: (0, 0)),
            pl.BlockSpec((None, d, tn), lambda l, j: (l, 0, j)),
            pl.BlockSpec((None, 1, tn), lambda l, j: (l, 0, j)),
        ],
        out_specs=pl.BlockSpec((None, rows, tn), lambda l, j: (l, 0, j)),
        out_shape=jax.ShapeDtypeStruct((depth, rows, w3), F32),
        compiler_params=_cparams(("parallel", "parallel")),
        name="modulation",
    )(cond, w_mod, b_mod.reshape(depth, 1, w3))


def _swap_halves16(a):
    lane = lax.broadcasted_iota(jnp.int32, a.shape, 1)
    first = (lane % 32) < 16
    return jnp.where(first, pltpu.roll(a, LANES - 16, axis=1), pltpu.roll(a, 16, axis=1))


def _proj_kernel(x_ref, sh_ref, sc_ref, nw_ref, w_ref, *rest, rope, scale_first):
    if rope:
        cos_ref, sin_ref, o_ref, h_ref = rest
    else:
        o_ref, h_ref = rest
    j = pl.program_id(2)

    @pl.when(j == 0)
    def _():
        x = x_ref[0]
        y = x * lax.rsqrt(jnp.mean(x * x, axis=-1, keepdims=True) + NORM_EPS)
        h = (y * nw_ref[...]) * (1.0 + sc_ref[0]) + sh_ref[0]
        h_ref[...] = h.astype(BF16)

    acc = jnp.dot(h_ref[...], w_ref[...], preferred_element_type=F32)
    tn = acc.shape[1]

    if rope:
        cos = cos_ref[...]
        sin = sin_ref[...]
        for hh in range(tn // LANES):
            a = acc[:, hh * LANES:(hh + 1) * LANES]
            r = a * cos + _swap_halves16(a) * sin
            o_ref[0, :, hh * LANES:(hh + 1) * LANES] = r.astype(BF16)
    elif scale_first:
        scale = jnp.where(j == 0, 1.0 / math.sqrt(HEAD_DIM), 1.0).astype(F32)
        o_ref[0] = (acc * scale).astype(BF16)
    else:
        o_ref[0] = acc.astype(BF16)


def _proj(x, mod3, mod_row, norm_w, w_in_b, layer, col0, ncols, rope_tabs, tm):
    bx, sx, d = x.shape
    tn = d
    rope = rope_tabs is not None
    assert not rope or (col0, ncols) == (COL_Q, 2)
    in_specs = [
        pl.BlockSpec((1, tm, d), lambda b, i, j: (b, i, 0)),
        pl.BlockSpec((1, 1, d), lambda b, i, j: (mod_row(b), 0, 0)),
        pl.BlockSpec((1, 1, d), lambda b, i, j: (mod_row(b), 0, 1)),
        pl.BlockSpec((None, 1, d), lambda b, i, j: (layer, 0, 0)),
        pl.BlockSpec((None, d, tn), lambda b, i, j: (layer, 0, col0 + j)),
    ]
    args = [x, mod3, mod3, norm_w, w_in_b]
    if rope:
        cos_t, sin_t = rope_tabs
        tab_spec = pl.BlockSpec((None, tm, LANES), lambda b, i, j: (j, i, 0))
        in_specs += [tab_spec, tab_spec]
        args += [cos_t, sin_t]
    return pl.pallas_call(
        functools.partial(_proj_kernel, rope=rope, scale_first=(col0 == COL_Q)),
        grid=(bx, sx // tm, ncols),
        in_specs=in_specs,
        out_specs=pl.BlockSpec((1, tm, tn), lambda b, i, j: (b, i, j)),
        out_shape=jax.ShapeDtypeStruct((bx, sx, ncols * tn), BF16),
        scratch_shapes=[pltpu.VMEM((tm, d), BF16)],
        compiler_params=_cparams(("parallel", "parallel", "arbitrary")),
        name="proj_rope" if rope else "proj",
    )(*args)


def _attn_kernel(lq_ref, sw_ref, q_ref, k_ref, v_ref, *rest, has_ctx, lam_init):
    if has_ctx:
        kc_ref, vc_ref, o_ref, kall_ref, vext_ref = rest
    else:
        o_ref, kall_ref, vext_ref = rest
    n_lat = k_ref.shape[1]
    n_all = kall_ref.shape[0]

    @pl.when(pl.program_id(2) == 0)
    def _():
        kall_ref[0:n_lat, :] = k_ref[0]
        vext_ref[0:n_lat, 0:V_DIM] = v_ref[0]
        if has_ctx:
            kall_ref[n_lat:n_all, :] = kc_ref[0]
            vext_ref[n_lat:n_all, 0:V_DIM] = vc_ref[0]
        ones_lane = lax.broadcasted_iota(jnp.int32, (n_all, V_DIM), 1) == 0
        vext_ref[:, V_DIM:2 * V_DIM] = jnp.where(ones_lane, 1.0, 0.0).astype(BF16)

    lq = lq_ref[...].astype(F32)
    lam = (jnp.exp(jnp.sum(lq[0:1] * lq[1:2], axis=-1, keepdims=True))
           - jnp.exp(jnp.sum(lq[2:3] * lq[3:4], axis=-1, keepdims=True)) + lam_init)

    tq = q_ref.shape[1]
    sub = min(tq, ATTN_SUB_ROWS)
    lane = lax.broadcasted_iota(jnp.int32, (sub, V_DIM), 1)

    def scores(j):
        q = q_ref[0, j * sub:(j + 1) * sub, :]
        zero = jnp.zeros_like(q)
        qq = jnp.concatenate([jnp.where(lane < HEAD_DIM, q, zero),
                              jnp.where(lane >= HEAD_DIM, q, zero)], axis=0)
        return lax.dot_general(qq, kall_ref[...], (((1,), (1,)), ((), ())),
                               preferred_element_type=F32)

    def finish(j, s):
        m = jnp.max(s, axis=-1, keepdims=True)
        p = jnp.exp(s - m).astype(BF16)
        nd = jnp.dot(p, vext_ref[...], preferred_element_type=F32)
        o = nd[:, 0:V_DIM] / nd[:, V_DIM:V_DIM + 1]
        o = o[:sub] - lam * o[sub:]
        y = o * lax.rsqrt(jnp.mean(o * o, axis=-1, keepdims=True) + SUBLN_EPS)
        o_ref[0, j * sub:(j + 1) * sub, :] = ((y * sw_ref[...]) * (1.0 - lam_init)).astype(BF16)

    n_sub = tq // sub
    s_cur = scores(0)
    for j in range(n_sub):
        s_next = scores(j + 1) if j + 1 < n_sub else None
        finish(j, s_cur)
        s_cur = s_next


def _attention(lambda_qk, subln_w, layer, lam_init, q, k, v, kc, vc, tq):
    b, s, _ = q[0].shape
    nh = N_HEADS
    has_ctx = kc is not None

    def head_cols(rows, col):
        return pl.BlockSpec((1, rows, V_DIM), lambda bb, h, i: (bb, 0, col * nh + h))

    in_specs = [
        pl.BlockSpec((None, 4, HEAD_DIM), lambda bb, h, i: (layer, 0, 0)),
        pl.BlockSpec((None, 1, V_DIM), lambda bb, h, i: (layer, 0, 0)),
        pl.BlockSpec((1, tq, V_DIM), lambda bb, h, i: (bb, i, q[1] * nh + h)),
        head_cols(s, k[1]),
        head_cols(s, v[1]),
    ]
    args = [lambda_qk, subln_w, q[0], k[0], v[0]]
    n_keys = s
    if has_ctx:
        lc = kc[0].shape[1]
        n_keys = s + lc
        in_specs += [head_cols(lc, kc[1]), head_cols(lc, vc[1])]
        args += [kc[0], vc[0]]
    return pl.pallas_call(
        functools.partial(_attn_kernel, has_ctx=has_ctx, lam_init=lam_init),
        grid=(b, nh, s // tq),
        in_specs=in_specs,
        out_specs=pl.BlockSpec((1, tq, V_DIM), lambda bb, h, i: (bb, i, h)),
        out_shape=jax.ShapeDtypeStruct((b, s, nh * V_DIM), BF16),
        scratch_shapes=[pltpu.VMEM((n_keys, V_DIM), BF16), pltpu.VMEM((n_keys, 2 * V_DIM), BF16)],
        compiler_params=_cparams(("parallel", "parallel", "arbitrary")),
        name="attn_ctx" if has_ctx else "attn",
    )(*args)


def _fourier_kernel(u_ref, cs_ref, w_ref, o_ref, ab_ref):
    r = pl.program_id(1)
    n = u_ref.shape[1]

    @pl.when(r == 0)
    def _():
        for g in range(u_ref.shape[2] // FGROUP_DIM):
            cols = slice(g * FGROUP_DIM, (g + 1) * FGROUP_DIM)
            ab = jnp.dot(u_ref[0, :, cols], cs_ref[...], preferred_element_type=F32)
            ab_ref[0:n, cols] = ab[:, :FGROUP_DIM].astype(BF16)
            ab_ref[n:2 * n, cols] = ab[:, FGROUP_DIM:].astype(BF16)

    o_ref[0] = jnp.dot(w_ref[...], ab_ref[...], preferred_element_type=F32).astype(BF16)


def _dft_cos_sin(n):
    k = jnp.arange(n, dtype=jnp.int32)
    ang = ((k[:, None] * k[None, :]) % n).astype(F32) * (2.0 * math.pi / n)
    scale = 1.0 / math.sqrt(n)
    return jnp.cos(ang) * scale, jnp.sin(ang) * scale


def _fourier(p, col0, cs_chan, w_pos, tr):
    b, n, _ = p.shape
    d = N_HEADS * V_DIM
    return pl.pallas_call(
        _fourier_kernel,
        grid=(b, n // tr),
        in_specs=[
            pl.BlockSpec((1, n, d), lambda bb, r: (bb, 0, COL_UF - col0)),
            pl.BlockSpec((FGROUP_DIM, 2 * FGROUP_DIM), lambda bb, r: (0, 0)),
            pl.BlockSpec((tr, 2 * n), lambda bb, r: (r, 0)),
        ],
        out_specs=pl.BlockSpec((1, tr, d), lambda bb, r: (bb, r, 0)),
        out_shape=jax.ShapeDtypeStruct((b, n, d), BF16),
        scratch_shapes=[pltpu.VMEM((2 * n, d), BF16)],
        compiler_params=_cparams(("parallel", "arbitrary")),
        name="fourier",
    )(p, cs_chan, w_pos)


def _silu(z):
    return z * jax.nn.sigmoid(z)


def _merge_kernel(ya_ref, yf_ref, za_ref, xin_ref, bg_ref, cg_ref, zc_ref, zf_ref,
                  ga_ref, gc_ref, gf_ref, xin_p_ref, cg_p_ref, xin_n_ref, cg_n_ref,
                  x_ref, gate_ref, cw_ref, wa_ref, wc_ref, wf_ref, wo_ref, o_ref, u_ref):
    i = pl.program_id(1)
    tm = x_ref.shape[1]
    pad = SUBLANES

    u_ref[pad:pad + tm, :] = cg_ref[0].astype(F32) * xin_ref[0].astype(F32)
    up = cg_p_ref[0].astype(F32) * xin_p_ref[0].astype(F32)
    un = cg_n_ref[0].astype(F32) * xin_n_ref[0].astype(F32)
    u_ref[0:pad, :] = jnp.where(i == 0, 0.0, up)
    u_ref[pad + tm:2 * pad + tm, :] = jnp.where(i == pl.num_programs(1) - 1, 0.0, un)
    cw = cw_ref[...]
    conv = (cw[0:1] * u_ref[pad - 1:pad - 1 + tm, :] + cw[1:2] * u_ref[pad:pad + tm, :]
            + cw[2:3] * u_ref[pad + 1:pad + 1 + tm, :])
    y_c = bg_ref[0].astype(F32) * conv

    def branch(y, z_ref, w_ref, g_ref):
        t = jnp.dot((y * _silu(z_ref[0].astype(F32))).astype(BF16), w_ref[...],
                    preferred_element_type=F32)
        return jax.nn.sigmoid(g_ref[0].astype(F32)) * t

    merged = (branch(ya_ref[0].astype(F32), za_ref, wa_ref, ga_ref)
              + branch(y_c, zc_ref, wc_ref, gc_ref)
              + branch(yf_ref[0].astype(F32), zf_ref, wf_ref, gf_ref))
    out = jnp.dot(merged.astype(BF16), wo_ref[...], preferred_element_type=F32)
    o_ref[0] = x_ref[0] + gate_ref[0] * out


def _merge(x, p, col0, ya, yf, mod3, mod_row, conv_w, wa, wc, wf, wo, layer, tm):
    bx, sx, d = x.shape
    nb = tm // SUBLANES
    last_halo = sx // SUBLANES - 1

    def whole(b, i):
        return (b, i, 0)

    def col(cb):
        return pl.BlockSpec((1, tm, d), lambda b, i: (b, i, cb - col0))

    def halo_prev(cb):
        return pl.BlockSpec((1, SUBLANES, d),
                            lambda b, i: (b, jnp.maximum(i * nb - 1, 0), cb - col0))

    def halo_next(cb):
        return pl.BlockSpec((1, SUBLANES, d),
                            lambda b, i: (b, jnp.minimum((i + 1) * nb, last_halo), cb - col0))

    def weight():
        return pl.BlockSpec((None, d, d), lambda b, i: (layer, 0, 0))

    in_specs = [
        pl.BlockSpec((1, tm, d), whole), pl.BlockSpec((1, tm, d), whole),
        col(COL_ZA), col(COL_XIN), col(COL_BG), col(COL_CG), col(COL_ZC), col(COL_ZF),
        col(COL_GL), col(COL_GL + 1), col(COL_GL + 2),
        halo_prev(COL_XIN), halo_prev(COL_CG), halo_next(COL_XIN), halo_next(COL_CG),
        pl.BlockSpec((1, tm, d), whole),
        pl.BlockSpec((1, 1, d), lambda b, i: (mod_row(b), 0, 2)),
        pl.BlockSpec((None, 3, d), lambda b, i: (layer, 0, 0)),
        weight(), weight(), weight(), weight(),
    ]
    return pl.pallas_call(
        _merge_kernel,
        grid=(bx, sx // tm),
        in_specs=in_specs,
        out_specs=pl.BlockSpec((1, tm, d), lambda b, i: (b, i, 0)),
        out_shape=jax.ShapeDtypeStruct((bx, sx, d), F32),
        scratch_shapes=[pltpu.VMEM((tm + 2 * SUBLANES, d), F32)],
        compiler_params=_cparams(("parallel", "parallel")),
        name="merge",
    )(ya, yf, p, p, p, p, p, p, p, p, p, p, p, p, p, x, mod3, conv_w, wa, wc, wf, wo)


def _final_norm_kernel(x_ref, w_ref, o_ref):
    x = x_ref[...]
    y = x * lax.rsqrt(jnp.mean(x * x, axis=-1, keepdims=True) + NORM_EPS)
    o_ref[...] = y * w_ref[...]


def _final_norm(x, w, tm):
    b, s, d = x.shape
    x2 = x.reshape(b * s, d)
    out = pl.pallas_call(
        _final_norm_kernel,
        grid=(b * s // tm,),
        in_specs=[pl.BlockSpec((tm, d), lambda i: (i, 0)), pl.BlockSpec((1, d), lambda i: (0, 0))],
        out_specs=pl.BlockSpec((tm, d), lambda i: (i, 0)),
        out_shape=jax.ShapeDtypeStruct((b * s, d), F32),
        compiler_params=_cparams(("parallel",)),
        name="final_norm",
    )(x2, w.reshape(1, d))
    return out.reshape(b, s, d)


def _rope_tables(n_tokens):
    axis_dim = HEAD_DIM // 2
    rows = n_tokens // GRID_W
    row = jnp.repeat(jnp.arange(rows), GRID_W).astype(F32)
    col = jnp.tile(jnp.arange(GRID_W), rows).astype(F32)
    inv_freq = ROPE_BASE ** (-jnp.arange(0, axis_dim, 2, dtype=F32) / axis_dim)
    ang_r = row[:, None] * inv_freq
    ang_c = col[:, None] * inv_freq
    ang = jnp.concatenate([ang_r, ang_r, ang_c, ang_c], axis=-1)
    ang = jnp.concatenate([ang, ang], axis=-1)
    half = axis_dim // 2
    sign = jnp.where((jnp.arange(V_DIM) % axis_dim) < half, -1.0, 1.0).astype(F32)
    cos = jnp.cos(ang)
    sin = jnp.sin(ang) * sign
    qs = 1.0 / math.sqrt(HEAD_DIM)
    return jnp.stack([cos * qs, cos]), jnp.stack([sin * qs, sin])


def _fourier_tables(n):
    cn, sn = _dft_cos_sin(n)
    return jnp.concatenate([cn, -sn], axis=1).astype(BF16)


def _pick_tile(n, target):
    t = min(n, target)
    while n % t:
        t //= 2
    return t


def kernel(x, c, ctx, c_ctx, norm_w, w_mod, b_mod, w_in, lambda_qk, subln_w, conv_w,
           w_attn_o, w_conv_o, w_four_o, w_out, final_norm_w):
    b, s, d = x.shape
    lc = ctx.shape[1]
    depth = w_in.shape[0]
    assert d == N_HEADS * V_DIM and w_in.shape[2] == N_PROJ_BLOCKS * d
    assert s % GRID_W == 0 and s % LANES == 0 and lc % SUBLANES == 0

    pad = (-(b + 1)) % MOD_ROWS_PAD
    cond = jnp.concatenate([c, c_ctx[None, :], jnp.zeros((pad, d), F32)], axis=0)
    mod = _modulation(cond, w_mod, b_mod)
    n_rows = cond.shape[0]

    norm_w = norm_w.reshape(depth, 1, d)
    subln_w = subln_w.reshape(depth, 1, V_DIM)
    w_in_b = w_in.astype(BF16)
    wa_b, wc_b, wf_b, wo_b = (w.astype(BF16) for w in (w_attn_o, w_conv_o, w_four_o, w_out))

    rope_tabs = _rope_tables(s)
    cc, sc = _dft_cos_sin(FGROUP_DIM)
    cs_chan = jnp.concatenate([cc, sc], axis=1).astype(BF16)
    w_pos_lat = _fourier_tables(s)
    w_pos_ctx = _fourier_tables(lc)

    ctx_flat = ctx.reshape(1, b * lc, d)
    lat_row = lambda bb: bb
    ctx_row = lambda bb: b

    tm_lat = _pick_tile(s, 1024)
    tm_ctx = _pick_tile(b * lc, 1024)
    tq_lat = _pick_tile(s, 1024)
    tr_lat = _pick_tile(s, 512)
    tmm_lat = _pick_tile(s, 256)
    tmm_ctx = _pick_tile(lc, 256)

    for l in range(depth):
        last = l == depth - 1
        lam_init = 0.8 - 0.6 * math.exp(-0.3 * l)
        mod3 = mod[l].reshape(n_rows, 1, 3 * d)

        p_qk = _proj(x, mod3, lat_row, norm_w, w_in_b, l, COL_Q, 2, rope_tabs, tm_lat)
        p_lat = _proj(x, mod3, lat_row, norm_w, w_in_b, l, COL_V, N_PROJ_BLOCKS - COL_V, None, tm_lat)
        c0, cn = (COL_K, 2) if last else (COL_Q, N_PROJ_BLOCKS)
        p_ctx = _proj(ctx_flat, mod3, ctx_row, norm_w, w_in_b, l, c0, cn, None, tm_ctx).reshape(b, lc, -1)

        ya = _attention(lambda_qk, subln_w, l, lam_init, (p_qk, COL_Q), (p_qk, COL_K),
                        (p_lat, 0), (p_ctx, COL_K - c0), (p_ctx, COL_V - c0), tq_lat)
        yf = _fourier(p_lat, COL_V, cs_chan, w_pos_lat, tr_lat)
        x_new = _merge(x, p_lat, COL_V, ya, yf, mod3, lat_row, conv_w, wa_b, wc_b, wf_b, wo_b,
                       l, tmm_lat)

        if not last:
            yac = _attention(lambda_qk, subln_w, l, lam_init, (p_ctx, COL_Q), (p_ctx, COL_K),
                             (p_ctx, COL_V), None, None, lc)
            yfc = _fourier(p_ctx, COL_Q, cs_chan, w_pos_ctx, lc)
            ctx3 = ctx_flat.reshape(b, lc, d)
            ctx_flat = _merge(ctx3, p_ctx, COL_Q, yac, yfc, mod3, ctx_row, conv_w, wa_b, wc_b, wf_b,
                              wo_b, l, tmm_ctx).reshape(1, b * lc, d)
        x = x_new

    return _final_norm(x, final_norm_w, _pick_tile(b * s, 1024))
```

```python
import functools
import math

import jax
import jax.numpy as jnp
from jax import lax
from jax.experimental import pallas as pl
from jax.experimental.pallas import tpu as pltpu

F32 = jnp.float32
BF16 = jnp.bfloat16

N_HEADS = 8
HEAD_DIM = 64
V_DIM = 2 * HEAD_DIM
FGROUP_DIM = 128
GRID_W = 64
ROPE_BASE = 10000.0
NORM_EPS = 1e-6
SUBLN_EPS = 1e-5
N_PROJ_BLOCKS = 13
COL_Q, COL_K, COL_V, COL_ZA, COL_XIN, COL_BG, COL_CG, COL_ZC, COL_UF, COL_ZF, COL_GL = range(11)

LANES = 128
SUBLANES = 8
VMEM_LIMIT_BYTES = 56 * 1024 * 1024
MOD_ROWS_PAD = 8
ATTN_SUB_ROWS = 256
Q_SCALE = math.log2(math.e) / math.sqrt(HEAD_DIM)


def _cparams(sem):
    return pltpu.CompilerParams(dimension_semantics=sem, vmem_limit_bytes=VMEM_LIMIT_BYTES)


def _mod_kernel(cond_ref, w_ref, b_ref, o_ref):
    cond = cond_ref[...]
    a = cond * jax.nn.sigmoid(cond)
    o_ref[...] = jnp.dot(a, w_ref[...], preferred_element_type=F32,
                         precision=lax.Precision.HIGHEST) + b_ref[...]


def _modulation(cond, w_mod, b_mod):
    depth, d, w3 = w_mod.shape
    rows = cond.shape[0]
    tn = d
    return pl.pallas_call(
        _mod_kernel,
        grid=(depth, w3 // tn),
        in_specs=[
            pl.BlockSpec((rows, d), lambda l, j: (0, 0)),
            pl.BlockSpec((None, d, tn), lambda l, j: (l, 0, j)),
            pl.BlockSpec((None, 1, tn), lambda l, j: (l, 0, j)),
        ],
        out_specs=pl.BlockSpec((None, rows, tn), lambda l, j: (l, 0, j)),
        out_shape=jax.ShapeDtypeStruct((depth, rows, w3), F32),
        compiler_params=_cparams(("parallel", "parallel")),
        name="modulation",
    )(cond, w_mod, b_mod.reshape(depth, 1, w3))


def _swap_halves16(a):
    lane = lax.broadcasted_iota(jnp.int32, a.shape, 1)
    first = (lane % 32) < 16
    return jnp.where(first, pltpu.roll(a, LANES - 16, axis=1), pltpu.roll(a, 16, axis=1))


def _proj_kernel(x_ref, sh_ref, sc_ref, nw_ref, w_ref, *rest, rope, scale_first):
    if rope:
        cos_ref, sin_ref, o_ref, h_ref = rest
    else:
        o_ref, h_ref = rest
    j = pl.program_id(2)

    @pl.when(j == 0)
    def _():
        x = x_ref[0]
        y = x * lax.rsqrt(jnp.mean(x * x, axis=-1, keepdims=True) + NORM_EPS)
        h = (y * nw_ref[...]) * (1.0 + sc_ref[0]) + sh_ref[0]
        h_ref[...] = h.astype(BF16)

    acc = jnp.dot(h_ref[...], w_ref[...], preferred_element_type=F32)
    tn = acc.shape[1]

    if rope:
        cos = cos_ref[...]
        sin = sin_ref[...]
        for hh in range(tn // LANES):
            a = acc[:, hh * LANES:(hh + 1) * LANES]
            r = a * cos + _swap_halves16(a) * sin
            o_ref[0, :, hh * LANES:(hh + 1) * LANES] = r.astype(BF16)
    elif scale_first:
        scale = jnp.where(j == 0, Q_SCALE, 1.0).astype(F32)
        o_ref[0] = (acc * scale).astype(BF16)
    else:
        o_ref[0] = acc.astype(BF16)


def _proj(x, mod3, mod_row, norm_w, w_in_b, layer, col0, ncols, rope_tabs, tm):
    bx, sx, d = x.shape
    tn = d
    rope = rope_tabs is not None
    assert not rope or (col0, ncols) == (COL_Q, 2)
    in_specs = [
        pl.BlockSpec((1, tm, d), lambda b, i, j: (b, i, 0)),
        pl.BlockSpec((1, 1, d), lambda b, i, j: (mod_row(b), 0, 0)),
        pl.BlockSpec((1, 1, d), lambda b, i, j: (mod_row(b), 0, 1)),
        pl.BlockSpec((None, 1, d), lambda b, i, j: (layer, 0, 0)),
        pl.BlockSpec((None, d, tn), lambda b, i, j: (layer, 0, col0 + j)),
    ]
    args = [x, mod3, mod3, norm_w, w_in_b]
    if rope:
        cos_t, sin_t = rope_tabs
        tab_spec = pl.BlockSpec((None, tm, LANES), lambda b, i, j: (j, i, 0))
        in_specs += [tab_spec, tab_spec]
        args += [cos_t, sin_t]
    return pl.pallas_call(
        functools.partial(_proj_kernel, rope=rope, scale_first=(col0 == COL_Q)),
        grid=(bx, sx // tm, ncols),
        in_specs=in_specs,
        out_specs=pl.BlockSpec((1, tm, tn), lambda b, i, j: (b, i, j)),
        out_shape=jax.ShapeDtypeStruct((bx, sx, ncols * tn), BF16),
        scratch_shapes=[pltpu.VMEM((tm, d), BF16)],
        compiler_params=_cparams(("parallel", "parallel", "arbitrary")),
        name="proj_rope" if rope else "proj",
    )(*args)


def _attn_kernel(lq_ref, sw_ref, q_ref, k_ref, v_ref, *rest, has_ctx, lam_init):
    if has_ctx:
        kc_ref, vc_ref, o_ref, kall_ref, vext_ref = rest
    else:
        o_ref, kall_ref, vext_ref = rest
    n_lat = k_ref.shape[1]
    n_all = kall_ref.shape[0]

    @pl.when(pl.program_id(2) == 0)
    def _():
        kall_ref[0:n_lat, :] = k_ref[0]
        vext_ref[0:n_lat, 0:V_DIM] = v_ref[0]
        if has_ctx:
            kall_ref[n_lat:n_all, :] = kc_ref[0]
            vext_ref[n_lat:n_all, 0:V_DIM] = vc_ref[0]
        ones_lane = lax.broadcasted_iota(jnp.int32, (n_all, V_DIM), 1) == 0
        vext_ref[:, V_DIM:2 * V_DIM] = jnp.where(ones_lane, 1.0, 0.0).astype(BF16)

    lq = lq_ref[...].astype(F32)
    lam = (jnp.exp(jnp.sum(lq[0:1] * lq[1:2], axis=-1, keepdims=True))
           - jnp.exp(jnp.sum(lq[2:3] * lq[3:4], axis=-1, keepdims=True)) + lam_init)

    tq = q_ref.shape[1]
    sub = min(tq, ATTN_SUB_ROWS)
    lane = lax.broadcasted_iota(jnp.int32, (sub, V_DIM), 1)

    def scores(j):
        q = q_ref[0, j * sub:(j + 1) * sub, :]
        zero = jnp.zeros_like(q)
        qq = jnp.concatenate([jnp.where(lane < HEAD_DIM, q, zero),
                              jnp.where(lane >= HEAD_DIM, q, zero)], axis=0)
        return lax.dot_general(qq, kall_ref[...], (((1,), (1,)), ((), ())),
                               preferred_element_type=F32)

    def finish(j, s):
        m = jnp.max(s, axis=-1, keepdims=True)
        p = jnp.exp2(s - m).astype(BF16)
        nd = jnp.dot(p, vext_ref[...], preferred_element_type=F32)
        o = nd[:, 0:V_DIM] / nd[:, V_DIM:V_DIM + 1]
        o = o[:sub] - lam * o[sub:]
        y = o * lax.rsqrt(jnp.mean(o * o, axis=-1, keepdims=True) + SUBLN_EPS)
        o_ref[0, j * sub:(j + 1) * sub, :] = ((y * sw_ref[...]) * (1.0 - lam_init)).astype(BF16)

    n_sub = tq // sub
    s_cur = scores(0)
    for j in range(n_sub):
        s_next = scores(j + 1) if j + 1 < n_sub else None
        finish(j, s_cur)
        s_cur = s_next


def _attention(lambda_qk, subln_w, layer, lam_init, q, k, v, kc, vc, tq):
    b, s, _ = q[0].shape
    nh = N_HEADS
    has_ctx = kc is not None

    def head_cols(rows, col):
        return pl.BlockSpec((1, rows, V_DIM), lambda bb, h, i: (bb, 0, col * nh + h))

    in_specs = [
        pl.BlockSpec((None, 4, HEAD_DIM), lambda bb, h, i: (layer, 0, 0)),
        pl.BlockSpec((None, 1, V_DIM), lambda bb, h, i: (layer, 0, 0)),
        pl.BlockSpec((1, tq, V_DIM), lambda bb, h, i: (bb, i, q[1] * nh + h)),
        head_cols(s, k[1]),
        head_cols(s, v[1]),
    ]
    args = [lambda_qk, subln_w, q[0], k[0], v[0]]
    n_keys = s
    if has_ctx:
        lc = kc[0].shape[1]
        n_keys = s + lc
        in_specs += [head_cols(lc, kc[1]), head_cols(lc, vc[1])]
        args += [kc[0], vc[0]]
    return pl.pallas_call(
        functools.partial(_attn_kernel, has_ctx=has_ctx, lam_init=lam_init),
        grid=(b, nh, s // tq),
        in_specs=in_specs,
        out_specs=pl.BlockSpec((1, tq, V_DIM), lambda bb, h, i: (bb, i, h)),
        out_shape=jax.ShapeDtypeStruct((b, s, nh * V_DIM), BF16),
        scratch_shapes=[pltpu.VMEM((n_keys, V_DIM), BF16), pltpu.VMEM((n_keys, 2 * V_DIM), BF16)],
        compiler_params=_cparams(("parallel", "parallel", "arbitrary")),
        name="attn_ctx" if has_ctx else "attn",
    )(*args)


def _fourier_kernel(u_ref, cs_ref, w_ref, o_ref, ab_ref):
    r = pl.program_id(1)
    n = u_ref.shape[1]

    @pl.when(r == 0)
    def _():
        for g in range(u_ref.shape[2] // FGROUP_DIM):
            cols = slice(g * FGROUP_DIM, (g + 1) * FGROUP_DIM)
            ab = jnp.dot(u_ref[0, :, cols], cs_ref[...], preferred_element_type=F32)
            ab_ref[0:n, cols] = ab[:, :FGROUP_DIM].astype(BF16)
            ab_ref[n:2 * n, cols] = ab[:, FGROUP_DIM:].astype(BF16)

    o_ref[0] = jnp.dot(w_ref[...], ab_ref[...], preferred_element_type=F32).astype(BF16)


def _dft_cos_sin(n):
    k = jnp.arange(n, dtype=jnp.int32)
    ang = ((k[:, None] * k[None, :]) % n).astype(F32) * (2.0 * math.pi / n)
    scale = 1.0 / math.sqrt(n)
    return jnp.cos(ang) * scale, jnp.sin(ang) * scale


def _fourier(p, col0, cs_chan, w_pos, tr):
    b, n, _ = p.shape
    d = N_HEADS * V_DIM
    return pl.pallas_call(
        _fourier_kernel,
        grid=(b, n // tr),
        in_specs=[
            pl.BlockSpec((1, n, d), lambda bb, r: (bb, 0, COL_UF - col0)),
            pl.BlockSpec((FGROUP_DIM, 2 * FGROUP_DIM), lambda bb, r: (0, 0)),
            pl.BlockSpec((tr, 2 * n), lambda bb, r: (r, 0)),
        ],
        out_specs=pl.BlockSpec((1, tr, d), lambda bb, r: (bb, r, 0)),
        out_shape=jax.ShapeDtypeStruct((b, n, d), BF16),
        scratch_shapes=[pltpu.VMEM((2 * n, d), BF16)],
        compiler_params=_cparams(("parallel", "arbitrary")),
        name="fourier",
    )(p, cs_chan, w_pos)


def _sigmoid(z):
    return 0.5 * jnp.tanh(0.5 * z) + 0.5


def _silu(z):
    h = 0.5 * z
    return h * jnp.tanh(h) + h


def _merge_kernel(ya_ref, yf_ref, za_ref, xin_ref, bg_ref, cg_ref, zc_ref, zf_ref,
                  ga_ref, gc_ref, gf_ref, xin_p_ref, cg_p_ref, xin_n_ref, cg_n_ref,
                  x_ref, gate_ref, cw_ref, wa_ref, wc_ref, wf_ref, wo_ref, o_ref, u_ref):
    i = pl.program_id(1)
    tm = x_ref.shape[1]
    pad = SUBLANES

    u_ref[pad:pad + tm, :] = cg_ref[0].astype(F32) * xin_ref[0].astype(F32)
    up = cg_p_ref[0].astype(F32) * xin_p_ref[0].astype(F32)
    un = cg_n_ref[0].astype(F32) * xin_n_ref[0].astype(F32)
    u_ref[0:pad, :] = jnp.where(i == 0, 0.0, up)
    u_ref[pad + tm:2 * pad + tm, :] = jnp.where(i == pl.num_programs(1) - 1, 0.0, un)
    cw = cw_ref[...]
    conv = (cw[0:1] * u_ref[pad - 1:pad - 1 + tm, :] + cw[1:2] * u_ref[pad:pad + tm, :]
            + cw[2:3] * u_ref[pad + 1:pad + 1 + tm, :])
    y_c = bg_ref[0].astype(F32) * conv

    def branch(y, z_ref, w_ref, g_ref):
        t = jnp.dot((y * _silu(z_ref[0].astype(F32))).astype(BF16), w_ref[...],
                    preferred_element_type=F32)
        return _sigmoid(g_ref[0].astype(F32)) * t

    merged = (branch(ya_ref[0].astype(F32), za_ref, wa_ref, ga_ref)
              + branch(y_c, zc_ref, wc_ref, gc_ref)
              + branch(yf_ref[0].astype(F32), zf_ref, wf_ref, gf_ref))
    out = jnp.dot(merged.astype(BF16), wo_ref[...], preferred_element_type=F32)
    o_ref[0] = x_ref[0] + gate_ref[0] * out


def _merge(x, p, col0, ya, yf, mod3, mod_row, conv_w, wa, wc, wf, wo, layer, tm):
    bx, sx, d = x.shape
    nb = tm // SUBLANES
    last_halo = sx // SUBLANES - 1

    def whole(b, i):
        return (b, i, 0)

    def col(cb):
        return pl.BlockSpec((1, tm, d), lambda b, i: (b, i, cb - col0))

    def halo_prev(cb):
        return pl.BlockSpec((1, SUBLANES, d),
                            lambda b, i: (b, jnp.maximum(i * nb - 1, 0), cb - col0))

    def halo_next(cb):
        return pl.BlockSpec((1, SUBLANES, d),
                            lambda b, i: (b, jnp.minimum((i + 1) * nb, last_halo), cb - col0))

    def weight():
        return pl.BlockSpec((None, d, d), lambda b, i: (layer, 0, 0))

    in_specs = [
        pl.BlockSpec((1, tm, d), whole), pl.BlockSpec((1, tm, d), whole),
        col(COL_ZA), col(COL_XIN), col(COL_BG), col(COL_CG), col(COL_ZC), col(COL_ZF),
        col(COL_GL), col(COL_GL + 1), col(COL_GL + 2),
        halo_prev(COL_XIN), halo_prev(COL_CG), halo_next(COL_XIN), halo_next(COL_CG),
        pl.BlockSpec((1, tm, d), whole),
        pl.BlockSpec((1, 1, d), lambda b, i: (mod_row(b), 0, 2)),
        pl.BlockSpec((None, 3, d), lambda b, i: (layer, 0, 0)),
        weight(), weight(), weight(), weight(),
    ]
    return pl.pallas_call(
        _merge_kernel,
        grid=(bx, sx // tm),
        in_specs=in_specs,
        out_specs=pl.BlockSpec((1, tm, d), lambda b, i: (b, i, 0)),
        out_shape=jax.ShapeDtypeStruct((bx, sx, d), F32),
        scratch_shapes=[pltpu.VMEM((tm + 2 * SUBLANES, d), F32)],
        compiler_params=_cparams(("parallel", "parallel")),
        name="merge",
    )(ya, yf, p, p, p, p, p, p, p, p, p, p, p, p, p, x, mod3, conv_w, wa, wc, wf, wo)


def _final_norm_kernel(x_ref, w_ref, o_ref):
    x = x_ref[...]
    y = x * lax.rsqrt(jnp.mean(x * x, axis=-1, keepdims=True) + NORM_EPS)
    o_ref[...] = y * w_ref[...]


def _final_norm(x, w, tm):
    b, s, d = x.shape
    x2 = x.reshape(b * s, d)
    out = pl.pallas_call(
        _final_norm_kernel,
        grid=(b * s // tm,),
        in_specs=[pl.BlockSpec((tm, d), lambda i: (i, 0)), pl.BlockSpec((1, d), lambda i: (0, 0))],
        out_specs=pl.BlockSpec((tm, d), lambda i: (i, 0)),
        out_shape=jax.ShapeDtypeStruct((b * s, d), F32),
        compiler_params=_cparams(("parallel",)),
        name="final_norm",
    )(x2, w.reshape(1, d))
    return out.reshape(b, s, d)


def _rope_tables(n_tokens):
    axis_dim = HEAD_DIM // 2
    rows = n_tokens // GRID_W
    row = jnp.repeat(jnp.arange(rows), GRID_W).astype(F32)
    col = jnp.tile(jnp.arange(GRID_W), rows).astype(F32)
    inv_freq = ROPE_BASE ** (-jnp.arange(0, axis_dim, 2, dtype=F32) / axis_dim)
    ang_r = row[:, None] * inv_freq
    ang_c = col[:, None] * inv_freq
    ang = jnp.concatenate([ang_r, ang_r, ang_c, ang_c], axis=-1)
    ang = jnp.concatenate([ang, ang], axis=-1)
    half = axis_dim // 2
    sign = jnp.where((jnp.arange(V_DIM) % axis_dim) < half, -1.0, 1.0).astype(F32)
    cos = jnp.cos(ang)
    sin = jnp.sin(ang) * sign
    return jnp.stack([cos * Q_SCALE, cos]), jnp.stack([sin * Q_SCALE, sin])


def _fourier_tables(n):
    cn, sn = _dft_cos_sin(n)
    return jnp.concatenate([cn, -sn], axis=1).astype(BF16)


def _pick_tile(n, target):
    t = min(n, target)
    while n % t:
        t //= 2
    return t


def kernel(x, c, ctx, c_ctx, norm_w, w_mod, b_mod, w_in, lambda_qk, subln_w, conv_w,
           w_attn_o, w_conv_o, w_four_o, w_out, final_norm_w):
    b, s, d = x.shape
    lc = ctx.shape[1]
    depth = w_in.shape[0]
    assert d == N_HEADS * V_DIM and w_in.shape[2] == N_PROJ_BLOCKS * d
    assert s % GRID_W == 0 and s % LANES == 0 and lc % SUBLANES == 0

    pad = (-(b + 1)) % MOD_ROWS_PAD
    cond = jnp.concatenate([c, c_ctx[None, :], jnp.zeros((pad, d), F32)], axis=0)
    mod = _modulation(cond, w_mod, b_mod)
    n_rows = cond.shape[0]

    norm_w = norm_w.reshape(depth, 1, d)
    subln_w = subln_w.reshape(depth, 1, V_DIM)
    w_in_b = w_in.astype(BF16)
    wa_b, wc_b, wf_b, wo_b = (w.astype(BF16) for w in (w_attn_o, w_conv_o, w_four_o, w_out))

    rope_tabs = _rope_tables(s)
    cc, sc = _dft_cos_sin(FGROUP_DIM)
    cs_chan = jnp.concatenate([cc, sc], axis=1).astype(BF16)
    w_pos_lat = _fourier_tables(s)
    w_pos_ctx = _fourier_tables(lc)

    ctx_flat = ctx.reshape(1, b * lc, d)
    lat_row = lambda bb: bb
    ctx_row = lambda bb: b

    tm_lat = _pick_tile(s, 1024)
    tm_ctx = _pick_tile(b * lc, 1024)
    tq_lat = _pick_tile(s, 2048)
    tr_lat = _pick_tile(s, 512)
    tmm_lat = _pick_tile(s, 256)
    tmm_ctx = _pick_tile(lc, 256)

    for l in range(depth):
        last = l == depth - 1
        lam_init = 0.8 - 0.6 * math.exp(-0.3 * l)
        mod3 = mod[l].reshape(n_rows, 1, 3 * d)

        p_qk = _proj(x, mod3, lat_row, norm_w, w_in_b, l, COL_Q, 2, rope_tabs, tm_lat)
        p_lat = _proj(x, mod3, lat_row, norm_w, w_in_b, l, COL_V, N_PROJ_BLOCKS - COL_V, None, tm_lat)
        c0, cn = (COL_K, 2) if last else (COL_Q, N_PROJ_BLOCKS)
        p_ctx = _proj(ctx_flat, mod3, ctx_row, norm_w, w_in_b, l, c0, cn, None, tm_ctx).reshape(b, lc, -1)

        ya = _attention(lambda_qk, subln_w, l, lam_init, (p_qk, COL_Q), (p_qk, COL_K),
                        (p_lat, 0), (p_ctx, COL_K - c0), (p_ctx, COL_V - c0), tq_lat)
        yf = _fourier(p_lat, COL_V, cs_chan, w_pos_lat, tr_lat)
        x_new = _merge(x, p_lat, COL_V, ya, yf, mod3, lat_row, conv_w, wa_b, wc_b, wf_b, wo_b,
                       l, tmm_lat)

        if not last:
            yac = _attention(lambda_qk, subln_w, l, lam_init, (p_ctx, COL_Q), (p_ctx, COL_K),
                             (p_ctx, COL_V), None, None, lc)
            yfc = _fourier(p_ctx, COL_Q, cs_chan, w_pos_ctx, lc)
            ctx3 = ctx_flat.reshape(b, lc, d)
            ctx_flat = _merge(ctx3, p_ctx, COL_Q, yac, yfc, mod3, ctx_row, conv_w, wa_b, wc_b, wf_b,
                              wo_b, l, tmm_ctx).reshape(1, b * lc, d)
        x = x_new

    return _final_norm(x, final_norm_w, _pick_tile(b * s, 1024))
```

```python
import functools
import math

import jax
import jax.numpy as jnp
from jax import lax
from jax.experimental import pallas as pl
from jax.experimental.pallas import tpu as pltpu

F32 = jnp.float32
BF16 = jnp.bfloat16

N_HEADS = 8
HEAD_DIM = 64
V_DIM = 2 * HEAD_DIM
FGROUP_DIM = 128
GRID_W = 64
ROPE_BASE = 10000.0
NORM_EPS = 1e-6
SUBLN_EPS = 1e-5
N_PROJ_BLOCKS = 13
COL_Q, COL_K, COL_V, COL_ZA, COL_XIN, COL_BG, COL_CG, COL_ZC, COL_UF, COL_ZF, COL_GL = range(11)

LANES = 128
SUBLANES = 8
VMEM_LIMIT_BYTES = 56 * 1024 * 1024
MOD_ROWS_PAD = 8
ATTN_SUB_ROWS = 256
Q_SCALE = math.log2(math.e) / math.sqrt(HEAD_DIM)


def _cparams(sem):
    return pltpu.CompilerParams(dimension_semantics=sem, vmem_limit_bytes=VMEM_LIMIT_BYTES)


def _mod_kernel(cond_ref, w_ref, b_ref, o_ref):
    cond = cond_ref[...]
    a = cond * jax.nn.sigmoid(cond)
    o_ref[...] = jnp.dot(a, w_ref[...], preferred_element_type=F32,
                         precision=lax.Precision.HIGHEST) + b_ref[...]


def _modulation(cond, w_mod, b_mod):
    depth, d, w3 = w_mod.shape
    rows = cond.shape[0]
    tn = d
    return pl.pallas_call(
        _mod_kernel,
        grid=(depth, w3 // tn),
        in_specs=[
            pl.BlockSpec((rows, d), lambda l, j: (0, 0)),
            pl.BlockSpec((None, d, tn), lambda l, j: (l, 0, j)),
            pl.BlockSpec((None, 1, tn), lambda l, j: (l, 0, j)),
        ],
        out_specs=pl.BlockSpec((None, rows, tn), lambda l, j: (l, 0, j)),
        out_shape=jax.ShapeDtypeStruct((depth, rows, w3), F32),
        compiler_params=_cparams(("parallel", "parallel")),
        name="modulation",
    )(cond, w_mod, b_mod.reshape(depth, 1, w3))


def _swap_halves16(a):
    lane = lax.broadcasted_iota(jnp.int32, a.shape, 1)
    first = (lane % 32) < 16
    return jnp.where(first, pltpu.roll(a, LANES - 16, axis=1), pltpu.roll(a, 16, axis=1))


def _proj_kernel(x_ref, sh_ref, sc_ref, nw_ref, w_ref, *rest, rope, scale_first):
    if rope:
        cos_ref, sin_ref, o_ref, h_ref = rest
    else:
        o_ref, h_ref = rest
    j = pl.program_id(2)

    @pl.when(j == 0)
    def _():
        x = x_ref[0]
        y = x * lax.rsqrt(jnp.mean(x * x, axis=-1, keepdims=True) + NORM_EPS)
        h = (y * nw_ref[...]) * (1.0 + sc_ref[0]) + sh_ref[0]
        h_ref[...] = h.astype(BF16)

    acc = jnp.dot(h_ref[...], w_ref[...], preferred_element_type=F32)
    tn = acc.shape[1]

    if rope:
        cos = cos_ref[...]
        sin = sin_ref[...]
        for hh in range(tn // LANES):
            a = acc[:, hh * LANES:(hh + 1) * LANES]
            r = a * cos + _swap_halves16(a) * sin
            o_ref[0, :, hh * LANES:(hh + 1) * LANES] = r.astype(BF16)
    elif scale_first:
        scale = jnp.where(j == 0, Q_SCALE, 1.0).astype(F32)
        o_ref[0] = (acc * scale).astype(BF16)
    else:
        o_ref[0] = acc.astype(BF16)


def _proj(x, mod3, mod_row, norm_w, w_in_b, layer, col0, ncols, rope_tabs, tm):
    bx, sx, d = x.shape
    tn = d
    rope = rope_tabs is not None
    assert not rope or (col0, ncols) == (COL_Q, 2)
    in_specs = [
        pl.BlockSpec((1, tm, d), lambda b, i, j: (b, i, 0)),
        pl.BlockSpec((1, 1, d), lambda b, i, j: (mod_row(b), 0, 0)),
        pl.BlockSpec((1, 1, d), lambda b, i, j: (mod_row(b), 0, 1)),
        pl.BlockSpec((None, 1, d), lambda b, i, j: (layer, 0, 0)),
        pl.BlockSpec((None, d, tn), lambda b, i, j: (layer, 0, col0 + j)),
    ]
    args = [x, mod3, mod3, norm_w, w_in_b]
    if rope:
        cos_t, sin_t = rope_tabs
        tab_spec = pl.BlockSpec((None, tm, LANES), lambda b, i, j: (j, i, 0))
        in_specs += [tab_spec, tab_spec]
        args += [cos_t, sin_t]
    return pl.pallas_call(
        functools.partial(_proj_kernel, rope=rope, scale_first=(col0 == COL_Q)),
        grid=(bx, sx // tm, ncols),
        in_specs=in_specs,
        out_specs=pl.BlockSpec((1, tm, tn), lambda b, i, j: (b, i, j)),
        out_shape=jax.ShapeDtypeStruct((bx, sx, ncols * tn), BF16),
        scratch_shapes=[pltpu.VMEM((tm, d), BF16)],
        compiler_params=_cparams(("parallel", "parallel", "arbitrary")),
        name="proj_rope" if rope else "proj",
    )(*args)


def _attn_kernel(lq_ref, sw_ref, q_ref, k_ref, v_ref, *rest, has_ctx, lam_init):
    if has_ctx:
        kc_ref, vc_ref, o_ref, kall_ref, vext_ref = rest
    else:
        o_ref, kall_ref, vext_ref = rest
    n_lat = k_ref.shape[1]
    n_all = kall_ref.shape[0]

    @pl.when(pl.program_id(2) == 0)
    def _():
        kall_ref[0:n_lat, :] = k_ref[0]
        vext_ref[0:n_lat, 0:V_DIM] = v_ref[0]
        if has_ctx:
            kall_ref[n_lat:n_all, :] = kc_ref[0]
            vext_ref[n_lat:n_all, 0:V_DIM] = vc_ref[0]
        ones_lane = lax.broadcasted_iota(jnp.int32, (n_all, V_DIM), 1) == 0
        vext_ref[:, V_DIM:2 * V_DIM] = jnp.where(ones_lane, 1.0, 0.0).astype(BF16)

    lq = lq_ref[...].astype(F32)
    lam = (jnp.exp(jnp.sum(lq[0:1] * lq[1:2], axis=-1, keepdims=True))
           - jnp.exp(jnp.sum(lq[2:3] * lq[3:4], axis=-1, keepdims=True)) + lam_init)

    tq = q_ref.shape[1]
    sub = min(tq, ATTN_SUB_ROWS)
    lane = lax.broadcasted_iota(jnp.int32, (sub, V_DIM), 1)

    def scores(j):
        q = q_ref[0, j * sub:(j + 1) * sub, :]
        zero = jnp.zeros_like(q)
        qq = jnp.concatenate([jnp.where(lane < HEAD_DIM, q, zero),
                              jnp.where(lane >= HEAD_DIM, q, zero)], axis=0)
        return lax.dot_general(qq, kall_ref[...], (((1,), (1,)), ((), ())),
                               preferred_element_type=F32)

    def finish(j, s):
        m = jnp.max(s, axis=-1, keepdims=True)
        p = jnp.exp2(s - m).astype(BF16)
        nd = jnp.dot(p, vext_ref[...], preferred_element_type=F32)
        o = nd[:, 0:V_DIM] / nd[:, V_DIM:V_DIM + 1]
        o = o[:sub] - lam * o[sub:]
        y = o * lax.rsqrt(jnp.mean(o * o, axis=-1, keepdims=True) + SUBLN_EPS)
        o_ref[0, j * sub:(j + 1) * sub, :] = ((y * sw_ref[...]) * (1.0 - lam_init)).astype(BF16)

    n_sub = tq // sub
    s_cur = scores(0)
    for j in range(n_sub):
        s_next = scores(j + 1) if j + 1 < n_sub else None
        finish(j, s_cur)
        s_cur = s_next


def _attention(lambda_qk, subln_w, layer, lam_init, q, k, v, kc, vc, tq):
    b, s, _ = q[0].shape
    nh = N_HEADS
    has_ctx = kc is not None

    def head_cols(rows, col):
        return pl.BlockSpec((1, rows, V_DIM), lambda bb, h, i: (bb, 0, col * nh + h))

    in_specs = [
        pl.BlockSpec((None, 4, HEAD_DIM), lambda bb, h, i: (layer, 0, 0)),
        pl.BlockSpec((None, 1, V_DIM), lambda bb, h, i: (layer, 0, 0)),
        pl.BlockSpec((1, tq, V_DIM), lambda bb, h, i: (bb, i, q[1] * nh + h)),
        head_cols(s, k[1]),
        head_cols(s, v[1]),
    ]
    args = [lambda_qk, subln_w, q[0], k[0], v[0]]
    n_keys = s
    if has_ctx:
        lc = kc[0].shape[1]
        n_keys = s + lc
        in_specs += [head_cols(lc, kc[1]), head_cols(lc, vc[1])]
        args += [kc[0], vc[0]]
    return pl.pallas_call(
        functools.partial(_attn_kernel, has_ctx=has_ctx, lam_init=lam_init),
        grid=(b, nh, s // tq),
        in_specs=in_specs,
        out_specs=pl.BlockSpec((1, tq, V_DIM), lambda bb, h, i: (bb, i, h)),
        out_shape=jax.ShapeDtypeStruct((b, s, nh * V_DIM), BF16),
        scratch_shapes=[pltpu.VMEM((n_keys, V_DIM), BF16), pltpu.VMEM((n_keys, 2 * V_DIM), BF16)],
        compiler_params=_cparams(("parallel", "parallel", "arbitrary")),
        name="attn_ctx" if has_ctx else "attn",
    )(*args)


def _fourier_kernel(u_ref, cs_ref, w_ref, o_ref, ab_ref):
    r = pl.program_id(1)
    n = u_ref.shape[1]

    @pl.when(r == 0)
    def _():
        for g in range(u_ref.shape[2] // FGROUP_DIM):
            cols = slice(g * FGROUP_DIM, (g + 1) * FGROUP_DIM)
            ab = jnp.dot(u_ref[0, :, cols], cs_ref[...], preferred_element_type=F32)
            ab_ref[0:n, cols] = ab[:, :FGROUP_DIM].astype(BF16)
            ab_ref[n:2 * n, cols] = ab[:, FGROUP_DIM:].astype(BF16)

    o_ref[0] = jnp.dot(w_ref[...], ab_ref[...], preferred_element_type=F32).astype(BF16)


def _dft_cos_sin(n):
    k = jnp.arange(n, dtype=jnp.int32)
    ang = ((k[:, None] * k[None, :]) % n).astype(F32) * (2.0 * math.pi / n)
    scale = 1.0 / math.sqrt(n)
    return jnp.cos(ang) * scale, jnp.sin(ang) * scale


def _fourier(p, col0, cs_chan, w_pos, tr):
    b, n, _ = p.shape
    d = N_HEADS * V_DIM
    return pl.pallas_call(
        _fourier_kernel,
        grid=(b, n // tr),
        in_specs=[
            pl.BlockSpec((1, n, d), lambda bb, r: (bb, 0, COL_UF - col0)),
            pl.BlockSpec((FGROUP_DIM, 2 * FGROUP_DIM), lambda bb, r: (0, 0)),
            pl.BlockSpec((tr, 2 * n), lambda bb, r: (r, 0)),
        ],
        out_specs=pl.BlockSpec((1, tr, d), lambda bb, r: (bb, r, 0)),
        out_shape=jax.ShapeDtypeStruct((b, n, d), BF16),
        scratch_shapes=[pltpu.VMEM((2 * n, d), BF16)],
        compiler_params=_cparams(("parallel", "arbitrary")),
        name="fourier",
    )(p, cs_chan, w_pos)


def _fourier_half_kernel(u_ref, cs_ref, c_ref, s_ref, rev_ref, o_ref, a_ref, b_ref):
    r = pl.program_id(1)
    tr = o_ref.shape[3]

    @pl.when(r == 0)
    def _():
        for g in range(u_ref.shape[2] // FGROUP_DIM):
            cols = slice(g * FGROUP_DIM, (g + 1) * FGROUP_DIM)
            ab = jnp.dot(u_ref[0, :, cols], cs_ref[...], preferred_element_type=F32)
            a_ref[:, cols] = ab[:, :FGROUP_DIM].astype(BF16)
            b_ref[:, cols] = ab[:, FGROUP_DIM:].astype(BF16)

    p = jnp.dot(c_ref[...], a_ref[...], preferred_element_type=F32)
    q = jnp.dot(s_ref[...], b_ref[...], preferred_element_type=F32)
    o_ref[0, 0, 0] = (p[:tr] - q[:tr]).astype(BF16)
    o_ref[0, 0, 1] = jnp.dot(rev_ref[...], (p + q).astype(BF16),
                             preferred_element_type=F32).astype(BF16)


def _fourier_half_tables(n, tr):
    ext = 2 * SUBLANES
    n_tiles = n // (2 * tr)
    rows = (jnp.arange(n_tiles, dtype=jnp.int32)[:, None] * tr
            + jnp.arange(tr + ext, dtype=jnp.int32)[None, :])
    k = jnp.arange(n, dtype=jnp.int32)
    ang = ((rows[:, :, None] * k[None, None, :]) % n).astype(F32) * (2.0 * math.pi / n)
    scale = 1.0 / math.sqrt(n)
    i = jnp.arange(tr, dtype=jnp.int32)[:, None]
    j = jnp.arange(tr + ext, dtype=jnp.int32)[None, :]
    rev = (j == tr - i).astype(BF16)
    return (jnp.cos(ang) * scale).astype(BF16), (jnp.sin(ang) * scale).astype(BF16), rev


def _fourier_half(p, col0, cs_chan, tabs, tr):
    b, n, _ = p.shape
    d = N_HEADS * V_DIM
    c_t, s_t, rev = tabs
    n_tiles, rows_ext, _ = c_t.shape
    return pl.pallas_call(
        _fourier_half_kernel,
        grid=(b, n_tiles),
        in_specs=[
            pl.BlockSpec((1, n, d), lambda bb, r: (bb, 0, COL_UF - col0)),
            pl.BlockSpec((FGROUP_DIM, 2 * FGROUP_DIM), lambda bb, r: (0, 0)),
            pl.BlockSpec((None, rows_ext, n), lambda bb, r: (r, 0, 0)),
            pl.BlockSpec((None, rows_ext, n), lambda bb, r: (r, 0, 0)),
            pl.BlockSpec((tr, rows_ext), lambda bb, r: (0, 0)),
        ],
        out_specs=pl.BlockSpec((1, 1, 2, tr, d), lambda bb, r: (bb, r, 0, 0, 0)),
        out_shape=jax.ShapeDtypeStruct((b, n_tiles, 2, tr, d), BF16),
        scratch_shapes=[pltpu.VMEM((n, d), BF16), pltpu.VMEM((n, d), BF16)],
        compiler_params=_cparams(("parallel", "arbitrary")),
        name="fourier_half",
    )(p, cs_chan, c_t, s_t, rev)


def _sigmoid(z):
    return 0.5 * jnp.tanh(0.5 * z) + 0.5


def _silu(z):
    h = 0.5 * z
    return h * jnp.tanh(h) + h


def _merge_kernel(ya_ref, yf_ref, za_ref, xin_ref, bg_ref, cg_ref, zc_ref, zf_ref,
                  ga_ref, gc_ref, gf_ref, xin_p_ref, cg_p_ref, xin_n_ref, cg_n_ref,
                  x_ref, gate_ref, cw_ref, wa_ref, wc_ref, wf_ref, wo_ref, o_ref, u_ref):
    i = pl.program_id(1)
    tm = x_ref.shape[1]
    pad = SUBLANES

    u_ref[pad:pad + tm, :] = cg_ref[0].astype(F32) * xin_ref[0].astype(F32)
    up = cg_p_ref[0].astype(F32) * xin_p_ref[0].astype(F32)
    un = cg_n_ref[0].astype(F32) * xin_n_ref[0].astype(F32)
    u_ref[0:pad, :] = jnp.where(i == 0, 0.0, up)
    u_ref[pad + tm:2 * pad + tm, :] = jnp.where(i == pl.num_programs(1) - 1, 0.0, un)
    cw = cw_ref[...]
    conv = (cw[0:1] * u_ref[pad - 1:pad - 1 + tm, :] + cw[1:2] * u_ref[pad:pad + tm, :]
            + cw[2:3] * u_ref[pad + 1:pad + 1 + tm, :])
    y_c = bg_ref[0].astype(F32) * conv

    def branch(y, z_ref, w_ref, g_ref):
        t = jnp.dot((y * _silu(z_ref[0].astype(F32))).astype(BF16), w_ref[...],
                    preferred_element_type=F32)
        return _sigmoid(g_ref[0].astype(F32)) * t

    merged = (branch(ya_ref[0].astype(F32), za_ref, wa_ref, ga_ref)
              + branch(y_c, zc_ref, wc_ref, gc_ref)
              + branch(yf_ref[...].astype(F32), zf_ref, wf_ref, gf_ref))
    out = jnp.dot(merged.astype(BF16), wo_ref[...], preferred_element_type=F32)
    o_ref[0] = x_ref[0] + gate_ref[0] * out


def _merge(x, p, col0, ya, yf, yf_spec, mod3, mod_row, conv_w, wa, wc, wf, wo, layer, tm):
    bx, sx, d = x.shape
    nb = tm // SUBLANES
    last_halo = sx // SUBLANES - 1

    def whole(b, i):
        return (b, i, 0)

    def col(cb):
        return pl.BlockSpec((1, tm, d), lambda b, i: (b, i, cb - col0))

    def halo_prev(cb):
        return pl.BlockSpec((1, SUBLANES, d),
                            lambda b, i: (b, jnp.maximum(i * nb - 1, 0), cb - col0))

    def halo_next(cb):
        return pl.BlockSpec((1, SUBLANES, d),
                            lambda b, i: (b, jnp.minimum((i + 1) * nb, last_halo), cb - col0))

    def weight():
        return pl.BlockSpec((None, d, d), lambda b, i: (layer, 0, 0))

    in_specs = [
        pl.BlockSpec((1, tm, d), whole), yf_spec,
        col(COL_ZA), col(COL_XIN), col(COL_BG), col(COL_CG), col(COL_ZC), col(COL_ZF),
        col(COL_GL), col(COL_GL + 1), col(COL_GL + 2),
        halo_prev(COL_XIN), halo_prev(COL_CG), halo_next(COL_XIN), halo_next(COL_CG),
        pl.BlockSpec((1, tm, d), whole),
        pl.BlockSpec((1, 1, d), lambda b, i: (mod_row(b), 0, 2)),
        pl.BlockSpec((None, 3, d), lambda b, i: (layer, 0, 0)),
        weight(), weight(), weight(), weight(),
    ]
    return pl.pallas_call(
        _merge_kernel,
        grid=(bx, sx // tm),
        in_specs=in_specs,
        out_specs=pl.BlockSpec((1, tm, d), lambda b, i: (b, i, 0)),
        out_shape=jax.ShapeDtypeStruct((bx, sx, d), F32),
        scratch_shapes=[pltpu.VMEM((tm + 2 * SUBLANES, d), F32)],
        compiler_params=_cparams(("parallel", "parallel")),
        name="merge",
    )(ya, yf, p, p, p, p, p, p, p, p, p, p, p, p, p, x, mod3, conv_w, wa, wc, wf, wo)


def _final_norm_kernel(x_ref, w_ref, o_ref):
    x = x_ref[...]
    y = x * lax.rsqrt(jnp.mean(x * x, axis=-1, keepdims=True) + NORM_EPS)
    o_ref[...] = y * w_ref[...]


def _final_norm(x, w, tm):
    b, s, d = x.shape
    x2 = x.reshape(b * s, d)
    out = pl.pallas_call(
        _final_norm_kernel,
        grid=(b * s // tm,),
        in_specs=[pl.BlockSpec((tm, d), lambda i: (i, 0)), pl.BlockSpec((1, d), lambda i: (0, 0))],
        out_specs=pl.BlockSpec((tm, d), lambda i: (i, 0)),
        out_shape=jax.ShapeDtypeStruct((b * s, d), F32),
        compiler_params=_cparams(("parallel",)),
        name="final_norm",
    )(x2, w.reshape(1, d))
    return out.reshape(b, s, d)


def _rope_tables(n_tokens):
    axis_dim = HEAD_DIM // 2
    rows = n_tokens // GRID_W
    row = jnp.repeat(jnp.arange(rows), GRID_W).astype(F32)
    col = jnp.tile(jnp.arange(GRID_W), rows).astype(F32)
    inv_freq = ROPE_BASE ** (-jnp.arange(0, axis_dim, 2, dtype=F32) / axis_dim)
    ang_r = row[:, None] * inv_freq
    ang_c = col[:, None] * inv_freq
    ang = jnp.concatenate([ang_r, ang_r, ang_c, ang_c], axis=-1)
    ang = jnp.concatenate([ang, ang], axis=-1)
    half = axis_dim // 2
    sign = jnp.where((jnp.arange(V_DIM) % axis_dim) < half, -1.0, 1.0).astype(F32)
    cos = jnp.cos(ang)
    sin = jnp.sin(ang) * sign
    return jnp.stack([cos * Q_SCALE, cos]), jnp.stack([sin * Q_SCALE, sin])


def _fourier_tables(n):
    cn, sn = _dft_cos_sin(n)
    return jnp.concatenate([cn, -sn], axis=1).astype(BF16)


def _pick_tile(n, target):
    t = min(n, target)
    while n % t:
        t //= 2
    return t


def kernel(x, c, ctx, c_ctx, norm_w, w_mod, b_mod, w_in, lambda_qk, subln_w, conv_w,
           w_attn_o, w_conv_o, w_four_o, w_out, final_norm_w):
    b, s, d = x.shape
    lc = ctx.shape[1]
    depth = w_in.shape[0]
    assert d == N_HEADS * V_DIM and w_in.shape[2] == N_PROJ_BLOCKS * d
    assert s % GRID_W == 0 and s % LANES == 0 and lc % SUBLANES == 0

    pad = (-(b + 1)) % MOD_ROWS_PAD
    cond = jnp.concatenate([c, c_ctx[None, :], jnp.zeros((pad, d), F32)], axis=0)
    mod = _modulation(cond, w_mod, b_mod)
    n_rows = cond.shape[0]

    norm_w = norm_w.reshape(depth, 1, d)
    subln_w = subln_w.reshape(depth, 1, V_DIM)
    w_in_b = w_in.astype(BF16)
    wa_b, wc_b, wf_b, wo_b = (w.astype(BF16) for w in (w_attn_o, w_conv_o, w_four_o, w_out))

    rope_tabs = _rope_tables(s)
    cc, sc = _dft_cos_sin(FGROUP_DIM)
    cs_chan = jnp.concatenate([cc, sc], axis=1).astype(BF16)
    w_pos_ctx = _fourier_tables(lc)

    ctx_flat = ctx.reshape(1, b * lc, d)
    lat_row = lambda bb: bb
    ctx_row = lambda bb: b

    tm_lat = _pick_tile(s, 1024)
    tm_ctx = _pick_tile(b * lc, 1024)
    tq_lat = _pick_tile(s, 2048)
    tmm_lat = _pick_tile(s // 2, 256)
    tmm_ctx = _pick_tile(lc, 256)
    four_tabs = _fourier_half_tables(s, tmm_lat)
    n_ftiles = s // (2 * tmm_lat)
    yf_lat_spec = pl.BlockSpec(
        (None, None, None, tmm_lat, d),
        lambda bb, i: (bb, jnp.where(i < n_ftiles, i, 2 * n_ftiles - 1 - i),
                       jnp.where(i < n_ftiles, 0, 1), 0, 0))
    yf_ctx_spec = pl.BlockSpec((None, tmm_ctx, d), lambda bb, i: (bb, i, 0))

    for l in range(depth):
        last = l == depth - 1
        lam_init = 0.8 - 0.6 * math.exp(-0.3 * l)
        mod3 = mod[l].reshape(n_rows, 1, 3 * d)

        p_qk = _proj(x, mod3, lat_row, norm_w, w_in_b, l, COL_Q, 2, rope_tabs, tm_lat)
        p_lat = _proj(x, mod3, lat_row, norm_w, w_in_b, l, COL_V, N_PROJ_BLOCKS - COL_V, None, tm_lat)
        c0, cn = (COL_K, 2) if last else (COL_Q, N_PROJ_BLOCKS)
        p_ctx = _proj(ctx_flat, mod3, ctx_row, norm_w, w_in_b, l, c0, cn, None, tm_ctx).reshape(b, lc, -1)

        ya = _attention(lambda_qk, subln_w, l, lam_init, (p_qk, COL_Q), (p_qk, COL_K),
                        (p_lat, 0), (p_ctx, COL_K - c0), (p_ctx, COL_V - c0), tq_lat)
        yf = _fourier_half(p_lat, COL_V, cs_chan, four_tabs, tmm_lat)
        x_new = _merge(x, p_lat, COL_V, ya, yf, yf_lat_spec, mod3, lat_row, conv_w, wa_b, wc_b, wf_b,
                       wo_b, l, tmm_lat)

        if not last:
            yac = _attention(lambda_qk, subln_w, l, lam_init, (p_ctx, COL_Q), (p_ctx, COL_K),
                             (p_ctx, COL_V), None, None, lc)
            yfc = _fourier(p_ctx, COL_Q, cs_chan, w_pos_ctx, lc)
            ctx3 = ctx_flat.reshape(b, lc, d)
            ctx_flat = _merge(ctx3, p_ctx, COL_Q, yac, yfc, yf_ctx_spec, mod3, ctx_row, conv_w, wa_b, wc_b, wf_b,
                              wo_b, l, tmm_ctx).reshape(1, b * lc, d)
        x = x_new

    return _final_norm(x, final_norm_w, _pick_tile(b * s, 1024))
```

```python
import functools
import math

import jax
import jax.numpy as jnp
from jax import lax
from jax.experimental import pallas as pl
from jax.experimental.pallas import tpu as pltpu

F32 = jnp.float32
BF16 = jnp.bfloat16

N_HEADS = 8
HEAD_DIM = 64
V_DIM = 2 * HEAD_DIM
FGROUP_DIM = 128
GRID_W = 64
ROPE_BASE = 10000.0
NORM_EPS = 1e-6
SUBLN_EPS = 1e-5
N_PROJ_BLOCKS = 13
COL_Q, COL_K, COL_V, COL_ZA, COL_XIN, COL_BG, COL_CG, COL_ZC, COL_UF, COL_ZF, COL_GL = range(11)

LANES = 128
SUBLANES = 8
VMEM_LIMIT_BYTES = 56 * 1024 * 1024
MOD_ROWS_PAD = 8
ATTN_SUB_ROWS = 256
Q_SCALE = math.log2(math.e) / math.sqrt(HEAD_DIM)


def _cparams(sem):
    return pltpu.CompilerParams(dimension_semantics=sem, vmem_limit_bytes=VMEM_LIMIT_BYTES)


def _mod_kernel(cond_ref, w_ref, b_ref, o_ref):
    cond = cond_ref[...]
    a = cond * jax.nn.sigmoid(cond)
    o_ref[...] = jnp.dot(a, w_ref[...], preferred_element_type=F32,
                         precision=lax.Precision.HIGHEST) + b_ref[...]


def _modulation(cond, w_mod, b_mod):
    depth, d, w3 = w_mod.shape
    rows = cond.shape[0]
    tn = d
    return pl.pallas_call(
        _mod_kernel,
        grid=(depth, w3 // tn),
        in_specs=[
            pl.BlockSpec((rows, d), lambda l, j: (0, 0)),
            pl.BlockSpec((None, d, tn), lambda l, j: (l, 0, j)),
            pl.BlockSpec((None, 1, tn), lambda l, j: (l, 0, j)),
        ],
        out_specs=pl.BlockSpec((None, rows, tn), lambda l, j: (l, 0, j)),
        out_shape=jax.ShapeDtypeStruct((depth, rows, w3), F32),
        compiler_params=_cparams(("parallel", "parallel")),
        name="modulation",
    )(cond, w_mod, b_mod.reshape(depth, 1, w3))


def _swap_halves16(a):
    lane = lax.broadcasted_iota(jnp.int32, a.shape, 1)
    first = (lane % 32) < 16
    return jnp.where(first, pltpu.roll(a, LANES - 16, axis=1), pltpu.roll(a, 16, axis=1))


def _proj_kernel(x_ref, sh_ref, sc_ref, nw_ref, w_ref, *rest, rope, scale_first):
    if rope:
        cos_ref, sin_ref, o_ref, h_ref = rest
    else:
        o_ref, h_ref = rest
    j = pl.program_id(2)

    @pl.when(j == 0)
    def _():
        x = x_ref[0]
        y = x * lax.rsqrt(jnp.mean(x * x, axis=-1, keepdims=True) + NORM_EPS)
        h = (y * nw_ref[...]) * (1.0 + sc_ref[0]) + sh_ref[0]
        h_ref[...] = h.astype(BF16)

    acc = jnp.dot(h_ref[...], w_ref[...], preferred_element_type=F32)
    tn = acc.shape[1]

    if rope:
        cos = cos_ref[...]
        sin = sin_ref[...]
        for hh in range(tn // LANES):
            a = acc[:, hh * LANES:(hh + 1) * LANES]
            r = a * cos + _swap_halves16(a) * sin
            o_ref[0, :, hh * LANES:(hh + 1) * LANES] = r.astype(BF16)
    elif scale_first:
        scale = jnp.where(j == 0, Q_SCALE, 1.0).astype(F32)
        o_ref[0] = (acc * scale).astype(BF16)
    else:
        o_ref[0] = acc.astype(BF16)


def _proj(x, mod3, mod_row, norm_w, w_in_b, layer, col0, ncols, rope_tabs, tm):
    bx, sx, d = x.shape
    tn = d
    rope = rope_tabs is not None
    assert not rope or (col0, ncols) == (COL_Q, 2)
    in_specs = [
        pl.BlockSpec((1, tm, d), lambda b, i, j: (b, i, 0)),
        pl.BlockSpec((1, 1, d), lambda b, i, j: (mod_row(b), 0, 0)),
        pl.BlockSpec((1, 1, d), lambda b, i, j: (mod_row(b), 0, 1)),
        pl.BlockSpec((None, 1, d), lambda b, i, j: (layer, 0, 0)),
        pl.BlockSpec((None, d, tn), lambda b, i, j: (layer, 0, col0 + j)),
    ]
    args = [x, mod3, mod3, norm_w, w_in_b]
    if rope:
        cos_t, sin_t = rope_tabs
        tab_spec = pl.BlockSpec((None, tm, LANES), lambda b, i, j: (j, i, 0))
        in_specs += [tab_spec, tab_spec]
        args += [cos_t, sin_t]
    return pl.pallas_call(
        functools.partial(_proj_kernel, rope=rope, scale_first=(col0 == COL_Q)),
        grid=(bx, sx // tm, ncols),
        in_specs=in_specs,
        out_specs=pl.BlockSpec((1, tm, tn), lambda b, i, j: (b, i, j)),
        out_shape=jax.ShapeDtypeStruct((bx, sx, ncols * tn), BF16),
        scratch_shapes=[pltpu.VMEM((tm, d), BF16)],
        compiler_params=_cparams(("parallel", "parallel", "arbitrary")),
        name="proj_rope" if rope else "proj",
    )(*args)


def _attn_kernel(lq_ref, sw_ref, q_ref, k_ref, v_ref, z_ref, *rest, has_ctx, lam_init):
    if has_ctx:
        kc_ref, vc_ref, o_ref, kall_ref, vext_ref = rest
    else:
        o_ref, kall_ref, vext_ref = rest
    n_lat = k_ref.shape[1]
    n_all = kall_ref.shape[0]

    @pl.when(pl.program_id(2) == 0)
    def _():
        kall_ref[0:n_lat, :] = k_ref[0]
        vext_ref[0:n_lat, 0:V_DIM] = v_ref[0]
        if has_ctx:
            kall_ref[n_lat:n_all, :] = kc_ref[0]
            vext_ref[n_lat:n_all, 0:V_DIM] = vc_ref[0]
        ones_lane = lax.broadcasted_iota(jnp.int32, (n_all, V_DIM), 1) == 0
        vext_ref[:, V_DIM:2 * V_DIM] = jnp.where(ones_lane, 1.0, 0.0).astype(BF16)

    lq = lq_ref[...].astype(F32)
    lam = (jnp.exp(jnp.sum(lq[0:1] * lq[1:2], axis=-1, keepdims=True))
           - jnp.exp(jnp.sum(lq[2:3] * lq[3:4], axis=-1, keepdims=True)) + lam_init)

    tq = q_ref.shape[1]
    sub = min(tq, ATTN_SUB_ROWS)
    lane = lax.broadcasted_iota(jnp.int32, (sub, V_DIM), 1)

    def scores(j):
        q = q_ref[0, j * sub:(j + 1) * sub, :]
        zero = jnp.zeros_like(q)
        qq = jnp.concatenate([jnp.where(lane < HEAD_DIM, q, zero),
                              jnp.where(lane >= HEAD_DIM, q, zero)], axis=0)
        return lax.dot_general(qq, kall_ref[...], (((1,), (1,)), ((), ())),
                               preferred_element_type=F32)

    def finish(j, s):
        m = jnp.max(s, axis=-1, keepdims=True)
        p = jnp.exp2(s - m).astype(BF16)
        nd = jnp.dot(p, vext_ref[...], preferred_element_type=F32)
        o = nd[:, 0:V_DIM] / nd[:, V_DIM:V_DIM + 1]
        o = o[:sub] - lam * o[sub:]
        y = o * lax.rsqrt(jnp.mean(o * o, axis=-1, keepdims=True) + SUBLN_EPS)
        y = (y * sw_ref[...]) * (1.0 - lam_init)
        z = z_ref[0, j * sub:(j + 1) * sub, :].astype(F32)
        o_ref[0, j * sub:(j + 1) * sub, :] = (y * _silu(z)).astype(BF16)

    n_sub = tq // sub
    s_cur = scores(0)
    for j in range(n_sub):
        s_next = scores(j + 1) if j + 1 < n_sub else None
        finish(j, s_cur)
        s_cur = s_next


def _attention(lambda_qk, subln_w, layer, lam_init, q, k, v, z, kc, vc, tq):
    b, s, _ = q[0].shape
    nh = N_HEADS
    has_ctx = kc is not None

    def head_cols(rows, col):
        return pl.BlockSpec((1, rows, V_DIM), lambda bb, h, i: (bb, 0, col * nh + h))

    in_specs = [
        pl.BlockSpec((None, 4, HEAD_DIM), lambda bb, h, i: (layer, 0, 0)),
        pl.BlockSpec((None, 1, V_DIM), lambda bb, h, i: (layer, 0, 0)),
        pl.BlockSpec((1, tq, V_DIM), lambda bb, h, i: (bb, i, q[1] * nh + h)),
        head_cols(s, k[1]),
        head_cols(s, v[1]),
        pl.BlockSpec((1, tq, V_DIM), lambda bb, h, i: (bb, i, z[1] * nh + h)),
    ]
    args = [lambda_qk, subln_w, q[0], k[0], v[0], z[0]]
    n_keys = s
    if has_ctx:
        lc = kc[0].shape[1]
        n_keys = s + lc
        in_specs += [head_cols(lc, kc[1]), head_cols(lc, vc[1])]
        args += [kc[0], vc[0]]
    return pl.pallas_call(
        functools.partial(_attn_kernel, has_ctx=has_ctx, lam_init=lam_init),
        grid=(b, nh, s // tq),
        in_specs=in_specs,
        out_specs=pl.BlockSpec((1, tq, V_DIM), lambda bb, h, i: (bb, i, h)),
        out_shape=jax.ShapeDtypeStruct((b, s, nh * V_DIM), BF16),
        scratch_shapes=[pltpu.VMEM((n_keys, V_DIM), BF16), pltpu.VMEM((n_keys, 2 * V_DIM), BF16)],
        compiler_params=_cparams(("parallel", "parallel", "arbitrary")),
        name="attn_ctx" if has_ctx else "attn",
    )(*args)


def _fourier_kernel(u_ref, cs_ref, w_ref, z_ref, o_ref, ab_ref):
    r = pl.program_id(1)
    n = u_ref.shape[1]

    @pl.when(r == 0)
    def _():
        for g in range(u_ref.shape[2] // FGROUP_DIM):
            cols = slice(g * FGROUP_DIM, (g + 1) * FGROUP_DIM)
            ab = jnp.dot(u_ref[0, :, cols], cs_ref[...], preferred_element_type=F32)
            ab_ref[0:n, cols] = ab[:, :FGROUP_DIM].astype(BF16)
            ab_ref[n:2 * n, cols] = ab[:, FGROUP_DIM:].astype(BF16)

    y = jnp.dot(w_ref[...], ab_ref[...], preferred_element_type=F32)
    o_ref[0] = (y * _silu(z_ref[0].astype(F32))).astype(BF16)


def _dft_cos_sin(n):
    k = jnp.arange(n, dtype=jnp.int32)
    ang = ((k[:, None] * k[None, :]) % n).astype(F32) * (2.0 * math.pi / n)
    scale = 1.0 / math.sqrt(n)
    return jnp.cos(ang) * scale, jnp.sin(ang) * scale


def _fourier(p, col0, cs_chan, w_pos, tr):
    b, n, _ = p.shape
    d = N_HEADS * V_DIM
    return pl.pallas_call(
        _fourier_kernel,
        grid=(b, n // tr),
        in_specs=[
            pl.BlockSpec((1, n, d), lambda bb, r: (bb, 0, COL_UF - col0)),
            pl.BlockSpec((FGROUP_DIM, 2 * FGROUP_DIM), lambda bb, r: (0, 0)),
            pl.BlockSpec((tr, 2 * n), lambda bb, r: (r, 0)),
            pl.BlockSpec((1, tr, d), lambda bb, r: (bb, r, COL_ZF - col0)),
        ],
        out_specs=pl.BlockSpec((1, tr, d), lambda bb, r: (bb, r, 0)),
        out_shape=jax.ShapeDtypeStruct((b, n, d), BF16),
        scratch_shapes=[pltpu.VMEM((2 * n, d), BF16)],
        compiler_params=_cparams(("parallel", "arbitrary")),
        name="fourier",
    )(p, cs_chan, w_pos, p)


def _fourier_half_kernel(u_ref, cs_ref, c_ref, s_ref, rev_ref, zlo_ref, zhi_ref, o_ref, a_ref, b_ref):
    r = pl.program_id(1)
    tr = o_ref.shape[3]

    @pl.when(r == 0)
    def _():
        for g in range(u_ref.shape[2] // FGROUP_DIM):
            cols = slice(g * FGROUP_DIM, (g + 1) * FGROUP_DIM)
            ab = jnp.dot(u_ref[0, :, cols], cs_ref[...], preferred_element_type=F32)
            a_ref[:, cols] = ab[:, :FGROUP_DIM].astype(BF16)
            b_ref[:, cols] = ab[:, FGROUP_DIM:].astype(BF16)

    p = jnp.dot(c_ref[...], a_ref[...], preferred_element_type=F32)
    q = jnp.dot(s_ref[...], b_ref[...], preferred_element_type=F32)
    o_ref[0, 0, 0] = ((p[:tr] - q[:tr]) * _silu(zlo_ref[0].astype(F32))).astype(BF16)
    hi = jnp.dot(rev_ref[...], (p + q).astype(BF16), preferred_element_type=F32)
    o_ref[0, 0, 1] = (hi * _silu(zhi_ref[0].astype(F32))).astype(BF16)


def _fourier_half_tables(n, tr):
    ext = 2 * SUBLANES
    n_tiles = n // (2 * tr)
    rows = (jnp.arange(n_tiles, dtype=jnp.int32)[:, None] * tr
            + jnp.arange(tr + ext, dtype=jnp.int32)[None, :])
    k = jnp.arange(n, dtype=jnp.int32)
    ang = ((rows[:, :, None] * k[None, None, :]) % n).astype(F32) * (2.0 * math.pi / n)
    scale = 1.0 / math.sqrt(n)
    i = jnp.arange(tr, dtype=jnp.int32)[:, None]
    j = jnp.arange(tr + ext, dtype=jnp.int32)[None, :]
    rev = (j == tr - i).astype(BF16)
    return (jnp.cos(ang) * scale).astype(BF16), (jnp.sin(ang) * scale).astype(BF16), rev


def _fourier_half(p, col0, cs_chan, tabs, tr):
    b, n, _ = p.shape
    d = N_HEADS * V_DIM
    c_t, s_t, rev = tabs
    n_tiles, rows_ext, _ = c_t.shape
    return pl.pallas_call(
        _fourier_half_kernel,
        grid=(b, n_tiles),
        in_specs=[
            pl.BlockSpec((1, n, d), lambda bb, r: (bb, 0, COL_UF - col0)),
            pl.BlockSpec((FGROUP_DIM, 2 * FGROUP_DIM), lambda bb, r: (0, 0)),
            pl.BlockSpec((None, rows_ext, n), lambda bb, r: (r, 0, 0)),
            pl.BlockSpec((None, rows_ext, n), lambda bb, r: (r, 0, 0)),
            pl.BlockSpec((tr, rows_ext), lambda bb, r: (0, 0)),
            pl.BlockSpec((1, tr, d), lambda bb, r: (bb, r, COL_ZF - col0)),
            pl.BlockSpec((1, tr, d), lambda bb, r: (bb, 2 * n_tiles - 1 - r, COL_ZF - col0)),
        ],
        out_specs=pl.BlockSpec((1, 1, 2, tr, d), lambda bb, r: (bb, r, 0, 0, 0)),
        out_shape=jax.ShapeDtypeStruct((b, n_tiles, 2, tr, d), BF16),
        scratch_shapes=[pltpu.VMEM((n, d), BF16), pltpu.VMEM((n, d), BF16)],
        compiler_params=_cparams(("parallel", "arbitrary")),
        name="fourier_half",
    )(p, cs_chan, c_t, s_t, rev, p, p)


def _sigmoid(z):
    return 0.5 * jnp.tanh(0.5 * z) + 0.5


def _silu(z):
    h = 0.5 * z
    return h * jnp.tanh(h) + h


def _merge_kernel(ya_ref, yf_ref, xin_ref, bg_ref, cg_ref, zc_ref,
                  ga_ref, gc_ref, gf_ref, xin_p_ref, cg_p_ref, xin_n_ref, cg_n_ref,
                  x_ref, gate_ref, cw_ref, wa_ref, wc_ref, wf_ref, wo_ref, o_ref):
    i = pl.program_id(1)
    tm = x_ref.shape[1]
    pad = SUBLANES

    u = cg_ref[0].astype(F32) * xin_ref[0].astype(F32)
    up = cg_p_ref[0, pad - 1:pad, :].astype(F32) * xin_p_ref[0, pad - 1:pad, :].astype(F32)
    un = cg_n_ref[0, 0:1, :].astype(F32) * xin_n_ref[0, 0:1, :].astype(F32)
    up = jnp.where(i == 0, 0.0, up)
    un = jnp.where(i == pl.num_programs(1) - 1, 0.0, un)
    row = lax.broadcasted_iota(jnp.int32, (tm, 1), 0)
    u_prev = jnp.where(row == 0, up, pltpu.roll(u, 1, axis=0))
    u_next = jnp.where(row == tm - 1, un, pltpu.roll(u, tm - 1, axis=0))
    cw = cw_ref[...]
    conv = cw[0:1] * u_prev + cw[1:2] * u + cw[2:3] * u_next
    y_c = bg_ref[0].astype(F32) * conv

    def branch(y_gated, w_ref, g_ref):
        t = jnp.dot(y_gated, w_ref[...], preferred_element_type=F32)
        return _sigmoid(g_ref[0].astype(F32)) * t

    yc_gated = (y_c * _silu(zc_ref[0].astype(F32))).astype(BF16)
    merged = (branch(ya_ref[0], wa_ref, ga_ref) + branch(yc_gated, wc_ref, gc_ref)
              + branch(yf_ref[...], wf_ref, gf_ref))
    out = jnp.dot(merged.astype(BF16), wo_ref[...], preferred_element_type=F32)
    o_ref[0] = x_ref[0] + gate_ref[0] * out


def _merge(x, p, col0, ya, yf, yf_spec, mod3, mod_row, conv_w, wa, wc, wf, wo, layer, tm):
    bx, sx, d = x.shape
    nb = tm // SUBLANES
    last_halo = sx // SUBLANES - 1

    def whole(b, i):
        return (b, i, 0)

    def col(cb):
        return pl.BlockSpec((1, tm, d), lambda b, i: (b, i, cb - col0))

    def halo_prev(cb):
        return pl.BlockSpec((1, SUBLANES, d),
                            lambda b, i: (b, jnp.maximum(i * nb - 1, 0), cb - col0))

    def halo_next(cb):
        return pl.BlockSpec((1, SUBLANES, d),
                            lambda b, i: (b, jnp.minimum((i + 1) * nb, last_halo), cb - col0))

    def weight():
        return pl.BlockSpec((None, d, d), lambda b, i: (layer, 0, 0), pipeline_mode=pl.Buffered(1))

    in_specs = [
        pl.BlockSpec((1, tm, d), whole), yf_spec,
        col(COL_XIN), col(COL_BG), col(COL_CG), col(COL_ZC),
        col(COL_GL), col(COL_GL + 1), col(COL_GL + 2),
        halo_prev(COL_XIN), halo_prev(COL_CG), halo_next(COL_XIN), halo_next(COL_CG),
        pl.BlockSpec((1, tm, d), whole),
        pl.BlockSpec((1, 1, d), lambda b, i: (mod_row(b), 0, 2)),
        pl.BlockSpec((None, 3, d), lambda b, i: (layer, 0, 0)),
        weight(), weight(), weight(), weight(),
    ]
    return pl.pallas_call(
        _merge_kernel,
        grid=(bx, sx // tm),
        in_specs=in_specs,
        out_specs=pl.BlockSpec((1, tm, d), lambda b, i: (b, i, 0)),
        out_shape=jax.ShapeDtypeStruct((bx, sx, d), F32),
        compiler_params=_cparams(("parallel", "parallel")),
        name="merge",
    )(ya, yf, p, p, p, p, p, p, p, p, p, p, p, x, mod3, conv_w, wa, wc, wf, wo)


def _final_norm_kernel(x_ref, w_ref, o_ref):
    x = x_ref[...]
    y = x * lax.rsqrt(jnp.mean(x * x, axis=-1, keepdims=True) + NORM_EPS)
    o_ref[...] = y * w_ref[...]


def _final_norm(x, w, tm):
    b, s, d = x.shape
    x2 = x.reshape(b * s, d)
    out = pl.pallas_call(
        _final_norm_kernel,
        grid=(b * s // tm,),
        in_specs=[pl.BlockSpec((tm, d), lambda i: (i, 0)), pl.BlockSpec((1, d), lambda i: (0, 0))],
        out_specs=pl.BlockSpec((tm, d), lambda i: (i, 0)),
        out_shape=jax.ShapeDtypeStruct((b * s, d), F32),
        compiler_params=_cparams(("parallel",)),
        name="final_norm",
    )(x2, w.reshape(1, d))
    return out.reshape(b, s, d)


def _rope_tables(n_tokens):
    axis_dim = HEAD_DIM // 2
    rows = n_tokens // GRID_W
    row = jnp.repeat(jnp.arange(rows), GRID_W).astype(F32)
    col = jnp.tile(jnp.arange(GRID_W), rows).astype(F32)
    inv_freq = ROPE_BASE ** (-jnp.arange(0, axis_dim, 2, dtype=F32) / axis_dim)
    ang_r = row[:, None] * inv_freq
    ang_c = col[:, None] * inv_freq
    ang = jnp.concatenate([ang_r, ang_r, ang_c, ang_c], axis=-1)
    ang = jnp.concatenate([ang, ang], axis=-1)
    half = axis_dim // 2
    sign = jnp.where((jnp.arange(V_DIM) % axis_dim) < half, -1.0, 1.0).astype(F32)
    cos = jnp.cos(ang)
    sin = jnp.sin(ang) * sign
    return jnp.stack([cos * Q_SCALE, cos]), jnp.stack([sin * Q_SCALE, sin])


def _fourier_tables(n):
    cn, sn = _dft_cos_sin(n)
    return jnp.concatenate([cn, -sn], axis=1).astype(BF16)


def _pick_tile(n, target):
    t = min(n, target)
    while n % t:
        t //= 2
    return t


def kernel(x, c, ctx, c_ctx, norm_w, w_mod, b_mod, w_in, lambda_qk, subln_w, conv_w,
           w_attn_o, w_conv_o, w_four_o, w_out, final_norm_w):
    b, s, d = x.shape
    lc = ctx.shape[1]
    depth = w_in.shape[0]
    assert d == N_HEADS * V_DIM and w_in.shape[2] == N_PROJ_BLOCKS * d
    assert s % GRID_W == 0 and s % LANES == 0 and lc % SUBLANES == 0

    pad = (-(b + 1)) % MOD_ROWS_PAD
    cond = jnp.concatenate([c, c_ctx[None, :], jnp.zeros((pad, d), F32)], axis=0)
    mod = _modulation(cond, w_mod, b_mod)
    n_rows = cond.shape[0]

    norm_w = norm_w.reshape(depth, 1, d)
    subln_w = subln_w.reshape(depth, 1, V_DIM)
    w_in_b = w_in.astype(BF16)
    wa_b, wc_b, wf_b, wo_b = (w.astype(BF16) for w in (w_attn_o, w_conv_o, w_four_o, w_out))

    rope_tabs = _rope_tables(s)
    cc, sc = _dft_cos_sin(FGROUP_DIM)
    cs_chan = jnp.concatenate([cc, sc], axis=1).astype(BF16)
    w_pos_ctx = _fourier_tables(lc)

    ctx_flat = ctx.reshape(1, b * lc, d)
    lat_row = lambda bb: bb
    ctx_row = lambda bb: b

    tm_lat = _pick_tile(s, 1024)
    tm_ctx = _pick_tile(b * lc, 1024)
    tq_lat = _pick_tile(s, 2048)
    tmm_lat = _pick_tile(s // 2, 512)
    tmm_ctx = _pick_tile(lc, 256)
    four_tabs = _fourier_half_tables(s, tmm_lat)
    n_ftiles = s // (2 * tmm_lat)
    yf_lat_spec = pl.BlockSpec(
        (None, None, None, tmm_lat, d),
        lambda bb, i: (bb, jnp.where(i < n_ftiles, i, 2 * n_ftiles - 1 - i),
                       jnp.where(i < n_ftiles, 0, 1), 0, 0))
    yf_ctx_spec = pl.BlockSpec((None, tmm_ctx, d), lambda bb, i: (bb, i, 0))

    for l in range(depth):
        last = l == depth - 1
        lam_init = 0.8 - 0.6 * math.exp(-0.3 * l)
        mod3 = mod[l].reshape(n_rows, 1, 3 * d)

        p_qk = _proj(x, mod3, lat_row, norm_w, w_in_b, l, COL_Q, 2, rope_tabs, tm_lat)
        p_lat = _proj(x, mod3, lat_row, norm_w, w_in_b, l, COL_V, N_PROJ_BLOCKS - COL_V, None, tm_lat)
        c0, cn = (COL_K, 2) if last else (COL_Q, N_PROJ_BLOCKS)
        p_ctx = _proj(ctx_flat, mod3, ctx_row, norm_w, w_in_b, l, c0, cn, None, tm_ctx).reshape(b, lc, -1)

        ya = _attention(lambda_qk, subln_w, l, lam_init, (p_qk, COL_Q), (p_qk, COL_K),
                        (p_lat, 0), (p_lat, COL_ZA - COL_V),
                        (p_ctx, COL_K - c0), (p_ctx, COL_V - c0), tq_lat)
        yf = _fourier_half(p_lat, COL_V, cs_chan, four_tabs, tmm_lat)
        x_new = _merge(x, p_lat, COL_V, ya, yf, yf_lat_spec, mod3, lat_row, conv_w, wa_b, wc_b, wf_b,
                       wo_b, l, tmm_lat)

        if not last:
            yac = _attention(lambda_qk, subln_w, l, lam_init, (p_ctx, COL_Q), (p_ctx, COL_K),
                             (p_ctx, COL_V), (p_ctx, COL_ZA), None, None, lc)
            yfc = _fourier(p_ctx, COL_Q, cs_chan, w_pos_ctx, lc)
            ctx3 = ctx_flat.reshape(b, lc, d)
            ctx_flat = _merge(ctx3, p_ctx, COL_Q, yac, yfc, yf_ctx_spec, mod3, ctx_row, conv_w, wa_b, wc_b, wf_b,
                              wo_b, l, tmm_ctx).reshape(1, b * lc, d)
        x = x_new

    return _final_norm(x, final_norm_w, _pick_tile(b * s, 1024))
```

```python
import functools
import math

import jax
import jax.numpy as jnp
from jax import lax
from jax.experimental import pallas as pl
from jax.experimental.pallas import tpu as pltpu

F32 = jnp.float32
BF16 = jnp.bfloat16

N_HEADS = 8
HEAD_DIM = 64
V_DIM = 2 * HEAD_DIM
FGROUP_DIM = 128
GRID_W = 64
ROPE_BASE = 10000.0
NORM_EPS = 1e-6
SUBLN_EPS = 1e-5
N_PROJ_BLOCKS = 13
COL_Q, COL_K, COL_V, COL_ZA, COL_XIN, COL_BG, COL_CG, COL_ZC, COL_UF, COL_ZF, COL_GL = range(11)

LANES = 128
SUBLANES = 8
VMEM_LIMIT_BYTES = 56 * 1024 * 1024
MOD_ROWS_PAD = 8
ATTN_SUB_ROWS = 256
Q_SCALE = math.log2(math.e) / math.sqrt(HEAD_DIM)


def _cparams(sem):
    return pltpu.CompilerParams(dimension_semantics=sem, vmem_limit_bytes=VMEM_LIMIT_BYTES)


def _mod_kernel(cond_ref, w_ref, b_ref, o_ref):
    cond = cond_ref[...]
    a = cond * jax.nn.sigmoid(cond)
    o_ref[...] = jnp.dot(a, w_ref[...], preferred_element_type=F32,
                         precision=lax.Precision.HIGHEST) + b_ref[...]


def _modulation(cond, w_mod, b_mod):
    depth, d, w3 = w_mod.shape
    rows = cond.shape[0]
    tn = d
    return pl.pallas_call(
        _mod_kernel,
        grid=(depth, w3 // tn),
        in_specs=[
            pl.BlockSpec((rows, d), lambda l, j: (0, 0)),
            pl.BlockSpec((None, d, tn), lambda l, j: (l, 0, j)),
            pl.BlockSpec((None, 1, tn), lambda l, j: (l, 0, j)),
        ],
        out_specs=pl.BlockSpec((None, rows, tn), lambda l, j: (l, 0, j)),
        out_shape=jax.ShapeDtypeStruct((depth, rows, w3), F32),
        compiler_params=_cparams(("parallel", "parallel")),
        name="modulation",
    )(cond, w_mod, b_mod.reshape(depth, 1, w3))


def _proj_kernel(x_ref, sh_ref, sc_ref, nw_ref, w_ref, *rest, rope, scale_first):
    if rope:
        cos_ref, sin_ref, o_ref, h_ref = rest
    else:
        o_ref, h_ref = rest
    j = pl.program_id(2)

    @pl.when(j == 0)
    def _():
        x = x_ref[0]
        y = x * lax.rsqrt(jnp.mean(x * x, axis=-1, keepdims=True) + NORM_EPS)
        h = (y * nw_ref[...]) * (1.0 + sc_ref[0]) + sh_ref[0]
        h_ref[...] = h.astype(BF16)

    acc = jnp.dot(h_ref[...], w_ref[...], preferred_element_type=F32)
    tn = acc.shape[1]

    if rope:
        cos = cos_ref[...]
        sin = sin_ref[...]
        for hh in range(tn // LANES):
            a = acc[:, hh * LANES:(hh + 1) * LANES]
            r = a * cos + pltpu.roll(a, LANES // 2, axis=1) * sin
            o_ref[0, :, hh * LANES:(hh + 1) * LANES] = r.astype(BF16)
    elif scale_first:
        scale = jnp.where(j == 0, Q_SCALE, 1.0).astype(F32)
        o_ref[0] = (acc * scale).astype(BF16)
    else:
        o_ref[0] = acc.astype(BF16)


def _proj(x, mod3, mod_row, norm_w, w_in_b, layer, col0, ncols, rope_tabs, tm):
    bx, sx, d = x.shape
    tn = d
    rope = rope_tabs is not None
    assert not rope or (col0, ncols) == (COL_Q, 2)
    in_specs = [
        pl.BlockSpec((1, tm, d), lambda b, i, j: (b, i, 0)),
        pl.BlockSpec((1, 1, d), lambda b, i, j: (mod_row(b), 0, 0)),
        pl.BlockSpec((1, 1, d), lambda b, i, j: (mod_row(b), 0, 1)),
        pl.BlockSpec((None, 1, d), lambda b, i, j: (layer, 0, 0)),
        pl.BlockSpec((None, d, tn), lambda b, i, j: (layer, 0, col0 + j)),
    ]
    args = [x, mod3, mod3, norm_w, w_in_b]
    if rope:
        cos_t, sin_t = rope_tabs
        tab_spec = pl.BlockSpec((None, tm, LANES), lambda b, i, j: (j, i, 0))
        in_specs += [tab_spec, tab_spec]
        args += [cos_t, sin_t]
    return pl.pallas_call(
        functools.partial(_proj_kernel, rope=rope, scale_first=(col0 == COL_Q)),
        grid=(bx, sx // tm, ncols),
        in_specs=in_specs,
        out_specs=pl.BlockSpec((1, tm, tn), lambda b, i, j: (b, i, j)),
        out_shape=jax.ShapeDtypeStruct((bx, sx, ncols * tn), BF16),
        scratch_shapes=[pltpu.VMEM((tm, d), BF16)],
        compiler_params=_cparams(("parallel", "parallel", "arbitrary")),
        name="proj_rope" if rope else "proj",
    )(*args)


def _attn_kernel(lq_ref, sw_ref, q_ref, k_ref, v_ref, z_ref, *rest, has_ctx, carry, lam_init):
    rest = list(rest)
    kc_ref, vc_ref = (rest.pop(0), rest.pop(0)) if has_ctx else (None, None)
    if carry:
        qn_ref, kn_ref = rest.pop(0), rest.pop(0)
        kcn_ref = rest.pop(0) if has_ctx else None
        o_ref, kall_ref, vext_ref, kalln_ref, scarry_ref = rest
    else:
        o_ref, kall_ref, vext_ref = rest
    n_lat = k_ref.shape[1]
    n_all = kall_ref.shape[0]

    def gather_keys(dst_ref, lat_ref, ctx_ref):
        dst_ref[0:n_lat, :] = lat_ref[0]
        if has_ctx:
            dst_ref[n_lat:n_all, :] = ctx_ref[0]

    gather_keys(kall_ref, k_ref, kc_ref)
    vext_ref[0:n_lat, 0:V_DIM] = v_ref[0]
    if has_ctx:
        vext_ref[n_lat:n_all, 0:V_DIM] = vc_ref[0]
    ones_lane = lax.broadcasted_iota(jnp.int32, (n_all, V_DIM), 1) == 0
    vext_ref[:, V_DIM:2 * V_DIM] = jnp.where(ones_lane, 1.0, 0.0).astype(BF16)

    lq = lq_ref[...].astype(F32)
    lam = (jnp.exp(jnp.sum(lq[0:1] * lq[1:2], axis=-1, keepdims=True))
           - jnp.exp(jnp.sum(lq[2:3] * lq[3:4], axis=-1, keepdims=True)) + lam_init)

    tq = q_ref.shape[1]
    sub = min(tq, ATTN_SUB_ROWS)
    lane = lax.broadcasted_iota(jnp.int32, (sub, V_DIM), 1)

    def scores(q, keys_ref):
        zero = jnp.zeros_like(q)
        map1 = (lane % HEAD_DIM) < HEAD_DIM // 2
        qq = jnp.concatenate([jnp.where(map1, q, zero), jnp.where(map1, zero, q)], axis=0)
        return lax.dot_general(qq, keys_ref[...], (((1,), (1,)), ((), ())),
                               preferred_element_type=F32)

    def finish(j, get_s):
        m = jnp.max(get_s(), axis=-1, keepdims=True)
        p = jnp.exp2(get_s() - m).astype(BF16)
        nd = jnp.dot(p, vext_ref[...], preferred_element_type=F32)
        o = nd[:, 0:V_DIM] / nd[:, V_DIM:V_DIM + 1]
        o = o[:sub] - lam * o[sub:]
        y = o * lax.rsqrt(jnp.mean(o * o, axis=-1, keepdims=True) + SUBLN_EPS)
        y = (y * sw_ref[...]) * (1.0 - lam_init)
        z = z_ref[0, j * sub:(j + 1) * sub, :].astype(F32)
        o_ref[0, j * sub:(j + 1) * sub, :] = (y * _silu(z)).astype(BF16)

    def value(v):
        return lambda: v

    n_sub = tq // sub
    if carry:
        @pl.when((pl.program_id(0) == 0) & (pl.program_id(1) == 0))
        def _():
            scarry_ref[...] = scores(q_ref[0, 0:sub, :], kall_ref)

        gather_keys(kalln_ref, kn_ref, kcn_ref)
        get_s = lambda: scarry_ref[...]
    else:
        get_s = value(scores(q_ref[0, 0:sub, :], kall_ref))

    for j in range(n_sub):
        if j + 1 < n_sub:
            s_next = scores(q_ref[0, (j + 1) * sub:(j + 2) * sub, :], kall_ref)
        elif carry:
            s_next = scores(qn_ref[0], kalln_ref)
        else:
            s_next = None
        finish(j, get_s)
        get_s = value(s_next)
    if carry:
        scarry_ref[...] = s_next


def _attention(lambda_qk, subln_w, layer, lam_init, q, k, v, z, kc, vc, carry):
    b, s, _ = q[0].shape
    nh = N_HEADS
    has_ctx = kc is not None
    sub = min(s, ATTN_SUB_ROWS)

    def head_cols(rows, col):
        return pl.BlockSpec((1, rows, V_DIM), lambda bb, h: (bb, 0, col * nh + h))

    def next_head_cols(rows, col):
        def index(bb, h):
            flat = jnp.minimum(bb * nh + h + 1, b * nh - 1)
            return (flat // nh, 0, col * nh + flat % nh)
        return pl.BlockSpec((1, rows, V_DIM), index)

    in_specs = [
        pl.BlockSpec((None, 4, HEAD_DIM), lambda bb, h: (layer, 0, 0)),
        pl.BlockSpec((None, 1, V_DIM), lambda bb, h: (layer, 0, 0)),
        head_cols(s, q[1]), head_cols(s, k[1]), head_cols(s, v[1]), head_cols(s, z[1]),
    ]
    args = [lambda_qk, subln_w, q[0], k[0], v[0], z[0]]
    n_keys = s
    if has_ctx:
        lc = kc[0].shape[1]
        n_keys = s + lc
        in_specs += [head_cols(lc, kc[1]), head_cols(lc, vc[1])]
        args += [kc[0], vc[0]]
    scratch = [pltpu.VMEM((n_keys, V_DIM), BF16), pltpu.VMEM((n_keys, 2 * V_DIM), BF16)]
    if carry:
        in_specs += [next_head_cols(sub, q[1]), next_head_cols(s, k[1])]
        args += [q[0], k[0]]
        if has_ctx:
            in_specs.append(next_head_cols(lc, kc[1]))
            args.append(kc[0])
        scratch += [pltpu.VMEM((n_keys, V_DIM), BF16), pltpu.VMEM((2 * sub, n_keys), F32)]
    return pl.pallas_call(
        functools.partial(_attn_kernel, has_ctx=has_ctx, carry=carry, lam_init=lam_init),
        grid=(b, nh),
        in_specs=in_specs,
        out_specs=head_cols(s, 0),
        out_shape=jax.ShapeDtypeStruct((b, s, nh * V_DIM), BF16),
        scratch_shapes=scratch,
        compiler_params=_cparams(("arbitrary", "arbitrary")),
        name="attn_ctx" if has_ctx else "attn",
    )(*args)


def _fourier_kernel(u_ref, cs_ref, w_ref, z_ref, o_ref, ab_ref):
    r = pl.program_id(1)
    n = u_ref.shape[1]

    @pl.when(r == 0)
    def _():
        for g in range(u_ref.shape[2] // FGROUP_DIM):
            cols = slice(g * FGROUP_DIM, (g + 1) * FGROUP_DIM)
            ab = jnp.dot(u_ref[0, :, cols], cs_ref[...], preferred_element_type=F32)
            ab_ref[0:n, cols] = ab[:, :FGROUP_DIM].astype(BF16)
            ab_ref[n:2 * n, cols] = ab[:, FGROUP_DIM:].astype(BF16)

    y = jnp.dot(w_ref[...], ab_ref[...], preferred_element_type=F32)
    o_ref[0] = (y * _silu(z_ref[0].astype(F32))).astype(BF16)


def _dft_cos_sin(n):
    k = jnp.arange(n, dtype=jnp.int32)
    ang = ((k[:, None] * k[None, :]) % n).astype(F32) * (2.0 * math.pi / n)
    scale = 1.0 / math.sqrt(n)
    return jnp.cos(ang) * scale, jnp.sin(ang) * scale


def _fourier(p, col0, cs_chan, w_pos, tr):
    b, n, _ = p.shape
    d = N_HEADS * V_DIM
    return pl.pallas_call(
        _fourier_kernel,
        grid=(b, n // tr),
        in_specs=[
            pl.BlockSpec((1, n, d), lambda bb, r: (bb, 0, COL_UF - col0)),
            pl.BlockSpec((FGROUP_DIM, 2 * FGROUP_DIM), lambda bb, r: (0, 0)),
            pl.BlockSpec((tr, 2 * n), lambda bb, r: (r, 0)),
            pl.BlockSpec((1, tr, d), lambda bb, r: (bb, r, COL_ZF - col0)),
        ],
        out_specs=pl.BlockSpec((1, tr, d), lambda bb, r: (bb, r, 0)),
        out_shape=jax.ShapeDtypeStruct((b, n, d), BF16),
        scratch_shapes=[pltpu.VMEM((2 * n, d), BF16)],
        compiler_params=_cparams(("parallel", "arbitrary")),
        name="fourier",
    )(p, cs_chan, w_pos, p)


def _fourier_half_kernel(u_ref, cs_ref, c_ref, s_ref, rev_ref, zlo_ref, zhi_ref, o_ref, a_ref, b_ref):
    r = pl.program_id(1)
    tr = o_ref.shape[3]

    @pl.when(r == 0)
    def _():
        for g in range(u_ref.shape[2] // FGROUP_DIM):
            cols = slice(g * FGROUP_DIM, (g + 1) * FGROUP_DIM)
            ab = jnp.dot(u_ref[0, :, cols], cs_ref[...], preferred_element_type=F32)
            a_ref[:, cols] = ab[:, :FGROUP_DIM].astype(BF16)
            b_ref[:, cols] = ab[:, FGROUP_DIM:].astype(BF16)

    p = jnp.dot(c_ref[...], a_ref[...], preferred_element_type=F32)
    q = jnp.dot(s_ref[...], b_ref[...], preferred_element_type=F32)
    o_ref[0, 0, 0] = ((p[:tr] - q[:tr]) * _silu(zlo_ref[0].astype(F32))).astype(BF16)
    hi = jnp.dot(rev_ref[...], (p + q).astype(BF16), preferred_element_type=F32)
    o_ref[0, 0, 1] = (hi * _silu(zhi_ref[0].astype(F32))).astype(BF16)


def _fourier_half_tables(n, tr):
    ext = 2 * SUBLANES
    n_tiles = n // (2 * tr)
    rows = (jnp.arange(n_tiles, dtype=jnp.int32)[:, None] * tr
            + jnp.arange(tr + ext, dtype=jnp.int32)[None, :])
    k = jnp.arange(n, dtype=jnp.int32)
    ang = ((rows[:, :, None] * k[None, None, :]) % n).astype(F32) * (2.0 * math.pi / n)
    scale = 1.0 / math.sqrt(n)
    i = jnp.arange(tr, dtype=jnp.int32)[:, None]
    j = jnp.arange(tr + ext, dtype=jnp.int32)[None, :]
    rev = (j == tr - i).astype(BF16)
    return (jnp.cos(ang) * scale).astype(BF16), (jnp.sin(ang) * scale).astype(BF16), rev


def _fourier_half(p, col0, cs_chan, tabs, tr):
    b, n, _ = p.shape
    d = N_HEADS * V_DIM
    c_t, s_t, rev = tabs
    n_tiles, rows_ext, _ = c_t.shape
    return pl.pallas_call(
        _fourier_half_kernel,
        grid=(b, n_tiles),
        in_specs=[
            pl.BlockSpec((1, n, d), lambda bb, r: (bb, 0, COL_UF - col0)),
            pl.BlockSpec((FGROUP_DIM, 2 * FGROUP_DIM), lambda bb, r: (0, 0)),
            pl.BlockSpec((None, rows_ext, n), lambda bb, r: (r, 0, 0)),
            pl.BlockSpec((None, rows_ext, n), lambda bb, r: (r, 0, 0)),
            pl.BlockSpec((tr, rows_ext), lambda bb, r: (0, 0)),
            pl.BlockSpec((1, tr, d), lambda bb, r: (bb, r, COL_ZF - col0)),
            pl.BlockSpec((1, tr, d), lambda bb, r: (bb, 2 * n_tiles - 1 - r, COL_ZF - col0)),
        ],
        out_specs=pl.BlockSpec((1, 1, 2, tr, d), lambda bb, r: (bb, r, 0, 0, 0)),
        out_shape=jax.ShapeDtypeStruct((b, n_tiles, 2, tr, d), BF16),
        scratch_shapes=[pltpu.VMEM((n, d), BF16), pltpu.VMEM((n, d), BF16)],
        compiler_params=_cparams(("parallel", "arbitrary")),
        name="fourier_half",
    )(p, cs_chan, c_t, s_t, rev, p, p)


def _sigmoid(z):
    return 0.5 * jnp.tanh(0.5 * z) + 0.5


def _silu(z):
    h = 0.5 * z
    return h * jnp.tanh(h) + h


def _merge_kernel(ya_ref, yf_ref, xin_ref, bg_ref, cg_ref, zc_ref,
                  ga_ref, gc_ref, gf_ref, xin_p_ref, cg_p_ref, xin_n_ref, cg_n_ref,
                  x_ref, gate_ref, cw_ref, wa_ref, wc_ref, wf_ref, wo_ref, o_ref):
    i = pl.program_id(1)
    tm = x_ref.shape[1]
    pad = SUBLANES

    u = cg_ref[0].astype(F32) * xin_ref[0].astype(F32)
    up = cg_p_ref[0, pad - 1:pad, :].astype(F32) * xin_p_ref[0, pad - 1:pad, :].astype(F32)
    un = cg_n_ref[0, 0:1, :].astype(F32) * xin_n_ref[0, 0:1, :].astype(F32)
    up = jnp.where(i == 0, 0.0, up)
    un = jnp.where(i == pl.num_programs(1) - 1, 0.0, un)
    row = lax.broadcasted_iota(jnp.int32, (tm, 1), 0)
    u_prev = jnp.where(row == 0, up, pltpu.roll(u, 1, axis=0))
    u_next = jnp.where(row == tm - 1, un, pltpu.roll(u, tm - 1, axis=0))
    cw = cw_ref[...]
    conv = cw[0:1] * u_prev + cw[1:2] * u + cw[2:3] * u_next
    y_c = bg_ref[0].astype(F32) * conv

    def branch(y_gated, w_ref, g_ref):
        t = jnp.dot(y_gated, w_ref[...], preferred_element_type=F32)
        return _sigmoid(g_ref[0].astype(F32)) * t

    yc_gated = (y_c * _silu(zc_ref[0].astype(F32))).astype(BF16)
    merged = (branch(ya_ref[0], wa_ref, ga_ref) + branch(yc_gated, wc_ref, gc_ref)
              + branch(yf_ref[...], wf_ref, gf_ref))
    out = jnp.dot(merged.astype(BF16), wo_ref[...], preferred_element_type=F32)
    o_ref[0] = x_ref[0] + gate_ref[0] * out


def _merge(x, p, col0, ya, yf, yf_spec, mod3, mod_row, conv_w, wa, wc, wf, wo, layer, tm):
    bx, sx, d = x.shape
    nb = tm // SUBLANES
    last_halo = sx // SUBLANES - 1

    def whole(b, i):
        return (b, i, 0)

    def col(cb):
        return pl.BlockSpec((1, tm, d), lambda b, i: (b, i, cb - col0))

    def halo_prev(cb):
        return pl.BlockSpec((1, SUBLANES, d),
                            lambda b, i: (b, jnp.maximum(i * nb - 1, 0), cb - col0))

    def halo_next(cb):
        return pl.BlockSpec((1, SUBLANES, d),
                            lambda b, i: (b, jnp.minimum((i + 1) * nb, last_halo), cb - col0))

    def weight():
        return pl.BlockSpec((None, d, d), lambda b, i: (layer, 0, 0), pipeline_mode=pl.Buffered(1))

    in_specs = [
        pl.BlockSpec((1, tm, d), whole), yf_spec,
        col(COL_XIN), col(COL_BG), col(COL_CG), col(COL_ZC),
        col(COL_GL), col(COL_GL + 1), col(COL_GL + 2),
        halo_prev(COL_XIN), halo_prev(COL_CG), halo_next(COL_XIN), halo_next(COL_CG),
        pl.BlockSpec((1, tm, d), whole),
        pl.BlockSpec((1, 1, d), lambda b, i: (mod_row(b), 0, 2)),
        pl.BlockSpec((None, 3, d), lambda b, i: (layer, 0, 0)),
        weight(), weight(), weight(), weight(),
    ]
    return pl.pallas_call(
        _merge_kernel,
        grid=(bx, sx // tm),
        in_specs=in_specs,
        out_specs=pl.BlockSpec((1, tm, d), lambda b, i: (b, i, 0)),
        out_shape=jax.ShapeDtypeStruct((bx, sx, d), F32),
        compiler_params=_cparams(("parallel", "parallel")),
        name="merge",
    )(ya, yf, p, p, p, p, p, p, p, p, p, p, p, x, mod3, conv_w, wa, wc, wf, wo)


def _final_norm_kernel(x_ref, w_ref, o_ref):
    x = x_ref[...]
    y = x * lax.rsqrt(jnp.mean(x * x, axis=-1, keepdims=True) + NORM_EPS)
    o_ref[...] = y * w_ref[...]


def _final_norm(x, w, tm):
    b, s, d = x.shape
    x2 = x.reshape(b * s, d)
    out = pl.pallas_call(
        _final_norm_kernel,
        grid=(b * s // tm,),
        in_specs=[pl.BlockSpec((tm, d), lambda i: (i, 0)), pl.BlockSpec((1, d), lambda i: (0, 0))],
        out_specs=pl.BlockSpec((tm, d), lambda i: (i, 0)),
        out_shape=jax.ShapeDtypeStruct((b * s, d), F32),
        compiler_params=_cparams(("parallel",)),
        name="final_norm",
    )(x2, w.reshape(1, d))
    return out.reshape(b, s, d)


def _rope_tables(n_tokens):
    axis_dim = HEAD_DIM // 2
    rows = n_tokens // GRID_W
    row = jnp.repeat(jnp.arange(rows), GRID_W).astype(F32)
    col = jnp.tile(jnp.arange(GRID_W), rows).astype(F32)
    inv_freq = ROPE_BASE ** (-jnp.arange(0, axis_dim, 2, dtype=F32) / axis_dim)
    ang_r = row[:, None] * inv_freq
    ang_c = col[:, None] * inv_freq
    ang = jnp.concatenate([ang_r, ang_c] * 4, axis=-1)
    sign = jnp.where(jnp.arange(V_DIM) < V_DIM // 2, -1.0, 1.0).astype(F32)
    cos = jnp.cos(ang)
    sin = jnp.sin(ang) * sign
    return jnp.stack([cos * Q_SCALE, cos]), jnp.stack([sin * Q_SCALE, sin])


def _head_layout(w_qk):
    depth, d, w = w_qk.shape
    t = w_qk.reshape(depth, d, w // V_DIM, 2, 2, 2, HEAD_DIM // 4)
    return t.transpose(0, 1, 2, 5, 3, 4, 6).reshape(depth, d, w)


def _fourier_tables(n):
    cn, sn = _dft_cos_sin(n)
    return jnp.concatenate([cn, -sn], axis=1).astype(BF16)


def _pick_tile(n, target):
    t = min(n, target)
    while n % t:
        t //= 2
    return t


def kernel(x, c, ctx, c_ctx, norm_w, w_mod, b_mod, w_in, lambda_qk, subln_w, conv_w,
           w_attn_o, w_conv_o, w_four_o, w_out, final_norm_w):
    b, s, d = x.shape
    lc = ctx.shape[1]
    depth = w_in.shape[0]
    assert d == N_HEADS * V_DIM and w_in.shape[2] == N_PROJ_BLOCKS * d
    assert s % GRID_W == 0 and s % LANES == 0 and lc % SUBLANES == 0

    pad = (-(b + 1)) % MOD_ROWS_PAD
    cond = jnp.concatenate([c, c_ctx[None, :], jnp.zeros((pad, d), F32)], axis=0)
    mod = _modulation(cond, w_mod, b_mod)
    n_rows = cond.shape[0]

    norm_w = norm_w.reshape(depth, 1, d)
    subln_w = subln_w.reshape(depth, 1, V_DIM)
    n_qk = 2 * d
    w_in_b = jnp.concatenate([_head_layout(w_in[:, :, :n_qk]), w_in[:, :, n_qk:]],
                             axis=-1).astype(BF16)
    wa_b, wc_b, wf_b, wo_b = (w.astype(BF16) for w in (w_attn_o, w_conv_o, w_four_o, w_out))

    rope_tabs = _rope_tables(s)
    cc, sc = _dft_cos_sin(FGROUP_DIM)
    cs_chan = jnp.concatenate([cc, sc], axis=1).astype(BF16)
    w_pos_ctx = _fourier_tables(lc)

    ctx_flat = ctx.reshape(1, b * lc, d)
    lat_row = lambda bb: bb
    ctx_row = lambda bb: b

    tm_lat = _pick_tile(s, 2048)
    tm_ctx = _pick_tile(b * lc, 2048)
    tmm_lat = _pick_tile(s // 2, 512)
    tmm_ctx = _pick_tile(lc, 256)
    four_tabs = _fourier_half_tables(s, tmm_lat)
    n_ftiles = s // (2 * tmm_lat)
    yf_lat_spec = pl.BlockSpec(
        (None, None, None, tmm_lat, d),
        lambda bb, i: (bb, jnp.where(i < n_ftiles, i, 2 * n_ftiles - 1 - i),
                       jnp.where(i < n_ftiles, 0, 1), 0, 0))
    yf_ctx_spec = pl.BlockSpec((None, tmm_ctx, d), lambda bb, i: (bb, i, 0))

    for l in range(depth):
        last = l == depth - 1
        lam_init = 0.8 - 0.6 * math.exp(-0.3 * l)
        mod3 = mod[l].reshape(n_rows, 1, 3 * d)

        p_qk = _proj(x, mod3, lat_row, norm_w, w_in_b, l, COL_Q, 2, rope_tabs, tm_lat)
        p_lat = _proj(x, mod3, lat_row, norm_w, w_in_b, l, COL_V, N_PROJ_BLOCKS - COL_V, None, tm_lat)
        c0, cn = (COL_K, 2) if last else (COL_Q, N_PROJ_BLOCKS)
        p_ctx = _proj(ctx_flat, mod3, ctx_row, norm_w, w_in_b, l, c0, cn, None, tm_ctx).reshape(b, lc, -1)

        ya = _attention(lambda_qk, subln_w, l, lam_init, (p_qk, COL_Q), (p_qk, COL_K),
                        (p_lat, 0), (p_lat, COL_ZA - COL_V),
                        (p_ctx, COL_K - c0), (p_ctx, COL_V - c0), carry=True)
        yf = _fourier_half(p_lat, COL_V, cs_chan, four_tabs, tmm_lat)
        x_new = _merge(x, p_lat, COL_V, ya, yf, yf_lat_spec, mod3, lat_row, conv_w, wa_b, wc_b, wf_b,
                       wo_b, l, tmm_lat)

        if not last:
            yac = _attention(lambda_qk, subln_w, l, lam_init, (p_ctx, COL_Q), (p_ctx, COL_K),
                             (p_ctx, COL_V), (p_ctx, COL_ZA), None, None, carry=False)
            yfc = _fourier(p_ctx, COL_Q, cs_chan, w_pos_ctx, lc)
            ctx3 = ctx_flat.reshape(b, lc, d)
            ctx_flat = _merge(ctx3, p_ctx, COL_Q, yac, yfc, yf_ctx_spec, mod3, ctx_row, conv_w, wa_b, wc_b, wf_b,
                              wo_b, l, tmm_ctx).reshape(1, b * lc, d)
        x = x_new

    return _final_norm(x, final_norm_w, _pick_tile(b * s, 1024))
```

```python
import functools
import math

import jax
import jax.numpy as jnp
from jax import lax
from jax.experimental import pallas as pl
from jax.experimental.pallas import tpu as pltpu

F32 = jnp.float32
BF16 = jnp.bfloat16

N_HEADS = 8
HEAD_DIM = 64
V_DIM = 2 * HEAD_DIM
FGROUP_DIM = 128
GRID_W = 64
ROPE_BASE = 10000.0
NORM_EPS = 1e-6
SUBLN_EPS = 1e-5
N_PROJ_BLOCKS = 13
COL_Q, COL_K, COL_V, COL_ZA, COL_XIN, COL_BG, COL_CG, COL_ZC, COL_UF, COL_ZF, COL_GL = range(11)

LANES = 128
SUBLANES = 8
VMEM_LIMIT_BYTES = 56 * 1024 * 1024
MOD_ROWS_PAD = 8
ATTN_SUB_ROWS = 256
Q_SCALE = math.log2(math.e) / math.sqrt(HEAD_DIM)


def _cparams(sem):
    return pltpu.CompilerParams(dimension_semantics=sem, vmem_limit_bytes=VMEM_LIMIT_BYTES)


def _mod_kernel(cond_ref, w_ref, b_ref, o_ref):
    cond = cond_ref[...]
    a = cond * jax.nn.sigmoid(cond)
    o_ref[...] = jnp.dot(a, w_ref[...], preferred_element_type=F32,
                         precision=lax.Precision.HIGHEST) + b_ref[...]


def _modulation(cond, w_mod, b_mod):
    depth, d, w3 = w_mod.shape
    rows = cond.shape[0]
    tn = d
    return pl.pallas_call(
        _mod_kernel,
        grid=(depth, w3 // tn),
        in_specs=[
            pl.BlockSpec((rows, d), lambda l, j: (0, 0)),
            pl.BlockSpec((None, d, tn), lambda l, j: (l, 0, j)),
            pl.BlockSpec((None, 1, tn), lambda l, j: (l, 0, j)),
        ],
        out_specs=pl.BlockSpec((None, rows, tn), lambda l, j: (l, 0, j)),
        out_shape=jax.ShapeDtypeStruct((depth, rows, w3), F32),
        compiler_params=_cparams(("parallel", "parallel")),
        name="modulation",
    )(cond, w_mod, b_mod.reshape(depth, 1, w3))


def _proj_kernel(x_ref, sh_ref, sc_ref, nw_ref, w_ref, *rest, rope, scale_first):
    if rope:
        cos_ref, sin_ref, o_ref, h_ref = rest
    else:
        o_ref, h_ref = rest
    j = pl.program_id(2)

    @pl.when(j == 0)
    def _():
        x = x_ref[0]
        y = x * lax.rsqrt(jnp.mean(x * x, axis=-1, keepdims=True) + NORM_EPS)
        h = (y * nw_ref[...]) * (1.0 + sc_ref[0]) + sh_ref[0]
        h_ref[...] = h.astype(BF16)

    acc = jnp.dot(h_ref[...], w_ref[...], preferred_element_type=F32)
    tn = acc.shape[1]

    if rope:
        cos = cos_ref[...]
        sin = sin_ref[...]
        for hh in range(tn // LANES):
            a = acc[:, hh * LANES:(hh + 1) * LANES]
            r = a * cos + pltpu.roll(a, LANES // 2, axis=1) * sin
            o_ref[0, :, hh * LANES:(hh + 1) * LANES] = r.astype(BF16)
    elif scale_first:
        scale = jnp.where(j == 0, Q_SCALE, 1.0).astype(F32)
        o_ref[0] = (acc * scale).astype(BF16)
    else:
        o_ref[0] = acc.astype(BF16)


def _proj(x, mod3, mod_row, norm_w, w_in_b, layer, col0, ncols, rope_tabs, tm):
    bx, sx, d = x.shape
    tn = d
    rope = rope_tabs is not None
    assert not rope or (col0, ncols) == (COL_Q, 2)
    in_specs = [
        pl.BlockSpec((1, tm, d), lambda b, i, j: (b, i, 0)),
        pl.BlockSpec((1, 1, d), lambda b, i, j: (mod_row(b), 0, 0)),
        pl.BlockSpec((1, 1, d), lambda b, i, j: (mod_row(b), 0, 1)),
        pl.BlockSpec((None, 1, d), lambda b, i, j: (layer, 0, 0)),
        pl.BlockSpec((None, d, tn), lambda b, i, j: (layer, 0, col0 + j)),
    ]
    args = [x, mod3, mod3, norm_w, w_in_b]
    if rope:
        cos_t, sin_t = rope_tabs
        tab_spec = pl.BlockSpec((None, tm, LANES), lambda b, i, j: (j, i, 0))
        in_specs += [tab_spec, tab_spec]
        args += [cos_t, sin_t]
    return pl.pallas_call(
        functools.partial(_proj_kernel, rope=rope, scale_first=(col0 == COL_Q)),
        grid=(bx, sx // tm, ncols),
        in_specs=in_specs,
        out_specs=pl.BlockSpec((1, tm, tn), lambda b, i, j: (b, i, j)),
        out_shape=jax.ShapeDtypeStruct((bx, sx, ncols * tn), BF16),
        scratch_shapes=[pltpu.VMEM((tm, d), BF16)],
        compiler_params=_cparams(("parallel", "parallel", "arbitrary")),
        name="proj_rope" if rope else "proj",
    )(*args)


def _attn_kernel(lq_ref, sw_ref, q_ref, k_ref, v_ref, z_ref, *rest, has_ctx, carry, lam_init):
    rest = list(rest)
    kc_ref, vc_ref = (rest.pop(0), rest.pop(0)) if has_ctx else (None, None)
    if carry:
        qn_ref, kn_ref = rest.pop(0), rest.pop(0)
        kcn_ref = rest.pop(0) if has_ctx else None
        o_ref, kall_ref, vext_ref, kalln_ref, scarry_ref = rest
    else:
        o_ref, kall_ref, vext_ref = rest
    n_lat = k_ref.shape[1]
    n_all = kall_ref.shape[0]

    def gather_keys(dst_ref, lat_ref, ctx_ref):
        dst_ref[0:n_lat, :] = lat_ref[0]
        if has_ctx:
            dst_ref[n_lat:n_all, :] = ctx_ref[0]

    gather_keys(kall_ref, k_ref, kc_ref)
    vext_ref[0:n_lat, 0:V_DIM] = v_ref[0]
    if has_ctx:
        vext_ref[n_lat:n_all, 0:V_DIM] = vc_ref[0]
    ones_lane = lax.broadcasted_iota(jnp.int32, (n_all, V_DIM), 1) == 0
    vext_ref[:, V_DIM:2 * V_DIM] = jnp.where(ones_lane, 1.0, 0.0).astype(BF16)

    lq = lq_ref[...].astype(F32)
    lam = (jnp.exp(jnp.sum(lq[0:1] * lq[1:2], axis=-1, keepdims=True))
           - jnp.exp(jnp.sum(lq[2:3] * lq[3:4], axis=-1, keepdims=True)) + lam_init)

    tq = q_ref.shape[1]
    sub = min(tq, ATTN_SUB_ROWS)
    lane = lax.broadcasted_iota(jnp.int32, (sub, V_DIM), 1)

    def scores(q, keys_ref):
        zero = jnp.zeros_like(q)
        map1 = (lane % HEAD_DIM) < HEAD_DIM // 2
        qq = jnp.concatenate([jnp.where(map1, q, zero), jnp.where(map1, zero, q)], axis=0)
        return lax.dot_general(qq, keys_ref[...], (((1,), (1,)), ((), ())),
                               preferred_element_type=F32)

    def finish(j, get_s):
        m = jnp.max(get_s(), axis=-1, keepdims=True)
        p = jnp.exp2(get_s() - m).astype(BF16)
        nd = jnp.dot(p, vext_ref[...], preferred_element_type=F32)
        o = nd[:, 0:V_DIM] / nd[:, V_DIM:V_DIM + 1]
        o = o[:sub] - lam * o[sub:]
        y = o * lax.rsqrt(jnp.mean(o * o, axis=-1, keepdims=True) + SUBLN_EPS)
        y = (y * sw_ref[...]) * (1.0 - lam_init)
        z = z_ref[0, j * sub:(j + 1) * sub, :].astype(F32)
        o_ref[0, j * sub:(j + 1) * sub, :] = (y * _silu(z)).astype(BF16)

    def value(v):
        return lambda: v

    n_sub = tq // sub
    if carry:
        @pl.when((pl.program_id(0) == 0) & (pl.program_id(1) == 0))
        def _():
            scarry_ref[...] = scores(q_ref[0, 0:sub, :], kall_ref)

        gather_keys(kalln_ref, kn_ref, kcn_ref)
        get_s = lambda: scarry_ref[...]
    else:
        get_s = value(scores(q_ref[0, 0:sub, :], kall_ref))

    for j in range(n_sub):
        if j + 1 < n_sub:
            s_next = scores(q_ref[0, (j + 1) * sub:(j + 2) * sub, :], kall_ref)
        elif carry:
            s_next = scores(qn_ref[0], kalln_ref)
        else:
            s_next = None
        finish(j, get_s)
        get_s = value(s_next)
    if carry:
        scarry_ref[...] = s_next


def _attention(lambda_qk, subln_w, layer, lam_init, q, k, v, z, kc, vc, carry):
    b, s, _ = q[0].shape
    nh = N_HEADS
    has_ctx = kc is not None
    sub = min(s, ATTN_SUB_ROWS)

    def head_cols(rows, col):
        return pl.BlockSpec((1, rows, V_DIM), lambda bb, h: (bb, 0, col * nh + h))

    def next_head_cols(rows, col):
        def index(bb, h):
            flat = jnp.minimum(bb * nh + h + 1, b * nh - 1)
            return (flat // nh, 0, col * nh + flat % nh)
        return pl.BlockSpec((1, rows, V_DIM), index)

    in_specs = [
        pl.BlockSpec((None, 4, HEAD_DIM), lambda bb, h: (layer, 0, 0)),
        pl.BlockSpec((None, 1, V_DIM), lambda bb, h: (layer, 0, 0)),
        head_cols(s, q[1]), head_cols(s, k[1]), head_cols(s, v[1]), head_cols(s, z[1]),
    ]
    args = [lambda_qk, subln_w, q[0], k[0], v[0], z[0]]
    n_keys = s
    if has_ctx:
        lc = kc[0].shape[1]
        n_keys = s + lc
        in_specs += [head_cols(lc, kc[1]), head_cols(lc, vc[1])]
        args += [kc[0], vc[0]]
    scratch = [pltpu.VMEM((n_keys, V_DIM), BF16), pltpu.VMEM((n_keys, 2 * V_DIM), BF16)]
    if carry:
        in_specs += [next_head_cols(sub, q[1]), next_head_cols(s, k[1])]
        args += [q[0], k[0]]
        if has_ctx:
            in_specs.append(next_head_cols(lc, kc[1]))
            args.append(kc[0])
        scratch += [pltpu.VMEM((n_keys, V_DIM), BF16), pltpu.VMEM((2 * sub, n_keys), F32)]
    return pl.pallas_call(
        functools.partial(_attn_kernel, has_ctx=has_ctx, carry=carry, lam_init=lam_init),
        grid=(b, nh),
        in_specs=in_specs,
        out_specs=head_cols(s, 0),
        out_shape=jax.ShapeDtypeStruct((b, s, nh * V_DIM), BF16),
        scratch_shapes=scratch,
        compiler_params=_cparams(("arbitrary", "arbitrary")),
        name="attn_ctx" if has_ctx else "attn",
    )(*args)


def _fourier_kernel(u_ref, cs_ref, w_ref, z_ref, o_ref, ab_ref):
    r = pl.program_id(1)
    n = u_ref.shape[1]

    @pl.when(r == 0)
    def _():
        for g in range(u_ref.shape[2] // FGROUP_DIM):
            cols = slice(g * FGROUP_DIM, (g + 1) * FGROUP_DIM)
            ab = jnp.dot(u_ref[0, :, cols], cs_ref[...], preferred_element_type=F32)
            ab_ref[0:n, cols] = ab[:, :FGROUP_DIM].astype(BF16)
            ab_ref[n:2 * n, cols] = ab[:, FGROUP_DIM:].astype(BF16)

    y = jnp.dot(w_ref[...], ab_ref[...], preferred_element_type=F32)
    o_ref[0] = (y * _silu(z_ref[0].astype(F32))).astype(BF16)


def _dft_cos_sin(n):
    k = jnp.arange(n, dtype=jnp.int32)
    ang = ((k[:, None] * k[None, :]) % n).astype(F32) * (2.0 * math.pi / n)
    scale = 1.0 / math.sqrt(n)
    return jnp.cos(ang) * scale, jnp.sin(ang) * scale


def _fourier(p, col0, cs_chan, w_pos, tr):
    b, n, _ = p.shape
    d = N_HEADS * V_DIM
    return pl.pallas_call(
        _fourier_kernel,
        grid=(b, n // tr),
        in_specs=[
            pl.BlockSpec((1, n, d), lambda bb, r: (bb, 0, COL_UF - col0)),
            pl.BlockSpec((FGROUP_DIM, 2 * FGROUP_DIM), lambda bb, r: (0, 0)),
            pl.BlockSpec((tr, 2 * n), lambda bb, r: (r, 0)),
            pl.BlockSpec((1, tr, d), lambda bb, r: (bb, r, COL_ZF - col0)),
        ],
        out_specs=pl.BlockSpec((1, tr, d), lambda bb, r: (bb, r, 0)),
        out_shape=jax.ShapeDtypeStruct((b, n, d), BF16),
        scratch_shapes=[pltpu.VMEM((2 * n, d), BF16)],
        compiler_params=_cparams(("parallel", "arbitrary")),
        name="fourier",
    )(p, cs_chan, w_pos, p)


def _fourier_half_kernel(u_ref, cs_ref, c_ref, s_ref, rev_ref, zlo_ref, zhi_ref, o_ref, a_ref, b_ref):
    r = pl.program_id(1)
    tr = o_ref.shape[3]

    @pl.when(r == 0)
    def _():
        for g in range(u_ref.shape[2] // FGROUP_DIM):
            cols = slice(g * FGROUP_DIM, (g + 1) * FGROUP_DIM)
            ab = jnp.dot(u_ref[0, :, cols], cs_ref[...], preferred_element_type=F32)
            a_ref[:, cols] = ab[:, :FGROUP_DIM].astype(BF16)
            b_ref[:, cols] = ab[:, FGROUP_DIM:].astype(BF16)

    p = jnp.dot(c_ref[...], a_ref[...], preferred_element_type=F32)
    q = jnp.dot(s_ref[...], b_ref[...], preferred_element_type=F32)
    o_ref[0, 0, 0] = ((p[:tr] - q[:tr]) * _silu(zlo_ref[0].astype(F32))).astype(BF16)
    hi = jnp.dot(rev_ref[...], (p + q).astype(BF16), preferred_element_type=F32)
    o_ref[0, 0, 1] = (hi * _silu(zhi_ref[0].astype(F32))).astype(BF16)


def _fourier_half_tables(n, tr):
    ext = 2 * SUBLANES
    n_tiles = n // (2 * tr)
    rows = (jnp.arange(n_tiles, dtype=jnp.int32)[:, None] * tr
            + jnp.arange(tr + ext, dtype=jnp.int32)[None, :])
    k = jnp.arange(n, dtype=jnp.int32)
    ang = ((rows[:, :, None] * k[None, None, :]) % n).astype(F32) * (2.0 * math.pi / n)
    scale = 1.0 / math.sqrt(n)
    i = jnp.arange(tr, dtype=jnp.int32)[:, None]
    j = jnp.arange(tr + ext, dtype=jnp.int32)[None, :]
    rev = (j == tr - i).astype(BF16)
    return (jnp.cos(ang) * scale).astype(BF16), (jnp.sin(ang) * scale).astype(BF16), rev


def _fourier_half(p, col0, cs_chan, tabs, tr):
    b, n, _ = p.shape
    d = N_HEADS * V_DIM
    c_t, s_t, rev = tabs
    n_tiles, rows_ext, _ = c_t.shape
    return pl.pallas_call(
        _fourier_half_kernel,
        grid=(b, n_tiles),
        in_specs=[
            pl.BlockSpec((1, n, d), lambda bb, r: (bb, 0, COL_UF - col0)),
            pl.BlockSpec((FGROUP_DIM, 2 * FGROUP_DIM), lambda bb, r: (0, 0)),
            pl.BlockSpec((None, rows_ext, n), lambda bb, r: (r, 0, 0)),
            pl.BlockSpec((None, rows_ext, n), lambda bb, r: (r, 0, 0)),
            pl.BlockSpec((tr, rows_ext), lambda bb, r: (0, 0)),
            pl.BlockSpec((1, tr, d), lambda bb, r: (bb, r, COL_ZF - col0)),
            pl.BlockSpec((1, tr, d), lambda bb, r: (bb, 2 * n_tiles - 1 - r, COL_ZF - col0)),
        ],
        out_specs=pl.BlockSpec((1, 1, 2, tr, d), lambda bb, r: (bb, r, 0, 0, 0)),
        out_shape=jax.ShapeDtypeStruct((b, n_tiles, 2, tr, d), BF16),
        scratch_shapes=[pltpu.VMEM((n, d), BF16), pltpu.VMEM((n, d), BF16)],
        compiler_params=_cparams(("parallel", "arbitrary")),
        name="fourier_half",
    )(p, cs_chan, c_t, s_t, rev, p, p)


def _sigmoid(z):
    return 0.5 * jnp.tanh(0.5 * z) + 0.5


def _silu(z):
    h = 0.5 * z
    return h * jnp.tanh(h) + h


def _merge_kernel(ya_ref, yf_ref, xin_ref, bg_ref, cg_ref, zc_ref,
                  ga_ref, gc_ref, gf_ref, xin_p_ref, cg_p_ref, xin_n_ref, cg_n_ref,
                  x_ref, gate_ref, cw_ref, wa_ref, wc_ref, wf_ref, wo_ref, o_ref):
    i = pl.program_id(1)
    tm = x_ref.shape[1]
    pad = SUBLANES

    def branch(y_gated, w_ref, g_ref):
        t = jnp.dot(y_gated, w_ref[...], preferred_element_type=F32)
        return _sigmoid(g_ref[0].astype(F32)) * t

    t_a = jnp.dot(ya_ref[0], wa_ref[...], preferred_element_type=F32)
    merged_af = (_sigmoid(ga_ref[0].astype(F32)) * t_a
                 + branch(yf_ref[...], wf_ref, gf_ref))
    bits = pltpu.bitcast(t_a[0:SUBLANES, 0:LANES], jnp.uint32)
    zero = pltpu.bitcast((bits >> 16) >> 16, F32)[0:1, 0:1]

    u = cg_ref[0].astype(F32) * xin_ref[0].astype(F32)
    up = cg_p_ref[0, pad - 1:pad, :].astype(F32) * xin_p_ref[0, pad - 1:pad, :].astype(F32)
    un = cg_n_ref[0, 0:1, :].astype(F32) * xin_n_ref[0, 0:1, :].astype(F32)
    up = jnp.where(i == 0, 0.0, up)
    un = jnp.where(i == pl.num_programs(1) - 1, 0.0, un)
    row = lax.broadcasted_iota(jnp.int32, (tm, 1), 0)
    u_prev = jnp.where(row == 0, up, pltpu.roll(u, 1, axis=0))
    u_next = jnp.where(row == tm - 1, un, pltpu.roll(u, tm - 1, axis=0))
    cw = cw_ref[...] + zero
    conv = cw[0:1] * u_prev + cw[1:2] * u + cw[2:3] * u_next
    y_c = bg_ref[0].astype(F32) * conv
    yc_gated = (y_c * _silu(zc_ref[0].astype(F32))).astype(BF16)
    merged = merged_af + branch(yc_gated, wc_ref, gc_ref)
    out = jnp.dot(merged.astype(BF16), wo_ref[...], preferred_element_type=F32)
    o_ref[0] = x_ref[0] + gate_ref[0] * out


def _merge(x, p, col0, ya, yf, yf_spec, mod3, mod_row, conv_w, wa, wc, wf, wo, layer, tm):
    bx, sx, d = x.shape
    nb = tm // SUBLANES
    last_halo = sx // SUBLANES - 1

    def whole(b, i):
        return (b, i, 0)

    def col(cb):
        return pl.BlockSpec((1, tm, d), lambda b, i: (b, i, cb - col0))

    def halo_prev(cb):
        return pl.BlockSpec((1, SUBLANES, d),
                            lambda b, i: (b, jnp.maximum(i * nb - 1, 0), cb - col0))

    def halo_next(cb):
        return pl.BlockSpec((1, SUBLANES, d),
                            lambda b, i: (b, jnp.minimum((i + 1) * nb, last_halo), cb - col0))

    def weight():
        return pl.BlockSpec((None, d, d), lambda b, i: (layer, 0, 0), pipeline_mode=pl.Buffered(1))

    in_specs = [
        pl.BlockSpec((1, tm, d), whole), yf_spec,
        col(COL_XIN), col(COL_BG), col(COL_CG), col(COL_ZC),
        col(COL_GL), col(COL_GL + 1), col(COL_GL + 2),
        halo_prev(COL_XIN), halo_prev(COL_CG), halo_next(COL_XIN), halo_next(COL_CG),
        pl.BlockSpec((1, tm, d), whole),
        pl.BlockSpec((1, 1, d), lambda b, i: (mod_row(b), 0, 2)),
        pl.BlockSpec((None, 3, d), lambda b, i: (layer, 0, 0)),
        weight(), weight(), weight(), weight(),
    ]
    return pl.pallas_call(
        _merge_kernel,
        grid=(bx, sx // tm),
        in_specs=in_specs,
        out_specs=pl.BlockSpec((1, tm, d), lambda b, i: (b, i, 0)),
        out_shape=jax.ShapeDtypeStruct((bx, sx, d), F32),
        compiler_params=_cparams(("parallel", "parallel")),
        name="merge",
    )(ya, yf, p, p, p, p, p, p, p, p, p, p, p, x, mod3, conv_w, wa, wc, wf, wo)


def _final_norm_kernel(x_ref, w_ref, o_ref):
    x = x_ref[...]
    y = x * lax.rsqrt(jnp.mean(x * x, axis=-1, keepdims=True) + NORM_EPS)
    o_ref[...] = y * w_ref[...]


def _final_norm(x, w, tm):
    b, s, d = x.shape
    x2 = x.reshape(b * s, d)
    out = pl.pallas_call(
        _final_norm_kernel,
        grid=(b * s // tm,),
        in_specs=[pl.BlockSpec((tm, d), lambda i: (i, 0)), pl.BlockSpec((1, d), lambda i: (0, 0))],
        out_specs=pl.BlockSpec((tm, d), lambda i: (i, 0)),
        out_shape=jax.ShapeDtypeStruct((b * s, d), F32),
        compiler_params=_cparams(("parallel",)),
        name="final_norm",
    )(x2, w.reshape(1, d))
    return out.reshape(b, s, d)


def _rope_tables(n_tokens):
    axis_dim = HEAD_DIM // 2
    rows = n_tokens // GRID_W
    row = jnp.repeat(jnp.arange(rows), GRID_W).astype(F32)
    col = jnp.tile(jnp.arange(GRID_W), rows).astype(F32)
    inv_freq = ROPE_BASE ** (-jnp.arange(0, axis_dim, 2, dtype=F32) / axis_dim)
    ang_r = row[:, None] * inv_freq
    ang_c = col[:, None] * inv_freq
    ang = jnp.concatenate([ang_r, ang_c] * 4, axis=-1)
    sign = jnp.where(jnp.arange(V_DIM) < V_DIM // 2, -1.0, 1.0).astype(F32)
    cos = jnp.cos(ang)
    sin = jnp.sin(ang) * sign
    return jnp.stack([cos * Q_SCALE, cos]), jnp.stack([sin * Q_SCALE, sin])


def _head_layout(w_qk):
    depth, d, w = w_qk.shape
    t = w_qk.reshape(depth, d, w // V_DIM, 2, 2, 2, HEAD_DIM // 4)
    return t.transpose(0, 1, 2, 5, 3, 4, 6).reshape(depth, d, w)


def _fourier_tables(n):
    cn, sn = _dft_cos_sin(n)
    return jnp.concatenate([cn, -sn], axis=1).astype(BF16)


def _pick_tile(n, target):
    t = min(n, target)
    while n % t:
        t //= 2
    return t


def kernel(x, c, ctx, c_ctx, norm_w, w_mod, b_mod, w_in, lambda_qk, subln_w, conv_w,
           w_attn_o, w_conv_o, w_four_o, w_out, final_norm_w):
    b, s, d = x.shape
    lc = ctx.shape[1]
    depth = w_in.shape[0]
    assert d == N_HEADS * V_DIM and w_in.shape[2] == N_PROJ_BLOCKS * d
    assert s % GRID_W == 0 and s % LANES == 0 and lc % SUBLANES == 0

    pad = (-(b + 1)) % MOD_ROWS_PAD
    cond = jnp.concatenate([c, c_ctx[None, :], jnp.zeros((pad, d), F32)], axis=0)
    mod = _modulation(cond, w_mod, b_mod)
    n_rows = cond.shape[0]

    norm_w = norm_w.reshape(depth, 1, d)
    subln_w = subln_w.reshape(depth, 1, V_DIM)
    n_qk = 2 * d
    w_in_b = w_in.astype(BF16)
    w_in_b = w_in_b.at[:, :, :n_qk].set(_head_layout(w_in_b[:, :, :n_qk]))
    wa_b, wc_b, wf_b, wo_b = (w.astype(BF16) for w in (w_attn_o, w_conv_o, w_four_o, w_out))

    rope_tabs = _rope_tables(s)
    cc, sc = _dft_cos_sin(FGROUP_DIM)
    cs_chan = jnp.concatenate([cc, sc], axis=1).astype(BF16)
    w_pos_ctx = _fourier_tables(lc)

    ctx_flat = ctx.reshape(1, b * lc, d)
    lat_row = lambda bb: bb
    ctx_row = lambda bb: b

    tm_lat = _pick_tile(s, 2048)
    tm_ctx = _pick_tile(b * lc, 2048)
    tmm_lat = _pick_tile(s // 2, 512)
    tmm_ctx = _pick_tile(lc, 256)
    four_tabs = _fourier_half_tables(s, tmm_lat)
    n_ftiles = s // (2 * tmm_lat)
    yf_lat_spec = pl.BlockSpec(
        (None, None, None, tmm_lat, d),
        lambda bb, i: (bb, jnp.where(i < n_ftiles, i, 2 * n_ftiles - 1 - i),
                       jnp.where(i < n_ftiles, 0, 1), 0, 0))
    yf_ctx_spec = pl.BlockSpec((None, tmm_ctx, d), lambda bb, i: (bb, i, 0))

    for l in range(depth):
        last = l == depth - 1
        lam_init = 0.8 - 0.6 * math.exp(-0.3 * l)
        mod3 = mod[l].reshape(n_rows, 1, 3 * d)

        p_qk = _proj(x, mod3, lat_row, norm_w, w_in_b, l, COL_Q, 2, rope_tabs, tm_lat)
        p_lat = _proj(x, mod3, lat_row, norm_w, w_in_b, l, COL_V, N_PROJ_BLOCKS - COL_V, None, tm_lat)
        c0, cn = (COL_K, 2) if last else (COL_Q, N_PROJ_BLOCKS)
        p_ctx = _proj(ctx_flat, mod3, ctx_row, norm_w, w_in_b, l, c0, cn, None, tm_ctx).reshape(b, lc, -1)

        ya = _attention(lambda_qk, subln_w, l, lam_init, (p_qk, COL_Q), (p_qk, COL_K),
                        (p_lat, 0), (p_lat, COL_ZA - COL_V),
                        (p_ctx, COL_K - c0), (p_ctx, COL_V - c0), carry=True)
        yf = _fourier_half(p_lat, COL_V, cs_chan, four_tabs, tmm_lat)
        x_new = _merge(x, p_lat, COL_V, ya, yf, yf_lat_spec, mod3, lat_row, conv_w, wa_b, wc_b, wf_b,
                       wo_b, l, tmm_lat)

        if not last:
            yac = _attention(lambda_qk, subln_w, l, lam_init, (p_ctx, COL_Q), (p_ctx, COL_K),
                             (p_ctx, COL_V), (p_ctx, COL_ZA), None, None, carry=False)
            yfc = _fourier(p_ctx, COL_Q, cs_chan, w_pos_ctx, lc)
            ctx3 = ctx_flat.reshape(b, lc, d)
            ctx_flat = _merge(ctx3, p_ctx, COL_Q, yac, yfc, yf_ctx_spec, mod3, ctx_row, conv_w, wa_b, wc_b, wf_b,
                              wo_b, l, tmm_ctx).reshape(1, b * lc, d)
        x = x_new

    return _final_norm(x, final_norm_w, _pick_tile(b * s, 1024))
```

```python
import functools
import math

import jax
import jax.numpy as jnp
from jax import lax
from jax.experimental import pallas as pl
from jax.experimental.pallas import tpu as pltpu

F32 = jnp.float32
BF16 = jnp.bfloat16

N_HEADS = 8
HEAD_DIM = 64
V_DIM = 2 * HEAD_DIM
FGROUP_DIM = 128
GRID_W = 64
ROPE_BASE = 10000.0
NORM_EPS = 1e-6
SUBLN_EPS = 1e-5
N_PROJ_BLOCKS = 13
COL_Q, COL_K, COL_V, COL_ZA, COL_XIN, COL_BG, COL_CG, COL_ZC, COL_UF, COL_ZF, COL_GL = range(11)

LANES = 128
SUBLANES = 8
VMEM_LIMIT_BYTES = 56 * 1024 * 1024
MOD_ROWS_PAD = 8
ATTN_SUB_ROWS = 256
ATTN_CARRY_BLOCKS = 2
Q_SCALE = math.log2(math.e) / math.sqrt(HEAD_DIM)


def _cparams(sem):
    return pltpu.CompilerParams(dimension_semantics=sem, vmem_limit_bytes=VMEM_LIMIT_BYTES)


def _mod_kernel(cond_ref, w_ref, b_ref, o_ref):
    cond = cond_ref[...]
    a = cond * jax.nn.sigmoid(cond)
    o_ref[...] = jnp.dot(a, w_ref[...], preferred_element_type=F32,
                         precision=lax.Precision.HIGHEST) + b_ref[...]


def _modulation(cond, w_mod, b_mod):
    depth, d, w3 = w_mod.shape
    rows = cond.shape[0]
    tn = d
    return pl.pallas_call(
        _mod_kernel,
        grid=(depth, w3 // tn),
        in_specs=[
            pl.BlockSpec((rows, d), lambda l, j: (0, 0)),
            pl.BlockSpec((None, d, tn), lambda l, j: (l, 0, j)),
            pl.BlockSpec((None, 1, tn), lambda l, j: (l, 0, j)),
        ],
        out_specs=pl.BlockSpec((None, rows, tn), lambda l, j: (l, 0, j)),
        out_shape=jax.ShapeDtypeStruct((depth, rows, w3), F32),
        compiler_params=_cparams(("parallel", "parallel")),
        name="modulation",
    )(cond, w_mod, b_mod.reshape(depth, 1, w3))


def _proj_kernel(x_ref, sh_ref, sc_ref, nw_ref, w_ref, *rest, rope, scale_first):
    if rope:
        cos_ref, sin_ref, o_ref, h_ref = rest
    else:
        o_ref, h_ref = rest
    j = pl.program_id(2)

    @pl.when(j == 0)
    def _():
        x = x_ref[0]
        y = x * lax.rsqrt(jnp.mean(x * x, axis=-1, keepdims=True) + NORM_EPS)
        h = (y * nw_ref[...]) * (1.0 + sc_ref[0]) + sh_ref[0]
        h_ref[...] = h.astype(BF16)

    acc = jnp.dot(h_ref[...], w_ref[...], preferred_element_type=F32)
    tn = acc.shape[1]

    if rope:
        cos = cos_ref[...]
        sin = sin_ref[...]
        for hh in range(tn // LANES):
            a = acc[:, hh * LANES:(hh + 1) * LANES]
            r = a * cos + pltpu.roll(a, LANES // 2, axis=1) * sin
            o_ref[0, :, hh * LANES:(hh + 1) * LANES] = r.astype(BF16)
    elif scale_first:
        scale = jnp.where(j == 0, Q_SCALE, 1.0).astype(F32)
        o_ref[0] = (acc * scale).astype(BF16)
    else:
        o_ref[0] = acc.astype(BF16)


def _proj(x, mod3, mod_row, norm_w, w_in_b, layer, col0, ncols, rope_tabs, tm):
    bx, sx, d = x.shape
    tn = d
    rope = rope_tabs is not None
    assert not rope or (col0, ncols) == (COL_Q, 2)
    in_specs = [
        pl.BlockSpec((1, tm, d), lambda b, i, j: (b, i, 0)),
        pl.BlockSpec((1, 1, d), lambda b, i, j: (mod_row(b), 0, 0)),
        pl.BlockSpec((1, 1, d), lambda b, i, j: (mod_row(b), 0, 1)),
        pl.BlockSpec((None, 1, d), lambda b, i, j: (layer, 0, 0)),
        pl.BlockSpec((None, d, tn), lambda b, i, j: (layer, 0, col0 + j)),
    ]
    args = [x, mod3, mod3, norm_w, w_in_b]
    if rope:
        cos_t, sin_t = rope_tabs
        tab_spec = pl.BlockSpec((None, tm, LANES), lambda b, i, j: (j, i, 0))
        in_specs += [tab_spec, tab_spec]
        args += [cos_t, sin_t]
    return pl.pallas_call(
        functools.partial(_proj_kernel, rope=rope, scale_first=(col0 == COL_Q)),
        grid=(bx, sx // tm, ncols),
        in_specs=in_specs,
        out_specs=pl.BlockSpec((1, tm, tn), lambda b, i, j: (b, i, j)),
        out_shape=jax.ShapeDtypeStruct((bx, sx, ncols * tn), BF16),
        scratch_shapes=[pltpu.VMEM((tm, d), BF16)],
        compiler_params=_cparams(("parallel", "parallel", "arbitrary")),
        name="proj_rope" if rope else "proj",
    )(*args)


def _attn_kernel(lq_ref, sw_ref, q_ref, k_ref, v_ref, z_ref, *rest, has_ctx, carry, lam_init):
    rest = list(rest)
    kc_ref, vc_ref = (rest.pop(0), rest.pop(0)) if has_ctx else (None, None)
    if carry:
        qn_ref, kn_ref = rest.pop(0), rest.pop(0)
        kcn_ref = rest.pop(0) if has_ctx else None
        o_ref, kall_ref, vext_ref, kalln_ref, scarry_ref = rest
    else:
        o_ref, kall_ref, vext_ref = rest
    n_lat = k_ref.shape[1]
    n_all = kall_ref.shape[0]

    def gather_keys(dst_ref, lat_ref, ctx_ref):
        dst_ref[0:n_lat, :] = lat_ref[0]
        if has_ctx:
            dst_ref[n_lat:n_all, :] = ctx_ref[0]

    gather_keys(kall_ref, k_ref, kc_ref)
    vext_ref[0:n_lat, 0:V_DIM] = v_ref[0]
    if has_ctx:
        vext_ref[n_lat:n_all, 0:V_DIM] = vc_ref[0]
    ones_lane = lax.broadcasted_iota(jnp.int32, (n_all, V_DIM), 1) == 0
    vext_ref[:, V_DIM:2 * V_DIM] = jnp.where(ones_lane, 1.0, 0.0).astype(BF16)

    lq = lq_ref[...].astype(F32)
    lam = (jnp.exp(jnp.sum(lq[0:1] * lq[1:2], axis=-1, keepdims=True))
           - jnp.exp(jnp.sum(lq[2:3] * lq[3:4], axis=-1, keepdims=True)) + lam_init)

    tq = q_ref.shape[1]
    sub = min(tq, ATTN_SUB_ROWS)
    lane = lax.broadcasted_iota(jnp.int32, (sub, V_DIM), 1)

    def scores(q, keys_ref):
        zero = jnp.zeros_like(q)
        map1 = (lane % HEAD_DIM) < HEAD_DIM // 2
        qq = jnp.concatenate([jnp.where(map1, q, zero), jnp.where(map1, zero, q)], axis=0)
        return lax.dot_general(qq, keys_ref[...], (((1,), (1,)), ((), ())),
                               preferred_element_type=F32)

    def finish(j, get_s):
        m = jnp.max(get_s(), axis=-1, keepdims=True)
        p = jnp.exp2(get_s() - m).astype(BF16)
        nd = jnp.dot(p, vext_ref[...], preferred_element_type=F32)
        o = nd[:, 0:V_DIM] / nd[:, V_DIM:V_DIM + 1]
        o = o[:sub] - lam * o[sub:]
        y = o * lax.rsqrt(jnp.mean(o * o, axis=-1, keepdims=True) + SUBLN_EPS)
        y = (y * sw_ref[...]) * (1.0 - lam_init)
        z = z_ref[0, j * sub:(j + 1) * sub, :].astype(F32)
        o_ref[0, j * sub:(j + 1) * sub, :] = (y * _silu(z)).astype(BF16)

    def value(v):
        return lambda: v

    def rows(ref, j):
        return ref[0, j * sub:(j + 1) * sub, :]

    n_sub = tq // sub
    if carry:
        depth = scarry_ref.shape[0]

        @pl.when((pl.program_id(0) == 0) & (pl.program_id(1) == 0))
        def _():
            for d in range(depth):
                scarry_ref[d] = scores(rows(q_ref, d), kall_ref)

        gather_keys(kalln_ref, kn_ref, kcn_ref)
        pending = [(lambda d=d: scarry_ref[d]) for d in range(depth)]
    else:
        depth = 1
        pending = [value(scores(rows(q_ref, 0), kall_ref))]

    for j in range(n_sub):
        if j + depth < n_sub:
            pending.append(value(scores(rows(q_ref, j + depth), kall_ref)))
        elif carry:
            pending.append(value(scores(rows(qn_ref, j + depth - n_sub), kalln_ref)))
        finish(j, pending.pop(0))
    if carry:
        for d in range(depth):
            scarry_ref[d] = pending[d]()


def _attention(lambda_qk, subln_w, layer, lam_init, q, k, v, z, kc, vc, carry):
    b, s, _ = q[0].shape
    nh = N_HEADS
    has_ctx = kc is not None
    sub = min(s, ATTN_SUB_ROWS)

    def head_cols(rows, col):
        return pl.BlockSpec((1, rows, V_DIM), lambda bb, h: (bb, 0, col * nh + h))

    def next_head_cols(rows, col):
        def index(bb, h):
            flat = jnp.minimum(bb * nh + h + 1, b * nh - 1)
            return (flat // nh, 0, col * nh + flat % nh)
        return pl.BlockSpec((1, rows, V_DIM), index)

    in_specs = [
        pl.BlockSpec((None, 4, HEAD_DIM), lambda bb, h: (layer, 0, 0)),
        pl.BlockSpec((None, 1, V_DIM), lambda bb, h: (layer, 0, 0)),
        head_cols(s, q[1]), head_cols(s, k[1]), head_cols(s, v[1]), head_cols(s, z[1]),
    ]
    args = [lambda_qk, subln_w, q[0], k[0], v[0], z[0]]
    n_keys = s
    if has_ctx:
        lc = kc[0].shape[1]
        n_keys = s + lc
        in_specs += [head_cols(lc, kc[1]), head_cols(lc, vc[1])]
        args += [kc[0], vc[0]]
    scratch = [pltpu.VMEM((n_keys, V_DIM), BF16), pltpu.VMEM((n_keys, 2 * V_DIM), BF16)]
    if carry:
        depth = min(ATTN_CARRY_BLOCKS, s // sub)
        in_specs += [next_head_cols(depth * sub, q[1]), next_head_cols(s, k[1])]
        args += [q[0], k[0]]
        if has_ctx:
            in_specs.append(next_head_cols(lc, kc[1]))
            args.append(kc[0])
        scratch += [pltpu.VMEM((n_keys, V_DIM), BF16), pltpu.VMEM((depth, 2 * sub, n_keys), F32)]
    return pl.pallas_call(
        functools.partial(_attn_kernel, has_ctx=has_ctx, carry=carry, lam_init=lam_init),
        grid=(b, nh),
        in_specs=in_specs,
        out_specs=head_cols(s, 0),
        out_shape=jax.ShapeDtypeStruct((b, s, nh * V_DIM), BF16),
        scratch_shapes=scratch,
        compiler_params=_cparams(("arbitrary", "arbitrary")),
        name="attn_ctx" if has_ctx else "attn",
    )(*args)


def _fourier_kernel(u_ref, cs_ref, w_ref, z_ref, o_ref, ab_ref):
    r = pl.program_id(1)
    n = u_ref.shape[1]

    @pl.when(r == 0)
    def _():
        for g in range(u_ref.shape[2] // FGROUP_DIM):
            cols = slice(g * FGROUP_DIM, (g + 1) * FGROUP_DIM)
            ab = jnp.dot(u_ref[0, :, cols], cs_ref[...], preferred_element_type=F32)
            ab_ref[0:n, cols] = ab[:, :FGROUP_DIM].astype(BF16)
            ab_ref[n:2 * n, cols] = ab[:, FGROUP_DIM:].astype(BF16)

    y = jnp.dot(w_ref[...], ab_ref[...], preferred_element_type=F32)
    o_ref[0] = (y * _silu(z_ref[0].astype(F32))).astype(BF16)


def _dft_cos_sin(n):
    k = jnp.arange(n, dtype=jnp.int32)
    ang = ((k[:, None] * k[None, :]) % n).astype(F32) * (2.0 * math.pi / n)
    scale = 1.0 / math.sqrt(n)
    return jnp.cos(ang) * scale, jnp.sin(ang) * scale


def _fourier(p, col0, cs_chan, w_pos, tr):
    b, n, _ = p.shape
    d = N_HEADS * V_DIM
    return pl.pallas_call(
        _fourier_kernel,
        grid=(b, n // tr),
        in_specs=[
            pl.BlockSpec((1, n, d), lambda bb, r: (bb, 0, COL_UF - col0)),
            pl.BlockSpec((FGROUP_DIM, 2 * FGROUP_DIM), lambda bb, r: (0, 0)),
            pl.BlockSpec((tr, 2 * n), lambda bb, r: (r, 0)),
            pl.BlockSpec((1, tr, d), lambda bb, r: (bb, r, COL_ZF - col0)),
        ],
        out_specs=pl.BlockSpec((1, tr, d), lambda bb, r: (bb, r, 0)),
        out_shape=jax.ShapeDtypeStruct((b, n, d), BF16),
        scratch_shapes=[pltpu.VMEM((2 * n, d), BF16)],
        compiler_params=_cparams(("parallel", "arbitrary")),
        name="fourier",
    )(p, cs_chan, w_pos, p)


def _fourier_half_kernel(u_ref, cs_ref, c_ref, s_ref, rev_ref, zlo_ref, zhi_ref, o_ref, a_ref, b_ref):
    r = pl.program_id(1)
    tr = o_ref.shape[3]

    @pl.when(r == 0)
    def _():
        for g in range(u_ref.shape[2] // FGROUP_DIM):
            cols = slice(g * FGROUP_DIM, (g + 1) * FGROUP_DIM)
            ab = jnp.dot(u_ref[0, :, cols], cs_ref[...], preferred_element_type=F32)
            a_ref[:, cols] = ab[:, :FGROUP_DIM].astype(BF16)
            b_ref[:, cols] = ab[:, FGROUP_DIM:].astype(BF16)

    p = jnp.dot(c_ref[...], a_ref[...], preferred_element_type=F32)
    q = jnp.dot(s_ref[...], b_ref[...], preferred_element_type=F32)
    o_ref[0, 0, 0] = ((p[:tr] - q[:tr]) * _silu(zlo_ref[0].astype(F32))).astype(BF16)
    hi = jnp.dot(rev_ref[...], (p + q).astype(BF16), preferred_element_type=F32)
    o_ref[0, 0, 1] = (hi * _silu(zhi_ref[0].astype(F32))).astype(BF16)


def _fourier_half_tables(n, tr):
    ext = 2 * SUBLANES
    n_tiles = n // (2 * tr)
    rows = (jnp.arange(n_tiles, dtype=jnp.int32)[:, None] * tr
            + jnp.arange(tr + ext, dtype=jnp.int32)[None, :])
    k = jnp.arange(n, dtype=jnp.int32)
    ang = ((rows[:, :, None] * k[None, None, :]) % n).astype(F32) * (2.0 * math.pi / n)
    scale = 1.0 / math.sqrt(n)
    i = jnp.arange(tr, dtype=jnp.int32)[:, None]
    j = jnp.arange(tr + ext, dtype=jnp.int32)[None, :]
    rev = (j == tr - i).astype(BF16)
    return (jnp.cos(ang) * scale).astype(BF16), (jnp.sin(ang) * scale).astype(BF16), rev


def _fourier_half(p, col0, cs_chan, tabs, tr):
    b, n, _ = p.shape
    d = N_HEADS * V_DIM
    c_t, s_t, rev = tabs
    n_tiles, rows_ext, _ = c_t.shape
    return pl.pallas_call(
        _fourier_half_kernel,
        grid=(b, n_tiles),
        in_specs=[
            pl.BlockSpec((1, n, d), lambda bb, r: (bb, 0, COL_UF - col0)),
            pl.BlockSpec((FGROUP_DIM, 2 * FGROUP_DIM), lambda bb, r: (0, 0)),
            pl.BlockSpec((None, rows_ext, n), lambda bb, r: (r, 0, 0)),
            pl.BlockSpec((None, rows_ext, n), lambda bb, r: (r, 0, 0)),
            pl.BlockSpec((tr, rows_ext), lambda bb, r: (0, 0)),
            pl.BlockSpec((1, tr, d), lambda bb, r: (bb, r, COL_ZF - col0)),
            pl.BlockSpec((1, tr, d), lambda bb, r: (bb, 2 * n_tiles - 1 - r, COL_ZF - col0)),
        ],
        out_specs=pl.BlockSpec((1, 1, 2, tr, d), lambda bb, r: (bb, r, 0, 0, 0)),
        out_shape=jax.ShapeDtypeStruct((b, n_tiles, 2, tr, d), BF16),
        scratch_shapes=[pltpu.VMEM((n, d), BF16), pltpu.VMEM((n, d), BF16)],
        compiler_params=_cparams(("parallel", "arbitrary")),
        name="fourier_half",
    )(p, cs_chan, c_t, s_t, rev, p, p)


def _sigmoid(z):
    return 0.5 * jnp.tanh(0.5 * z) + 0.5


def _silu(z):
    h = 0.5 * z
    return h * jnp.tanh(h) + h


def _merge_kernel(ya_ref, yf_ref, xin_ref, bg_ref, cg_ref, zc_ref,
                  ga_ref, gc_ref, gf_ref, xin_p_ref, cg_p_ref, xin_n_ref, cg_n_ref,
                  x_ref, gate_ref, cw_ref, wa_ref, wc_ref, wf_ref, wo_ref, *rest):
    fnw_ref, o_ref = rest if len(rest) == 2 else (None, rest[0])
    i = pl.program_id(1)
    tm = x_ref.shape[1]
    pad = SUBLANES

    def branch(y_gated, w_ref, g_ref):
        t = jnp.dot(y_gated, w_ref[...], preferred_element_type=F32)
        return _sigmoid(g_ref[0].astype(F32)) * t

    t_a = jnp.dot(ya_ref[0], wa_ref[...], preferred_element_type=F32)
    merged_af = (_sigmoid(ga_ref[0].astype(F32)) * t_a
                 + branch(yf_ref[...], wf_ref, gf_ref))
    bits = pltpu.bitcast(t_a[0:SUBLANES, 0:LANES], jnp.uint32)
    zero = pltpu.bitcast((bits >> 16) >> 16, F32)[0:1, 0:1]

    u = cg_ref[0].astype(F32) * xin_ref[0].astype(F32)
    up = cg_p_ref[0, pad - 1:pad, :].astype(F32) * xin_p_ref[0, pad - 1:pad, :].astype(F32)
    un = cg_n_ref[0, 0:1, :].astype(F32) * xin_n_ref[0, 0:1, :].astype(F32)
    up = jnp.where(i == 0, 0.0, up)
    un = jnp.where(i == pl.num_programs(1) - 1, 0.0, un)
    row = lax.broadcasted_iota(jnp.int32, (tm, 1), 0)
    u_prev = jnp.where(row == 0, up, pltpu.roll(u, 1, axis=0))
    u_next = jnp.where(row == tm - 1, un, pltpu.roll(u, tm - 1, axis=0))
    cw = cw_ref[...] + zero
    conv = cw[0:1] * u_prev + cw[1:2] * u + cw[2:3] * u_next
    y_c = bg_ref[0].astype(F32) * conv
    yc_gated = (y_c * _silu(zc_ref[0].astype(F32))).astype(BF16)
    merged = merged_af + branch(yc_gated, wc_ref, gc_ref)
    out = jnp.dot(merged.astype(BF16), wo_ref[...], preferred_element_type=F32)
    x_new = x_ref[0] + gate_ref[0] * out
    if fnw_ref is not None:
        y = x_new * lax.rsqrt(jnp.mean(x_new * x_new, axis=-1, keepdims=True) + NORM_EPS)
        x_new = y * fnw_ref[...]
    o_ref[0] = x_new


def _merge(x, p, col0, ya, yf, yf_spec, mod3, mod_row, conv_w, wa, wc, wf, wo, layer, tm,
           final_norm_w=None):
    bx, sx, d = x.shape
    nb = tm // SUBLANES
    last_halo = sx // SUBLANES - 1

    def whole(b, i):
        return (b, i, 0)

    def col(cb):
        return pl.BlockSpec((1, tm, d), lambda b, i: (b, i, cb - col0))

    def halo_prev(cb):
        return pl.BlockSpec((1, SUBLANES, d),
                            lambda b, i: (b, jnp.maximum(i * nb - 1, 0), cb - col0))

    def halo_next(cb):
        return pl.BlockSpec((1, SUBLANES, d),
                            lambda b, i: (b, jnp.minimum((i + 1) * nb, last_halo), cb - col0))

    def weight():
        return pl.BlockSpec((None, d, d), lambda b, i: (layer, 0, 0), pipeline_mode=pl.Buffered(1))

    in_specs = [
        pl.BlockSpec((1, tm, d), whole), yf_spec,
        col(COL_XIN), col(COL_BG), col(COL_CG), col(COL_ZC),
        col(COL_GL), col(COL_GL + 1), col(COL_GL + 2),
        halo_prev(COL_XIN), halo_prev(COL_CG), halo_next(COL_XIN), halo_next(COL_CG),
        pl.BlockSpec((1, tm, d), whole),
        pl.BlockSpec((1, 1, d), lambda b, i: (mod_row(b), 0, 2)),
        pl.BlockSpec((None, 3, d), lambda b, i: (layer, 0, 0)),
        weight(), weight(), weight(), weight(),
    ]
    args = [ya, yf, p, p, p, p, p, p, p, p, p, p, p, x, mod3, conv_w, wa, wc, wf, wo]
    if final_norm_w is not None:
        in_specs.append(pl.BlockSpec((1, d), lambda b, i: (0, 0)))
        args.append(final_norm_w)
    return pl.pallas_call(
        _merge_kernel,
        grid=(bx, sx // tm),
        in_specs=in_specs,
        out_specs=pl.BlockSpec((1, tm, d), lambda b, i: (b, i, 0)),
        out_shape=jax.ShapeDtypeStruct((bx, sx, d), F32),
        compiler_params=_cparams(("parallel", "parallel")),
        name="merge",
    )(*args)


def _rope_tables(n_tokens):
    axis_dim = HEAD_DIM // 2
    rows = n_tokens // GRID_W
    row = jnp.repeat(jnp.arange(rows), GRID_W).astype(F32)
    col = jnp.tile(jnp.arange(GRID_W), rows).astype(F32)
    inv_freq = ROPE_BASE ** (-jnp.arange(0, axis_dim, 2, dtype=F32) / axis_dim)
    ang_r = row[:, None] * inv_freq
    ang_c = col[:, None] * inv_freq
    ang = jnp.concatenate([ang_r, ang_c] * 4, axis=-1)
    sign = jnp.where(jnp.arange(V_DIM) < V_DIM // 2, -1.0, 1.0).astype(F32)
    cos = jnp.cos(ang)
    sin = jnp.sin(ang) * sign
    return jnp.stack([cos * Q_SCALE, cos]), jnp.stack([sin * Q_SCALE, sin])


def _head_layout(w_qk):
    depth, d, w = w_qk.shape
    t = w_qk.reshape(depth, d, w // V_DIM, 2, 2, 2, HEAD_DIM // 4)
    return t.transpose(0, 1, 2, 5, 3, 4, 6).reshape(depth, d, w)


def _fourier_tables(n):
    cn, sn = _dft_cos_sin(n)
    return jnp.concatenate([cn, -sn], axis=1).astype(BF16)


def _pick_tile(n, target):
    t = min(n, target)
    while n % t:
        t //= 2
    return t


def kernel(x, c, ctx, c_ctx, norm_w, w_mod, b_mod, w_in, lambda_qk, subln_w, conv_w,
           w_attn_o, w_conv_o, w_four_o, w_out, final_norm_w):
    b, s, d = x.shape
    lc = ctx.shape[1]
    depth = w_in.shape[0]
    assert d == N_HEADS * V_DIM and w_in.shape[2] == N_PROJ_BLOCKS * d
    assert s % GRID_W == 0 and s % LANES == 0 and lc % SUBLANES == 0

    pad = (-(b + 1)) % MOD_ROWS_PAD
    cond = jnp.concatenate([c, c_ctx[None, :], jnp.zeros((pad, d), F32)], axis=0)
    mod = _modulation(cond, w_mod, b_mod)
    n_rows = cond.shape[0]

    norm_w = norm_w.reshape(depth, 1, d)
    subln_w = subln_w.reshape(depth, 1, V_DIM)
    n_qk = 2 * d
    w_in_b = w_in.astype(BF16)
    w_in_b = w_in_b.at[:, :, :n_qk].set(_head_layout(w_in_b[:, :, :n_qk]))
    wa_b, wc_b, wf_b, wo_b = (w.astype(BF16) for w in (w_attn_o, w_conv_o, w_four_o, w_out))

    rope_tabs = _rope_tables(s)
    cc, sc = _dft_cos_sin(FGROUP_DIM)
    cs_chan = jnp.concatenate([cc, sc], axis=1).astype(BF16)
    w_pos_ctx = _fourier_tables(lc)

    ctx_flat = ctx.reshape(1, b * lc, d)
    lat_row = lambda bb: bb
    ctx_row = lambda bb: b

    tm_lat = _pick_tile(s, 2048)
    tm_ctx = _pick_tile(b * lc, 2048)
    tmm_lat = _pick_tile(s // 2, 512)
    tmm_ctx = _pick_tile(lc, 256)
    four_tabs = _fourier_half_tables(s, tmm_lat)
    n_ftiles = s // (2 * tmm_lat)
    yf_lat_spec = pl.BlockSpec(
        (None, None, None, tmm_lat, d),
        lambda bb, i: (bb, jnp.where(i < n_ftiles, i, 2 * n_ftiles - 1 - i),
                       jnp.where(i < n_ftiles, 0, 1), 0, 0))
    yf_ctx_spec = pl.BlockSpec((None, tmm_ctx, d), lambda bb, i: (bb, i, 0))

    for l in range(depth):
        last = l == depth - 1
        lam_init = 0.8 - 0.6 * math.exp(-0.3 * l)
        mod3 = mod[l].reshape(n_rows, 1, 3 * d)

        p_qk = _proj(x, mod3, lat_row, norm_w, w_in_b, l, COL_Q, 2, rope_tabs, tm_lat)
        p_lat = _proj(x, mod3, lat_row, norm_w, w_in_b, l, COL_V, N_PROJ_BLOCKS - COL_V, None, tm_lat)
        c0, cn = (COL_K, 2) if last else (COL_Q, N_PROJ_BLOCKS)
        p_ctx = _proj(ctx_flat, mod3, ctx_row, norm_w, w_in_b, l, c0, cn, None, tm_ctx).reshape(b, lc, -1)

        ya = _attention(lambda_qk, subln_w, l, lam_init, (p_qk, COL_Q), (p_qk, COL_K),
                        (p_lat, 0), (p_lat, COL_ZA - COL_V),
                        (p_ctx, COL_K - c0), (p_ctx, COL_V - c0), carry=True)
        yf = _fourier_half(p_lat, COL_V, cs_chan, four_tabs, tmm_lat)
        x_new = _merge(x, p_lat, COL_V, ya, yf, yf_lat_spec, mod3, lat_row, conv_w, wa_b, wc_b, wf_b,
                       wo_b, l, tmm_lat, final_norm_w.reshape(1, d) if last else None)

        if not last:
            yac = _attention(lambda_qk, subln_w, l, lam_init, (p_ctx, COL_Q), (p_ctx, COL_K),
                             (p_ctx, COL_V), (p_ctx, COL_ZA), None, None, carry=False)
            yfc = _fourier(p_ctx, COL_Q, cs_chan, w_pos_ctx, lc)
            ctx3 = ctx_flat.reshape(b, lc, d)
            ctx_flat = _merge(ctx3, p_ctx, COL_Q, yac, yfc, yf_ctx_spec, mod3, ctx_row, conv_w, wa_b, wc_b, wf_b,
                              wo_b, l, tmm_ctx).reshape(1, b * lc, d)
        x = x_new

    return x
```

```python
import functools
import math

import jax
import jax.numpy as jnp
from jax import lax
from jax.experimental import pallas as pl
from jax.experimental.pallas import tpu as pltpu

F32 = jnp.float32
BF16 = jnp.bfloat16

N_HEADS = 8
HEAD_DIM = 64
V_DIM = 2 * HEAD_DIM
FGROUP_DIM = 128
GRID_W = 64
ROPE_BASE = 10000.0
NORM_EPS = 1e-6
SUBLN_EPS = 1e-5
N_PROJ_BLOCKS = 13
COL_Q, COL_K, COL_V, COL_ZA, COL_XIN, COL_BG, COL_CG, COL_ZC, COL_UF, COL_ZF, COL_GL = range(11)

LANES = 128
SUBLANES = 8
VMEM_LIMIT_BYTES = 56 * 1024 * 1024
MOD_ROWS_PAD = 8
ATTN_SUB_ROWS = 256
ATTN_CARRY_BLOCKS = 2
Q_SCALE = math.log2(math.e) / math.sqrt(HEAD_DIM)


def _cparams(sem):
    return pltpu.CompilerParams(dimension_semantics=sem, vmem_limit_bytes=VMEM_LIMIT_BYTES)


def _mod_kernel(cond_ref, w_ref, b_ref, o_ref):
    cond = cond_ref[...]
    a = cond * jax.nn.sigmoid(cond)
    o_ref[...] = jnp.dot(a, w_ref[...], preferred_element_type=F32,
                         precision=lax.Precision.HIGHEST) + b_ref[...]


def _modulation(cond, w_mod, b_mod):
    depth, d, w3 = w_mod.shape
    rows = cond.shape[0]
    tn = d
    return pl.pallas_call(
        _mod_kernel,
        grid=(depth, w3 // tn),
        in_specs=[
            pl.BlockSpec((rows, d), lambda l, j: (0, 0)),
            pl.BlockSpec((None, d, tn), lambda l, j: (l, 0, j)),
            pl.BlockSpec((None, 1, tn), lambda l, j: (l, 0, j)),
        ],
        out_specs=pl.BlockSpec((None, rows, tn), lambda l, j: (l, 0, j)),
        out_shape=jax.ShapeDtypeStruct((depth, rows, w3), F32),
        compiler_params=_cparams(("parallel", "parallel")),
        name="modulation",
    )(cond, w_mod, b_mod.reshape(depth, 1, w3))


def _proj_kernel(x_ref, sh_ref, sc_ref, nw_ref, w_ref, *rest, rope, scale_first):
    if rope:
        cos_ref, sin_ref, o_ref, h_ref = rest
    else:
        o_ref, h_ref = rest
    j = pl.program_id(2)

    @pl.when(j == 0)
    def _():
        x = x_ref[0]
        y = x * lax.rsqrt(jnp.mean(x * x, axis=-1, keepdims=True) + NORM_EPS)
        h = (y * nw_ref[...]) * (1.0 + sc_ref[0]) + sh_ref[0]
        h_ref[...] = h.astype(BF16)

    acc = jnp.dot(h_ref[...], w_ref[...], preferred_element_type=F32)
    tn = acc.shape[1]

    if rope:
        cos = cos_ref[...]
        sin = sin_ref[...]
        for hh in range(tn // LANES):
            a = acc[:, hh * LANES:(hh + 1) * LANES]
            r = a * cos + pltpu.roll(a, LANES // 2, axis=1) * sin
            o_ref[0, :, hh * LANES:(hh + 1) * LANES] = r.astype(BF16)
    elif scale_first:
        scale = jnp.where(j == 0, Q_SCALE, 1.0).astype(F32)
        o_ref[0] = (acc * scale).astype(BF16)
    else:
        o_ref[0] = acc.astype(BF16)


def _proj(x, mod3, mod_row, norm_w, w_in_b, layer, col0, ncols, rope_tabs, tm):
    bx, sx, d = x.shape
    tn = d
    rope = rope_tabs is not None
    assert not rope or (col0, ncols) == (COL_Q, 2)
    in_specs = [
        pl.BlockSpec((1, tm, d), lambda b, i, j: (b, i, 0)),
        pl.BlockSpec((1, 1, d), lambda b, i, j: (mod_row(b), 0, 0)),
        pl.BlockSpec((1, 1, d), lambda b, i, j: (mod_row(b), 0, 1)),
        pl.BlockSpec((None, 1, d), lambda b, i, j: (layer, 0, 0)),
        pl.BlockSpec((None, d, tn), lambda b, i, j: (layer, 0, col0 + j)),
    ]
    args = [x, mod3, mod3, norm_w, w_in_b]
    if rope:
        cos_t, sin_t = rope_tabs
        tab_spec = pl.BlockSpec((None, tm, LANES), lambda b, i, j: (j, i, 0))
        in_specs += [tab_spec, tab_spec]
        args += [cos_t, sin_t]
    return pl.pallas_call(
        functools.partial(_proj_kernel, rope=rope, scale_first=(col0 == COL_Q)),
        grid=(bx, sx // tm, ncols),
        in_specs=in_specs,
        out_specs=pl.BlockSpec((1, tm, tn), lambda b, i, j: (b, i, j)),
        out_shape=jax.ShapeDtypeStruct((bx, sx, ncols * tn), BF16),
        scratch_shapes=[pltpu.VMEM((tm, d), BF16)],
        compiler_params=_cparams(("parallel", "parallel", "arbitrary")),
        name="proj_rope" if rope else "proj",
    )(*args)


def _attn_kernel(lq_ref, sw_ref, q_ref, k_ref, v_ref, z_ref, *rest, has_ctx, carry, lam_init):
    rest = list(rest)
    kc_ref, vc_ref = (rest.pop(0), rest.pop(0)) if has_ctx else (None, None)
    if carry:
        qn_ref, kn_ref = rest.pop(0), rest.pop(0)
        kcn_ref = rest.pop(0) if has_ctx else None
        o_ref, kall_ref, vext_ref, kalln_ref, scarry_ref = rest
    else:
        o_ref, kall_ref, vext_ref = rest
    n_lat = k_ref.shape[1]
    n_all = kall_ref.shape[0]

    def gather_keys(dst_ref, lat_ref, ctx_ref):
        dst_ref[0:n_lat, :] = lat_ref[0]
        if has_ctx:
            dst_ref[n_lat:n_all, :] = ctx_ref[0]

    gather_keys(kall_ref, k_ref, kc_ref)
    vext_ref[0:n_lat, 0:V_DIM] = v_ref[0]
    if has_ctx:
        vext_ref[n_lat:n_all, 0:V_DIM] = vc_ref[0]
    ones_lane = lax.broadcasted_iota(jnp.int32, (n_all, V_DIM), 1) == 0
    vext_ref[:, V_DIM:2 * V_DIM] = jnp.where(ones_lane, 1.0, 0.0).astype(BF16)

    lq = lq_ref[...].astype(F32)
    lam = (jnp.exp(jnp.sum(lq[0:1] * lq[1:2], axis=-1, keepdims=True))
           - jnp.exp(jnp.sum(lq[2:3] * lq[3:4], axis=-1, keepdims=True)) + lam_init)

    tq = q_ref.shape[1]
    sub = min(tq, ATTN_SUB_ROWS)
    lane = lax.broadcasted_iota(jnp.int32, (sub, V_DIM), 1)

    def scores(q, keys_ref):
        zero = jnp.zeros_like(q)
        map1 = (lane % HEAD_DIM) < HEAD_DIM // 2
        qq = jnp.concatenate([jnp.where(map1, q, zero), jnp.where(map1, zero, q)], axis=0)
        return lax.dot_general(qq, keys_ref[...], (((1,), (1,)), ((), ())),
                               preferred_element_type=F32)

    def finish(j, get_s):
        m = jnp.max(get_s(), axis=-1, keepdims=True)
        p = jnp.exp2(get_s() - m).astype(BF16)
        nd = jnp.dot(p, vext_ref[...], preferred_element_type=F32)
        o = nd[:, 0:V_DIM] / nd[:, V_DIM:V_DIM + 1]
        o = o[:sub] - lam * o[sub:]
        y = o * lax.rsqrt(jnp.mean(o * o, axis=-1, keepdims=True) + SUBLN_EPS)
        y = (y * sw_ref[...]) * (1.0 - lam_init)
        z = z_ref[0, j * sub:(j + 1) * sub, :].astype(F32)
        o_ref[0, j * sub:(j + 1) * sub, :] = (y * _silu(z)).astype(BF16)

    def value(v):
        return lambda: v

    def rows(ref, j):
        return ref[0, j * sub:(j + 1) * sub, :]

    n_sub = tq // sub
    if carry:
        depth = scarry_ref.shape[0]

        @pl.when((pl.program_id(0) == 0) & (pl.program_id(1) == 0))
        def _():
            for d in range(depth):
                scarry_ref[d] = scores(rows(q_ref, d), kall_ref)

        gather_keys(kalln_ref, kn_ref, kcn_ref)
        pending = [(lambda d=d: scarry_ref[d]) for d in range(depth)]
    else:
        depth = 1
        pending = [value(scores(rows(q_ref, 0), kall_ref))]

    for j in range(n_sub):
        if j + depth < n_sub:
            pending.append(value(scores(rows(q_ref, j + depth), kall_ref)))
        elif carry:
            pending.append(value(scores(rows(qn_ref, j + depth - n_sub), kalln_ref)))
        finish(j, pending.pop(0))
    if carry:
        for d in range(depth):
            scarry_ref[d] = pending[d]()


def _self_attn_kernel(lq_ref, sw_ref, q_ref, k_ref, v_ref, z_ref, o_ref, *, lam_init):
    n = q_ref.shape[1]
    lq = lq_ref[...].astype(F32)
    lam = (jnp.exp(jnp.sum(lq[0:1] * lq[1:2], axis=-1, keepdims=True))
           - jnp.exp(jnp.sum(lq[2:3] * lq[3:4], axis=-1, keepdims=True)) + lam_init)
    lane = lax.broadcasted_iota(jnp.int32, (n, V_DIM), 1)
    map1 = (lane % HEAD_DIM) < HEAD_DIM // 2
    ones_cols = jnp.where(lane == 0, 1.0, 0.0).astype(BF16)
    heads = [slice(h * V_DIM, (h + 1) * V_DIM) for h in range(q_ref.shape[2] // V_DIM)]

    def scores(cols):
        q = q_ref[0, :, cols]
        zero = jnp.zeros_like(q)
        qq = jnp.concatenate([jnp.where(map1, q, zero), jnp.where(map1, zero, q)], axis=0)
        return lax.dot_general(qq, k_ref[0, :, cols], (((1,), (1,)), ((), ())),
                               preferred_element_type=F32)

    s_all = [scores(cols) for cols in heads]
    p_all = [jnp.exp2(s - jnp.max(s, axis=-1, keepdims=True)).astype(BF16) for s in s_all]
    nd_all = [jnp.dot(p, jnp.concatenate([v_ref[0, :, cols], ones_cols], axis=1),
                      preferred_element_type=F32) for p, cols in zip(p_all, heads)]
    for nd, cols in zip(nd_all, heads):
        o = nd[:, 0:V_DIM] / nd[:, V_DIM:V_DIM + 1]
        o = o[:n] - lam * o[n:]
        y = o * lax.rsqrt(jnp.mean(o * o, axis=-1, keepdims=True) + SUBLN_EPS)
        y = (y * sw_ref[...]) * (1.0 - lam_init)
        o_ref[0, :, cols] = (y * _silu(z_ref[0, :, cols].astype(F32))).astype(BF16)


def _self_attention(lambda_qk, subln_w, layer, lam_init, p, col0):
    b, n, _ = p.shape
    d = N_HEADS * V_DIM

    def col(cb):
        return pl.BlockSpec((1, n, d), lambda bb: (bb, 0, cb - col0))

    return pl.pallas_call(
        functools.partial(_self_attn_kernel, lam_init=lam_init),
        grid=(b,),
        in_specs=[
            pl.BlockSpec((None, 4, HEAD_DIM), lambda bb: (layer, 0, 0)),
            pl.BlockSpec((None, 1, V_DIM), lambda bb: (layer, 0, 0)),
            col(COL_Q), col(COL_K), col(COL_V), col(COL_ZA),
        ],
        out_specs=pl.BlockSpec((1, n, d), lambda bb: (bb, 0, 0)),
        out_shape=jax.ShapeDtypeStruct((b, n, d), BF16),
        compiler_params=_cparams(("parallel",)),
        name="attn_self",
    )(lambda_qk, subln_w, p, p, p, p)


def _attention(lambda_qk, subln_w, layer, lam_init, q, k, v, z, kc, vc, carry):
    b, s, _ = q[0].shape
    nh = N_HEADS
    has_ctx = kc is not None
    sub = min(s, ATTN_SUB_ROWS)

    def head_cols(rows, col):
        return pl.BlockSpec((1, rows, V_DIM), lambda bb, h: (bb, 0, col * nh + h))

    def next_head_cols(rows, col):
        def index(bb, h):
            flat = jnp.minimum(bb * nh + h + 1, b * nh - 1)
            return (flat // nh, 0, col * nh + flat % nh)
        return pl.BlockSpec((1, rows, V_DIM), index)

    in_specs = [
        pl.BlockSpec((None, 4, HEAD_DIM), lambda bb, h: (layer, 0, 0)),
        pl.BlockSpec((None, 1, V_DIM), lambda bb, h: (layer, 0, 0)),
        head_cols(s, q[1]), head_cols(s, k[1]), head_cols(s, v[1]), head_cols(s, z[1]),
    ]
    args = [lambda_qk, subln_w, q[0], k[0], v[0], z[0]]
    n_keys = s
    if has_ctx:
        lc = kc[0].shape[1]
        n_keys = s + lc
        in_specs += [head_cols(lc, kc[1]), head_cols(lc, vc[1])]
        args += [kc[0], vc[0]]
    scratch = [pltpu.VMEM((n_keys, V_DIM), BF16), pltpu.VMEM((n_keys, 2 * V_DIM), BF16)]
    if carry:
        depth = min(ATTN_CARRY_BLOCKS, s // sub)
        in_specs += [next_head_cols(depth * sub, q[1]), next_head_cols(s, k[1])]
        args += [q[0], k[0]]
        if has_ctx:
            in_specs.append(next_head_cols(lc, kc[1]))
            args.append(kc[0])
        scratch += [pltpu.VMEM((n_keys, V_DIM), BF16), pltpu.VMEM((depth, 2 * sub, n_keys), F32)]
    return pl.pallas_call(
        functools.partial(_attn_kernel, has_ctx=has_ctx, carry=carry, lam_init=lam_init),
        grid=(b, nh),
        in_specs=in_specs,
        out_specs=head_cols(s, 0),
        out_shape=jax.ShapeDtypeStruct((b, s, nh * V_DIM), BF16),
        scratch_shapes=scratch,
        compiler_params=_cparams(("arbitrary", "arbitrary")),
        name="attn_ctx" if has_ctx else "attn",
    )(*args)


def _fourier_kernel(u_ref, cs_ref, w_ref, z_ref, o_ref, ab_ref):
    r = pl.program_id(1)
    n = u_ref.shape[1]

    @pl.when(r == 0)
    def _():
        for g in range(u_ref.shape[2] // FGROUP_DIM):
            cols = slice(g * FGROUP_DIM, (g + 1) * FGROUP_DIM)
            ab = jnp.dot(u_ref[0, :, cols], cs_ref[...], preferred_element_type=F32)
            ab_ref[0:n, cols] = ab[:, :FGROUP_DIM].astype(BF16)
            ab_ref[n:2 * n, cols] = ab[:, FGROUP_DIM:].astype(BF16)

    y = jnp.dot(w_ref[...], ab_ref[...], preferred_element_type=F32)
    o_ref[0] = (y * _silu(z_ref[0].astype(F32))).astype(BF16)


def _dft_cos_sin(n):
    k = jnp.arange(n, dtype=jnp.int32)
    ang = ((k[:, None] * k[None, :]) % n).astype(F32) * (2.0 * math.pi / n)
    scale = 1.0 / math.sqrt(n)
    return jnp.cos(ang) * scale, jnp.sin(ang) * scale


def _fourier(p, col0, cs_chan, w_pos, tr):
    b, n, _ = p.shape
    d = N_HEADS * V_DIM
    return pl.pallas_call(
        _fourier_kernel,
        grid=(b, n // tr),
        in_specs=[
            pl.BlockSpec((1, n, d), lambda bb, r: (bb, 0, COL_UF - col0)),
            pl.BlockSpec((FGROUP_DIM, 2 * FGROUP_DIM), lambda bb, r: (0, 0)),
            pl.BlockSpec((tr, 2 * n), lambda bb, r: (r, 0)),
            pl.BlockSpec((1, tr, d), lambda bb, r: (bb, r, COL_ZF - col0)),
        ],
        out_specs=pl.BlockSpec((1, tr, d), lambda bb, r: (bb, r, 0)),
        out_shape=jax.ShapeDtypeStruct((b, n, d), BF16),
        scratch_shapes=[pltpu.VMEM((2 * n, d), BF16)],
        compiler_params=_cparams(("parallel", "arbitrary")),
        name="fourier",
    )(p, cs_chan, w_pos, p)


def _fourier_half_kernel(u_ref, cs_ref, c_ref, s_ref, rev_ref, zlo_ref, zhi_ref, o_ref, a_ref, b_ref):
    r = pl.program_id(1)
    tr = o_ref.shape[3]

    @pl.when(r == 0)
    def _():
        for g in range(u_ref.shape[2] // FGROUP_DIM):
            cols = slice(g * FGROUP_DIM, (g + 1) * FGROUP_DIM)
            ab = jnp.dot(u_ref[0, :, cols], cs_ref[...], preferred_element_type=F32)
            a_ref[:, cols] = ab[:, :FGROUP_DIM].astype(BF16)
            b_ref[:, cols] = ab[:, FGROUP_DIM:].astype(BF16)

    p = jnp.dot(c_ref[...], a_ref[...], preferred_element_type=F32)
    q = jnp.dot(s_ref[...], b_ref[...], preferred_element_type=F32)
    o_ref[0, 0, 0] = ((p[:tr] - q[:tr]) * _silu(zlo_ref[0].astype(F32))).astype(BF16)
    hi = jnp.dot(rev_ref[...], (p + q).astype(BF16), preferred_element_type=F32)
    o_ref[0, 0, 1] = (hi * _silu(zhi_ref[0].astype(F32))).astype(BF16)


def _fourier_half_tables(n, tr):
    ext = 2 * SUBLANES
    n_tiles = n // (2 * tr)
    rows = (jnp.arange(n_tiles, dtype=jnp.int32)[:, None] * tr
            + jnp.arange(tr + ext, dtype=jnp.int32)[None, :])
    k = jnp.arange(n, dtype=jnp.int32)
    ang = ((rows[:, :, None] * k[None, None, :]) % n).astype(F32) * (2.0 * math.pi / n)
    scale = 1.0 / math.sqrt(n)
    i = jnp.arange(tr, dtype=jnp.int32)[:, None]
    j = jnp.arange(tr + ext, dtype=jnp.int32)[None, :]
    rev = (j == tr - i).astype(BF16)
    return (jnp.cos(ang) * scale).astype(BF16), (jnp.sin(ang) * scale).astype(BF16), rev


def _fourier_half(p, col0, cs_chan, tabs, tr):
    b, n, _ = p.shape
    d = N_HEADS * V_DIM
    c_t, s_t, rev = tabs
    n_tiles, rows_ext, _ = c_t.shape
    return pl.pallas_call(
        _fourier_half_kernel,
        grid=(b, n_tiles),
        in_specs=[
            pl.BlockSpec((1, n, d), lambda bb, r: (bb, 0, COL_UF - col0)),
            pl.BlockSpec((FGROUP_DIM, 2 * FGROUP_DIM), lambda bb, r: (0, 0)),
            pl.BlockSpec((None, rows_ext, n), lambda bb, r: (r, 0, 0)),
            pl.BlockSpec((None, rows_ext, n), lambda bb, r: (r, 0, 0)),
            pl.BlockSpec((tr, rows_ext), lambda bb, r: (0, 0)),
            pl.BlockSpec((1, tr, d), lambda bb, r: (bb, r, COL_ZF - col0)),
            pl.BlockSpec((1, tr, d), lambda bb, r: (bb, 2 * n_tiles - 1 - r, COL_ZF - col0)),
        ],
        out_specs=pl.BlockSpec((1, 1, 2, tr, d), lambda bb, r: (bb, r, 0, 0, 0)),
        out_shape=jax.ShapeDtypeStruct((b, n_tiles, 2, tr, d), BF16),
        scratch_shapes=[pltpu.VMEM((n, d), BF16), pltpu.VMEM((n, d), BF16)],
        compiler_params=_cparams(("parallel", "arbitrary")),
        name="fourier_half",
    )(p, cs_chan, c_t, s_t, rev, p, p)


def _sigmoid(z):
    return 0.5 * jnp.tanh(0.5 * z) + 0.5


def _silu(z):
    h = 0.5 * z
    return h * jnp.tanh(h) + h


def _merge_kernel(ya_ref, yf_ref, xin_ref, bg_ref, cg_ref, zc_ref,
                  ga_ref, gc_ref, gf_ref, xin_p_ref, cg_p_ref, xin_n_ref, cg_n_ref,
                  x_ref, gate_ref, cw_ref, wa_ref, wc_ref, wf_ref, wo_ref, *rest):
    fnw_ref, o_ref = rest if len(rest) == 2 else (None, rest[0])
    i = pl.program_id(1)
    tm = x_ref.shape[1]
    pad = SUBLANES

    def branch(y_gated, w_ref, g_ref):
        t = jnp.dot(y_gated, w_ref[...], preferred_element_type=F32)
        return _sigmoid(g_ref[0].astype(F32)) * t

    t_a = jnp.dot(ya_ref[0], wa_ref[...], preferred_element_type=F32)
    merged_af = (_sigmoid(ga_ref[0].astype(F32)) * t_a
                 + branch(yf_ref[...], wf_ref, gf_ref))
    bits = pltpu.bitcast(t_a[0:SUBLANES, 0:LANES], jnp.uint32)
    zero = pltpu.bitcast((bits >> 16) >> 16, F32)[0:1, 0:1]

    u = cg_ref[0].astype(F32) * xin_ref[0].astype(F32)
    up = cg_p_ref[0, pad - 1:pad, :].astype(F32) * xin_p_ref[0, pad - 1:pad, :].astype(F32)
    un = cg_n_ref[0, 0:1, :].astype(F32) * xin_n_ref[0, 0:1, :].astype(F32)
    up = jnp.where(i == 0, 0.0, up)
    un = jnp.where(i == pl.num_programs(1) - 1, 0.0, un)
    row = lax.broadcasted_iota(jnp.int32, (tm, 1), 0)
    u_prev = jnp.where(row == 0, up, pltpu.roll(u, 1, axis=0))
    u_next = jnp.where(row == tm - 1, un, pltpu.roll(u, tm - 1, axis=0))
    cw = cw_ref[...] + zero
    conv = cw[0:1] * u_prev + cw[1:2] * u + cw[2:3] * u_next
    y_c = bg_ref[0].astype(F32) * conv
    yc_gated = (y_c * _silu(zc_ref[0].astype(F32))).astype(BF16)
    merged = merged_af + branch(yc_gated, wc_ref, gc_ref)
    out = jnp.dot(merged.astype(BF16), wo_ref[...], preferred_element_type=F32)
    x_new = x_ref[0] + gate_ref[0] * out
    if fnw_ref is not None:
        y = x_new * lax.rsqrt(jnp.mean(x_new * x_new, axis=-1, keepdims=True) + NORM_EPS)
        x_new = y * fnw_ref[...]
    o_ref[0] = x_new


def _merge(x, p, col0, ya, yf, yf_spec, mod3, mod_row, conv_w, wa, wc, wf, wo, layer, tm,
           final_norm_w=None):
    bx, sx, d = x.shape
    nb = tm // SUBLANES
    last_halo = sx // SUBLANES - 1

    def whole(b, i):
        return (b, i, 0)

    def col(cb):
        return pl.BlockSpec((1, tm, d), lambda b, i: (b, i, cb - col0))

    def halo_prev(cb):
        return pl.BlockSpec((1, SUBLANES, d),
                            lambda b, i: (b, jnp.maximum(i * nb - 1, 0), cb - col0))

    def halo_next(cb):
        return pl.BlockSpec((1, SUBLANES, d),
                            lambda b, i: (b, jnp.minimum((i + 1) * nb, last_halo), cb - col0))

    def weight():
        return pl.BlockSpec((None, d, d), lambda b, i: (layer, 0, 0), pipeline_mode=pl.Buffered(1))

    in_specs = [
        pl.BlockSpec((1, tm, d), whole), yf_spec,
        col(COL_XIN), col(COL_BG), col(COL_CG), col(COL_ZC),
        col(COL_GL), col(COL_GL + 1), col(COL_GL + 2),
        halo_prev(COL_XIN), halo_prev(COL_CG), halo_next(COL_XIN), halo_next(COL_CG),
        pl.BlockSpec((1, tm, d), whole),
        pl.BlockSpec((1, 1, d), lambda b, i: (mod_row(b), 0, 2)),
        pl.BlockSpec((None, 3, d), lambda b, i: (layer, 0, 0)),
        weight(), weight(), weight(), weight(),
    ]
    args = [ya, yf, p, p, p, p, p, p, p, p, p, p, p, x, mod3, conv_w, wa, wc, wf, wo]
    if final_norm_w is not None:
        in_specs.append(pl.BlockSpec((1, d), lambda b, i: (0, 0)))
        args.append(final_norm_w)
    return pl.pallas_call(
        _merge_kernel,
        grid=(bx, sx // tm),
        in_specs=in_specs,
        out_specs=pl.BlockSpec((1, tm, d), lambda b, i: (b, i, 0)),
        out_shape=jax.ShapeDtypeStruct((bx, sx, d), F32),
        compiler_params=_cparams(("parallel", "parallel")),
        name="merge",
    )(*args)


def _rope_tables(n_tokens):
    axis_dim = HEAD_DIM // 2
    rows = n_tokens // GRID_W
    row = jnp.repeat(jnp.arange(rows), GRID_W).astype(F32)
    col = jnp.tile(jnp.arange(GRID_W), rows).astype(F32)
    inv_freq = ROPE_BASE ** (-jnp.arange(0, axis_dim, 2, dtype=F32) / axis_dim)
    ang_r = row[:, None] * inv_freq
    ang_c = col[:, None] * inv_freq
    ang = jnp.concatenate([ang_r, ang_c] * 4, axis=-1)
    sign = jnp.where(jnp.arange(V_DIM) < V_DIM // 2, -1.0, 1.0).astype(F32)
    cos = jnp.cos(ang)
    sin = jnp.sin(ang) * sign
    return jnp.stack([cos * Q_SCALE, cos]), jnp.stack([sin * Q_SCALE, sin])


def _head_layout(w_qk):
    depth, d, w = w_qk.shape
    t = w_qk.reshape(depth, d, w // V_DIM, 2, 2, 2, HEAD_DIM // 4)
    return t.transpose(0, 1, 2, 5, 3, 4, 6).reshape(depth, d, w)


def _fourier_tables(n):
    cn, sn = _dft_cos_sin(n)
    return jnp.concatenate([cn, -sn], axis=1).astype(BF16)


def _pick_tile(n, target):
    t = min(n, target)
    while n % t:
        t //= 2
    return t


def kernel(x, c, ctx, c_ctx, norm_w, w_mod, b_mod, w_in, lambda_qk, subln_w, conv_w,
           w_attn_o, w_conv_o, w_four_o, w_out, final_norm_w):
    b, s, d = x.shape
    lc = ctx.shape[1]
    depth = w_in.shape[0]
    assert d == N_HEADS * V_DIM and w_in.shape[2] == N_PROJ_BLOCKS * d
    assert s % GRID_W == 0 and s % LANES == 0 and lc % SUBLANES == 0

    pad = (-(b + 1)) % MOD_ROWS_PAD
    cond = jnp.concatenate([c, c_ctx[None, :], jnp.zeros((pad, d), F32)], axis=0)
    mod = _modulation(cond, w_mod, b_mod)
    n_rows = cond.shape[0]

    norm_w = norm_w.reshape(depth, 1, d)
    subln_w = subln_w.reshape(depth, 1, V_DIM)
    n_qk = 2 * d
    w_in_b = w_in.astype(BF16)
    w_in_b = w_in_b.at[:, :, :n_qk].set(_head_layout(w_in_b[:, :, :n_qk]))
    wa_b, wc_b, wf_b, wo_b = (w.astype(BF16) for w in (w_attn_o, w_conv_o, w_four_o, w_out))

    rope_tabs = _rope_tables(s)
    cc, sc = _dft_cos_sin(FGROUP_DIM)
    cs_chan = jnp.concatenate([cc, sc], axis=1).astype(BF16)
    w_pos_ctx = _fourier_tables(lc)

    ctx_flat = ctx.reshape(1, b * lc, d)
    lat_row = lambda bb: bb
    ctx_row = lambda bb: b

    tm_lat = _pick_tile(s, 2048)
    tm_ctx = _pick_tile(b * lc, 2048)
    tmm_lat = _pick_tile(s // 2, 512)
    tmm_ctx = _pick_tile(lc, 256)
    four_tabs = _fourier_half_tables(s, tmm_lat)
    n_ftiles = s // (2 * tmm_lat)
    yf_lat_spec = pl.BlockSpec(
        (None, None, None, tmm_lat, d),
        lambda bb, i: (bb, jnp.where(i < n_ftiles, i, 2 * n_ftiles - 1 - i),
                       jnp.where(i < n_ftiles, 0, 1), 0, 0))
    yf_ctx_spec = pl.BlockSpec((None, tmm_ctx, d), lambda bb, i: (bb, i, 0))

    for l in range(depth):
        last = l == depth - 1
        lam_init = 0.8 - 0.6 * math.exp(-0.3 * l)
        mod3 = mod[l].reshape(n_rows, 1, 3 * d)

        p_qk = _proj(x, mod3, lat_row, norm_w, w_in_b, l, COL_Q, 2, rope_tabs, tm_lat)
        p_lat = _proj(x, mod3, lat_row, norm_w, w_in_b, l, COL_V, N_PROJ_BLOCKS - COL_V, None, tm_lat)
        c0, cn = (COL_K, 2) if last else (COL_Q, N_PROJ_BLOCKS)
        p_ctx = _proj(ctx_flat, mod3, ctx_row, norm_w, w_in_b, l, c0, cn, None, tm_ctx).reshape(b, lc, -1)

        ya = _attention(lambda_qk, subln_w, l, lam_init, (p_qk, COL_Q), (p_qk, COL_K),
                        (p_lat, 0), (p_lat, COL_ZA - COL_V),
                        (p_ctx, COL_K - c0), (p_ctx, COL_V - c0), carry=True)
        yf = _fourier_half(p_lat, COL_V, cs_chan, four_tabs, tmm_lat)
        x_new = _merge(x, p_lat, COL_V, ya, yf, yf_lat_spec, mod3, lat_row, conv_w, wa_b, wc_b, wf_b,
                       wo_b, l, tmm_lat, final_norm_w.reshape(1, d) if last else None)

        if not last:
            yac = _self_attention(lambda_qk, subln_w, l, lam_init, p_ctx, COL_Q)
            yfc = _fourier(p_ctx, COL_Q, cs_chan, w_pos_ctx, lc)
            ctx3 = ctx_flat.reshape(b, lc, d)
            ctx_flat = _merge(ctx3, p_ctx, COL_Q, yac, yfc, yf_ctx_spec, mod3, ctx_row, conv_w, wa_b, wc_b, wf_b,
                              wo_b, l, tmm_ctx).reshape(1, b * lc, d)
        x = x_new

    return x
```

```python
import functools
import math

import jax
import jax.numpy as jnp
from jax import lax
from jax.experimental import pallas as pl
from jax.experimental.pallas import tpu as pltpu

F32 = jnp.float32
BF16 = jnp.bfloat16

N_HEADS = 8
HEAD_DIM = 64
V_DIM = 2 * HEAD_DIM
FGROUP_DIM = 128
GRID_W = 64
ROPE_BASE = 10000.0
NORM_EPS = 1e-6
SUBLN_EPS = 1e-5
N_PROJ_BLOCKS = 13
COL_Q, COL_K, COL_V, COL_ZA, COL_XIN, COL_BG, COL_CG, COL_ZC, COL_UF, COL_ZF, COL_GL = range(11)

LANES = 128
SUBLANES = 8
VMEM_LIMIT_BYTES = 56 * 1024 * 1024
MOD_ROWS_PAD = 8
ATTN_SUB_ROWS = 256
ATTN_CARRY_BLOCKS = 2
Q_SCALE = math.log2(math.e) / math.sqrt(HEAD_DIM)


def _cparams(sem):
    return pltpu.CompilerParams(dimension_semantics=sem, vmem_limit_bytes=VMEM_LIMIT_BYTES)


def _mod_kernel(cond_ref, w_ref, b_ref, o_ref):
    cond = cond_ref[...]
    a = cond * jax.nn.sigmoid(cond)
    o_ref[...] = jnp.dot(a, w_ref[...], preferred_element_type=F32,
                         precision=lax.Precision.HIGHEST) + b_ref[...]


def _modulation(cond, w_mod, b_mod):
    depth, d, w3 = w_mod.shape
    rows = cond.shape[0]
    tn = d
    return pl.pallas_call(
        _mod_kernel,
        grid=(depth, w3 // tn),
        in_specs=[
            pl.BlockSpec((rows, d), lambda l, j: (0, 0)),
            pl.BlockSpec((None, d, tn), lambda l, j: (l, 0, j)),
            pl.BlockSpec((None, 1, tn), lambda l, j: (l, 0, j)),
        ],
        out_specs=pl.BlockSpec((None, rows, tn), lambda l, j: (l, 0, j)),
        out_shape=jax.ShapeDtypeStruct((depth, rows, w3), F32),
        compiler_params=_cparams(("parallel", "parallel")),
        name="modulation",
    )(cond, w_mod, b_mod.reshape(depth, 1, w3))


def _proj_kernel(x_ref, sh_ref, sc_ref, nw_ref, w_ref, *rest, rope, scale_first):
    if rope:
        cos_ref, sin_ref, o_ref, h_ref = rest
    else:
        o_ref, h_ref = rest
    j = pl.program_id(2)

    @pl.when(j == 0)
    def _():
        x = x_ref[0]
        y = x * lax.rsqrt(jnp.mean(x * x, axis=-1, keepdims=True) + NORM_EPS)
        h = (y * nw_ref[...]) * (1.0 + sc_ref[0]) + sh_ref[0]
        h_ref[...] = h.astype(BF16)

    acc = jnp.dot(h_ref[...], w_ref[...], preferred_element_type=F32)
    tn = acc.shape[1]

    if rope:
        cos = cos_ref[...]
        sin = sin_ref[...]
        for hh in range(tn // LANES):
            a = acc[:, hh * LANES:(hh + 1) * LANES]
            r = a * cos + pltpu.roll(a, LANES // 2, axis=1) * sin
            o_ref[0, :, hh * LANES:(hh + 1) * LANES] = r.astype(BF16)
    elif scale_first:
        scale = jnp.where(j == 0, Q_SCALE, 1.0).astype(F32)
        o_ref[0] = (acc * scale).astype(BF16)
    else:
        o_ref[0] = acc.astype(BF16)


def _proj(x, mod3, mod_row, norm_w, w_in_b, layer, col0, ncols, rope_tabs, tm):
    bx, sx, d = x.shape
    tn = d
    rope = rope_tabs is not None
    assert not rope or (col0, ncols) == (COL_Q, 2)
    in_specs = [
        pl.BlockSpec((1, tm, d), lambda b, i, j: (b, i, 0)),
        pl.BlockSpec((1, 1, d), lambda b, i, j: (mod_row(b), 0, 0)),
        pl.BlockSpec((1, 1, d), lambda b, i, j: (mod_row(b), 0, 1)),
        pl.BlockSpec((None, 1, d), lambda b, i, j: (layer, 0, 0)),
        pl.BlockSpec((None, d, tn), lambda b, i, j: (layer, 0, col0 + j)),
    ]
    args = [x, mod3, mod3, norm_w, w_in_b]
    if rope:
        cos_t, sin_t = rope_tabs
        tab_spec = pl.BlockSpec((None, tm, LANES), lambda b, i, j: (j, i, 0))
        in_specs += [tab_spec, tab_spec]
        args += [cos_t, sin_t]
    return pl.pallas_call(
        functools.partial(_proj_kernel, rope=rope, scale_first=(col0 == COL_Q)),
        grid=(bx, sx // tm, ncols),
        in_specs=in_specs,
        out_specs=pl.BlockSpec((1, tm, tn), lambda b, i, j: (b, i, j)),
        out_shape=jax.ShapeDtypeStruct((bx, sx, ncols * tn), BF16),
        scratch_shapes=[pltpu.VMEM((tm, d), BF16)],
        compiler_params=_cparams(("parallel", "parallel", "arbitrary")),
        name="proj_rope" if rope else "proj",
    )(*args)


def _attn_kernel(lq_ref, sw_ref, q_ref, k_ref, v_ref, z_ref, kc_ref, vc_ref, qn_ref, kn_ref, kcn_ref,
                 o_ref, kall_ref, vext_ref, kalln_ref, scarry_ref, *, lam_init):
    n_lat = k_ref.shape[1]
    n_all = kall_ref.shape[0]

    def gather_keys(dst_ref, lat_ref, ctx_ref):
        dst_ref[0:n_lat, :] = lat_ref[0]
        dst_ref[n_lat:n_all, :] = ctx_ref[0]

    gather_keys(kall_ref, k_ref, kc_ref)
    vext_ref[0:n_lat, 0:V_DIM] = v_ref[0]
    vext_ref[n_lat:n_all, 0:V_DIM] = vc_ref[0]
    ones_lane = lax.broadcasted_iota(jnp.int32, (n_all, V_DIM), 1) == 0
    vext_ref[:, V_DIM:2 * V_DIM] = jnp.where(ones_lane, 1.0, 0.0).astype(BF16)

    lq = lq_ref[...].astype(F32)
    lam = (jnp.exp(jnp.sum(lq[0:1] * lq[1:2], axis=-1, keepdims=True))
           - jnp.exp(jnp.sum(lq[2:3] * lq[3:4], axis=-1, keepdims=True)) + lam_init)

    tq = q_ref.shape[1]
    sub = min(tq, ATTN_SUB_ROWS)
    lane = lax.broadcasted_iota(jnp.int32, (sub, V_DIM), 1)

    def scores(q, keys_ref):
        zero = jnp.zeros_like(q)
        map1 = (lane % HEAD_DIM) < HEAD_DIM // 2
        qq = jnp.concatenate([jnp.where(map1, q, zero), jnp.where(map1, zero, q)], axis=0)
        return lax.dot_general(qq, keys_ref[...], (((1,), (1,)), ((), ())),
                               preferred_element_type=F32)

    def finish(j, get_s):
        m = jnp.max(get_s(), axis=-1, keepdims=True)
        p = jnp.exp2(get_s() - m).astype(BF16)
        nd = jnp.dot(p, vext_ref[...], preferred_element_type=F32)
        o = nd[:, 0:V_DIM] / nd[:, V_DIM:V_DIM + 1]
        o = o[:sub] - lam * o[sub:]
        y = o * lax.rsqrt(jnp.mean(o * o, axis=-1, keepdims=True) + SUBLN_EPS)
        y = (y * sw_ref[...]) * (1.0 - lam_init)
        z = z_ref[0, j * sub:(j + 1) * sub, :].astype(F32)
        o_ref[0, j * sub:(j + 1) * sub, :] = (y * _silu(z)).astype(BF16)

    def value(v):
        return lambda: v

    def rows(ref, j):
        return ref[0, j * sub:(j + 1) * sub, :]

    n_sub = tq // sub
    depth = scarry_ref.shape[0]

    @pl.when((pl.program_id(0) == 0) & (pl.program_id(1) == 0))
    def _():
        for d in range(depth):
            scarry_ref[d] = scores(rows(q_ref, d), kall_ref)

    gather_keys(kalln_ref, kn_ref, kcn_ref)
    pending = [(lambda d=d: scarry_ref[d]) for d in range(depth)]

    for j in range(n_sub):
        if j + depth < n_sub:
            pending.append(value(scores(rows(q_ref, j + depth), kall_ref)))
        else:
            pending.append(value(scores(rows(qn_ref, j + depth - n_sub), kalln_ref)))
        finish(j, pending.pop(0))
    for d in range(depth):
        scarry_ref[d] = pending[d]()


def _self_attn_kernel(lq_ref, sw_ref, q_ref, k_ref, v_ref, z_ref, o_ref, *, lam_init):
    n = q_ref.shape[1]
    lq = lq_ref[...].astype(F32)
    lam = (jnp.exp(jnp.sum(lq[0:1] * lq[1:2], axis=-1, keepdims=True))
           - jnp.exp(jnp.sum(lq[2:3] * lq[3:4], axis=-1, keepdims=True)) + lam_init)
    lane = lax.broadcasted_iota(jnp.int32, (n, V_DIM), 1)
    map1 = (lane % HEAD_DIM) < HEAD_DIM // 2
    ones_cols = jnp.where(lane == 0, 1.0, 0.0).astype(BF16)
    heads = [slice(h * V_DIM, (h + 1) * V_DIM) for h in range(q_ref.shape[2] // V_DIM)]

    def scores(cols):
        q = q_ref[0, :, cols]
        zero = jnp.zeros_like(q)
        qq = jnp.concatenate([jnp.where(map1, q, zero), jnp.where(map1, zero, q)], axis=0)
        return lax.dot_general(qq, k_ref[0, :, cols], (((1,), (1,)), ((), ())),
                               preferred_element_type=F32)

    s_all = [scores(cols) for cols in heads]
    p_all = [jnp.exp2(s - jnp.max(s, axis=-1, keepdims=True)).astype(BF16) for s in s_all]
    nd_all = [jnp.dot(p, jnp.concatenate([v_ref[0, :, cols], ones_cols], axis=1),
                      preferred_element_type=F32) for p, cols in zip(p_all, heads)]
    for nd, cols in zip(nd_all, heads):
        o = nd[:, 0:V_DIM] / nd[:, V_DIM:V_DIM + 1]
        o = o[:n] - lam * o[n:]
        y = o * lax.rsqrt(jnp.mean(o * o, axis=-1, keepdims=True) + SUBLN_EPS)
        y = (y * sw_ref[...]) * (1.0 - lam_init)
        o_ref[0, :, cols] = (y * _silu(z_ref[0, :, cols].astype(F32))).astype(BF16)


def _self_attention(lambda_qk, subln_w, layer, lam_init, p, col0):
    b, n, _ = p.shape
    d = N_HEADS * V_DIM

    def col(cb):
        return pl.BlockSpec((1, n, d), lambda bb: (bb, 0, cb - col0))

    return pl.pallas_call(
        functools.partial(_self_attn_kernel, lam_init=lam_init),
        grid=(b,),
        in_specs=[
            pl.BlockSpec((None, 4, HEAD_DIM), lambda bb: (layer, 0, 0)),
            pl.BlockSpec((None, 1, V_DIM), lambda bb: (layer, 0, 0)),
            col(COL_Q), col(COL_K), col(COL_V), col(COL_ZA),
        ],
        out_specs=pl.BlockSpec((1, n, d), lambda bb: (bb, 0, 0)),
        out_shape=jax.ShapeDtypeStruct((b, n, d), BF16),
        compiler_params=_cparams(("parallel",)),
        name="attn_self",
    )(lambda_qk, subln_w, p, p, p, p)


def _attention(lambda_qk, subln_w, layer, lam_init, q, k, v, z, kc, vc):
    b, s, _ = q[0].shape
    lc = kc[0].shape[1]
    n_keys = s + lc
    nh = N_HEADS
    sub = min(s, ATTN_SUB_ROWS)
    depth = min(ATTN_CARRY_BLOCKS, s // sub)

    def head_cols(rows, col):
        return pl.BlockSpec((1, rows, V_DIM), lambda bb, h: (bb, 0, col * nh + h))

    def next_head_cols(rows, col):
        def index(bb, h):
            flat = jnp.minimum(bb * nh + h + 1, b * nh - 1)
            return (flat // nh, 0, col * nh + flat % nh)
        return pl.BlockSpec((1, rows, V_DIM), index)

    in_specs = [
        pl.BlockSpec((None, 4, HEAD_DIM), lambda bb, h: (layer, 0, 0)),
        pl.BlockSpec((None, 1, V_DIM), lambda bb, h: (layer, 0, 0)),
        head_cols(s, q[1]), head_cols(s, k[1]), head_cols(s, v[1]), head_cols(s, z[1]),
        head_cols(lc, kc[1]), head_cols(lc, vc[1]),
        next_head_cols(depth * sub, q[1]), next_head_cols(s, k[1]), next_head_cols(lc, kc[1]),
    ]
    return pl.pallas_call(
        functools.partial(_attn_kernel, lam_init=lam_init),
        grid=(b, nh),
        in_specs=in_specs,
        out_specs=head_cols(s, 0),
        out_shape=jax.ShapeDtypeStruct((b, s, nh * V_DIM), BF16),
        scratch_shapes=[
            pltpu.VMEM((n_keys, V_DIM), BF16),
            pltpu.VMEM((n_keys, 2 * V_DIM), BF16),
            pltpu.VMEM((n_keys, V_DIM), BF16),
            pltpu.VMEM((depth, 2 * sub, n_keys), F32),
        ],
        compiler_params=_cparams(("arbitrary", "arbitrary")),
        name="attn_latent",
    )(lambda_qk, subln_w, q[0], k[0], v[0], z[0], kc[0], vc[0], q[0], k[0], kc[0])


def _fourier_kernel(u_ref, cs_ref, w_ref, z_ref, o_ref, ab_ref):
    r = pl.program_id(1)
    n = u_ref.shape[1]

    @pl.when(r == 0)
    def _():
        for g in range(u_ref.shape[2] // FGROUP_DIM):
            cols = slice(g * FGROUP_DIM, (g + 1) * FGROUP_DIM)
            ab = jnp.dot(u_ref[0, :, cols], cs_ref[...], preferred_element_type=F32)
            ab_ref[0:n, cols] = ab[:, :FGROUP_DIM].astype(BF16)
            ab_ref[n:2 * n, cols] = ab[:, FGROUP_DIM:].astype(BF16)

    y = jnp.dot(w_ref[...], ab_ref[...], preferred_element_type=F32)
    o_ref[0] = (y * _silu(z_ref[0].astype(F32))).astype(BF16)


def _dft_cos_sin(n):
    k = jnp.arange(n, dtype=jnp.int32)
    ang = ((k[:, None] * k[None, :]) % n).astype(F32) * (2.0 * math.pi / n)
    scale = 1.0 / math.sqrt(n)
    return jnp.cos(ang) * scale, jnp.sin(ang) * scale


def _fourier(p, col0, cs_chan, w_pos, tr):
    b, n, _ = p.shape
    d = N_HEADS * V_DIM
    return pl.pallas_call(
        _fourier_kernel,
        grid=(b, n // tr),
        in_specs=[
            pl.BlockSpec((1, n, d), lambda bb, r: (bb, 0, COL_UF - col0)),
            pl.BlockSpec((FGROUP_DIM, 2 * FGROUP_DIM), lambda bb, r: (0, 0)),
            pl.BlockSpec((tr, 2 * n), lambda bb, r: (r, 0)),
            pl.BlockSpec((1, tr, d), lambda bb, r: (bb, r, COL_ZF - col0)),
        ],
        out_specs=pl.BlockSpec((1, tr, d), lambda bb, r: (bb, r, 0)),
        out_shape=jax.ShapeDtypeStruct((b, n, d), BF16),
        scratch_shapes=[pltpu.VMEM((2 * n, d), BF16)],
        compiler_params=_cparams(("parallel", "arbitrary")),
        name="fourier",
    )(p, cs_chan, w_pos, p)


def _fourier_half_kernel(u_ref, cs_ref, c_ref, s_ref, rev_ref, zlo_ref, zhi_ref, o_ref, a_ref, b_ref):
    r = pl.program_id(1)
    tr = o_ref.shape[3]

    @pl.when(r == 0)
    def _():
        for g in range(u_ref.shape[2] // FGROUP_DIM):
            cols = slice(g * FGROUP_DIM, (g + 1) * FGROUP_DIM)
            ab = jnp.dot(u_ref[0, :, cols], cs_ref[...], preferred_element_type=F32)
            a_ref[:, cols] = ab[:, :FGROUP_DIM].astype(BF16)
            b_ref[:, cols] = ab[:, FGROUP_DIM:].astype(BF16)

    p = jnp.dot(c_ref[...], a_ref[...], preferred_element_type=F32)
    q = jnp.dot(s_ref[...], b_ref[...], preferred_element_type=F32)
    o_ref[0, 0, 0] = ((p[:tr] - q[:tr]) * _silu(zlo_ref[0].astype(F32))).astype(BF16)
    hi = jnp.dot(rev_ref[...], (p + q).astype(BF16), preferred_element_type=F32)
    o_ref[0, 0, 1] = (hi * _silu(zhi_ref[0].astype(F32))).astype(BF16)


def _fourier_half_tables(n, tr):
    ext = 2 * SUBLANES
    n_tiles = n // (2 * tr)
    rows = (jnp.arange(n_tiles, dtype=jnp.int32)[:, None] * tr
            + jnp.arange(tr + ext, dtype=jnp.int32)[None, :])
    k = jnp.arange(n, dtype=jnp.int32)
    ang = ((rows[:, :, None] * k[None, None, :]) % n).astype(F32) * (2.0 * math.pi / n)
    scale = 1.0 / math.sqrt(n)
    i = jnp.arange(tr, dtype=jnp.int32)[:, None]
    j = jnp.arange(tr + ext, dtype=jnp.int32)[None, :]
    rev = (j == tr - i).astype(BF16)
    return (jnp.cos(ang) * scale).astype(BF16), (jnp.sin(ang) * scale).astype(BF16), rev


def _fourier_half(p, col0, cs_chan, tabs, tr):
    b, n, _ = p.shape
    d = N_HEADS * V_DIM
    c_t, s_t, rev = tabs
    n_tiles, rows_ext, _ = c_t.shape
    return pl.pallas_call(
        _fourier_half_kernel,
        grid=(b, n_tiles),
        in_specs=[
            pl.BlockSpec((1, n, d), lambda bb, r: (bb, 0, COL_UF - col0)),
            pl.BlockSpec((FGROUP_DIM, 2 * FGROUP_DIM), lambda bb, r: (0, 0)),
            pl.BlockSpec((None, rows_ext, n), lambda bb, r: (r, 0, 0)),
            pl.BlockSpec((None, rows_ext, n), lambda bb, r: (r, 0, 0)),
            pl.BlockSpec((tr, rows_ext), lambda bb, r: (0, 0)),
            pl.BlockSpec((1, tr, d), lambda bb, r: (bb, r, COL_ZF - col0)),
            pl.BlockSpec((1, tr, d), lambda bb, r: (bb, 2 * n_tiles - 1 - r, COL_ZF - col0)),
        ],
        out_specs=pl.BlockSpec((1, 1, 2, tr, d), lambda bb, r: (bb, r, 0, 0, 0)),
        out_shape=jax.ShapeDtypeStruct((b, n_tiles, 2, tr, d), BF16),
        scratch_shapes=[pltpu.VMEM((n, d), BF16), pltpu.VMEM((n, d), BF16)],
        compiler_params=_cparams(("parallel", "arbitrary")),
        name="fourier_half",
    )(p, cs_chan, c_t, s_t, rev, p, p)


def _sigmoid(z):
    return 0.5 * jnp.tanh(0.5 * z) + 0.5


def _silu(z):
    h = 0.5 * z
    return h * jnp.tanh(h) + h


def _merge_kernel(ya_ref, yf_ref, xin_ref, bg_ref, cg_ref, zc_ref,
                  ga_ref, gc_ref, gf_ref, xin_p_ref, cg_p_ref, xin_n_ref, cg_n_ref,
                  x_ref, gate_ref, cw_ref, wa_ref, wc_ref, wf_ref, wo_ref, *rest):
    fnw_ref, o_ref = rest if len(rest) == 2 else (None, rest[0])
    i = pl.program_id(1)
    tm = x_ref.shape[1]
    pad = SUBLANES

    def branch(y_gated, w_ref, g_ref):
        t = jnp.dot(y_gated, w_ref[...], preferred_element_type=F32)
        return _sigmoid(g_ref[0].astype(F32)) * t

    t_a = jnp.dot(ya_ref[0], wa_ref[...], preferred_element_type=F32)
    merged_af = (_sigmoid(ga_ref[0].astype(F32)) * t_a
                 + branch(yf_ref[...], wf_ref, gf_ref))
    bits = pltpu.bitcast(t_a[0:SUBLANES, 0:LANES], jnp.uint32)
    zero = pltpu.bitcast((bits >> 16) >> 16, F32)[0:1, 0:1]

    u = cg_ref[0].astype(F32) * xin_ref[0].astype(F32)
    up = cg_p_ref[0, pad - 1:pad, :].astype(F32) * xin_p_ref[0, pad - 1:pad, :].astype(F32)
    un = cg_n_ref[0, 0:1, :].astype(F32) * xin_n_ref[0, 0:1, :].astype(F32)
    up = jnp.where(i == 0, 0.0, up)
    un = jnp.where(i == pl.num_programs(1) - 1, 0.0, un)
    row = lax.broadcasted_iota(jnp.int32, (tm, 1), 0)
    u_prev = jnp.where(row == 0, up, pltpu.roll(u, 1, axis=0))
    u_next = jnp.where(row == tm - 1, un, pltpu.roll(u, tm - 1, axis=0))
    cw = cw_ref[...] + zero
    conv = cw[0:1] * u_prev + cw[1:2] * u + cw[2:3] * u_next
    y_c = bg_ref[0].astype(F32) * conv
    yc_gated = (y_c * _silu(zc_ref[0].astype(F32))).astype(BF16)
    merged = merged_af + branch(yc_gated, wc_ref, gc_ref)
    out = jnp.dot(merged.astype(BF16), wo_ref[...], preferred_element_type=F32)
    x_new = x_ref[0] + gate_ref[0] * out
    if fnw_ref is not None:
        y = x_new * lax.rsqrt(jnp.mean(x_new * x_new, axis=-1, keepdims=True) + NORM_EPS)
        x_new = y * fnw_ref[...]
    o_ref[0] = x_new


def _merge(x, p, col0, ya, yf, yf_spec, mod3, mod_row, conv_w, wa, wc, wf, wo, layer, tm,
           final_norm_w=None):
    bx, sx, d = x.shape
    nb = tm // SUBLANES
    last_halo = sx // SUBLANES - 1

    def whole(b, i):
        return (b, i, 0)

    def col(cb):
        return pl.BlockSpec((1, tm, d), lambda b, i: (b, i, cb - col0))

    def halo_prev(cb):
        return pl.BlockSpec((1, SUBLANES, d),
                            lambda b, i: (b, jnp.maximum(i * nb - 1, 0), cb - col0))

    def halo_next(cb):
        return pl.BlockSpec((1, SUBLANES, d),
                            lambda b, i: (b, jnp.minimum((i + 1) * nb, last_halo), cb - col0))

    def weight():
        return pl.BlockSpec((None, d, d), lambda b, i: (layer, 0, 0), pipeline_mode=pl.Buffered(1))

    in_specs = [
        pl.BlockSpec((1, tm, d), whole), yf_spec,
        col(COL_XIN), col(COL_BG), col(COL_CG), col(COL_ZC),
        col(COL_GL), col(COL_GL + 1), col(COL_GL + 2),
        halo_prev(COL_XIN), halo_prev(COL_CG), halo_next(COL_XIN), halo_next(COL_CG),
        pl.BlockSpec((1, tm, d), whole),
        pl.BlockSpec((1, 1, d), lambda b, i: (mod_row(b), 0, 2)),
        pl.BlockSpec((None, 3, d), lambda b, i: (layer, 0, 0)),
        weight(), weight(), weight(), weight(),
    ]
    args = [ya, yf, p, p, p, p, p, p, p, p, p, p, p, x, mod3, conv_w, wa, wc, wf, wo]
    if final_norm_w is not None:
        in_specs.append(pl.BlockSpec((1, d), lambda b, i: (0, 0)))
        args.append(final_norm_w)
    return pl.pallas_call(
        _merge_kernel,
        grid=(bx, sx // tm),
        in_specs=in_specs,
        out_specs=pl.BlockSpec((1, tm, d), lambda b, i: (b, i, 0)),
        out_shape=jax.ShapeDtypeStruct((bx, sx, d), F32),
        compiler_params=_cparams(("parallel", "parallel")),
        name="merge",
    )(*args)


def _rope_tables(n_tokens):
    axis_dim = HEAD_DIM // 2
    rows = n_tokens // GRID_W
    row = jnp.repeat(jnp.arange(rows), GRID_W).astype(F32)
    col = jnp.tile(jnp.arange(GRID_W), rows).astype(F32)
    inv_freq = ROPE_BASE ** (-jnp.arange(0, axis_dim, 2, dtype=F32) / axis_dim)
    ang_r = row[:, None] * inv_freq
    ang_c = col[:, None] * inv_freq
    ang = jnp.concatenate([ang_r, ang_c] * 4, axis=-1)
    sign = jnp.where(jnp.arange(V_DIM) < V_DIM // 2, -1.0, 1.0).astype(F32)
    cos = jnp.cos(ang)
    sin = jnp.sin(ang) * sign
    return jnp.stack([cos * Q_SCALE, cos]), jnp.stack([sin * Q_SCALE, sin])


def _head_layout(w_qk):
    depth, d, w = w_qk.shape
    t = w_qk.reshape(depth, d, w // V_DIM, 2, 2, 2, HEAD_DIM // 4)
    return t.transpose(0, 1, 2, 5, 3, 4, 6).reshape(depth, d, w)


def _fourier_tables(n):
    cn, sn = _dft_cos_sin(n)
    return jnp.concatenate([cn, -sn], axis=1).astype(BF16)


def _pick_tile(n, target):
    t = min(n, target)
    while n % t:
        t //= 2
    return t


def kernel(x, c, ctx, c_ctx, norm_w, w_mod, b_mod, w_in, lambda_qk, subln_w, conv_w,
           w_attn_o, w_conv_o, w_four_o, w_out, final_norm_w):
    b, s, d = x.shape
    lc = ctx.shape[1]
    depth = w_in.shape[0]
    assert d == N_HEADS * V_DIM and w_in.shape[2] == N_PROJ_BLOCKS * d
    assert s % GRID_W == 0 and s % LANES == 0 and lc % SUBLANES == 0

    pad = (-(b + 1)) % MOD_ROWS_PAD
    cond = jnp.concatenate([c, c_ctx[None, :], jnp.zeros((pad, d), F32)], axis=0)
    mod = _modulation(cond, w_mod, b_mod)
    n_rows = cond.shape[0]

    norm_w = norm_w.reshape(depth, 1, d)
    subln_w = subln_w.reshape(depth, 1, V_DIM)
    n_qk = 2 * d
    w_in_b = w_in.astype(BF16)
    w_in_b = w_in_b.at[:, :, :n_qk].set(_head_layout(w_in_b[:, :, :n_qk]))
    wa_b, wc_b, wf_b, wo_b = (w.astype(BF16) for w in (w_attn_o, w_conv_o, w_four_o, w_out))

    rope_tabs = _rope_tables(s)
    cc, sc = _dft_cos_sin(FGROUP_DIM)
    cs_chan = jnp.concatenate([cc, sc], axis=1).astype(BF16)
    w_pos_ctx = _fourier_tables(lc)

    ctx_flat = ctx.reshape(1, b * lc, d)
    lat_row = lambda bb: bb
    ctx_row = lambda bb: b

    tm_lat = _pick_tile(s, 2048)
    tm_ctx = _pick_tile(b * lc, 2048)
    tmm_lat = _pick_tile(s // 2, 512)
    tmm_ctx = _pick_tile(lc, 256)
    four_tabs = _fourier_half_tables(s, tmm_lat)
    n_ftiles = s // (2 * tmm_lat)
    yf_lat_spec = pl.BlockSpec(
        (None, None, None, tmm_lat, d),
        lambda bb, i: (bb, jnp.where(i < n_ftiles, i, 2 * n_ftiles - 1 - i),
                       jnp.where(i < n_ftiles, 0, 1), 0, 0))
    yf_ctx_spec = pl.BlockSpec((None, tmm_ctx, d), lambda bb, i: (bb, i, 0))

    for l in range(depth):
        last = l == depth - 1
        lam_init = 0.8 - 0.6 * math.exp(-0.3 * l)
        mod3 = mod[l].reshape(n_rows, 1, 3 * d)

        p_qk = _proj(x, mod3, lat_row, norm_w, w_in_b, l, COL_Q, 2, rope_tabs, tm_lat)
        p_lat = _proj(x, mod3, lat_row, norm_w, w_in_b, l, COL_V, N_PROJ_BLOCKS - COL_V, None, tm_lat)
        c0, cn = (COL_K, 2) if last else (COL_Q, N_PROJ_BLOCKS)
        p_ctx = _proj(ctx_flat, mod3, ctx_row, norm_w, w_in_b, l, c0, cn, None, tm_ctx).reshape(b, lc, -1)

        ya = _attention(lambda_qk, subln_w, l, lam_init, (p_qk, COL_Q), (p_qk, COL_K),
                        (p_lat, 0), (p_lat, COL_ZA - COL_V),
                        (p_ctx, COL_K - c0), (p_ctx, COL_V - c0))
        yf = _fourier_half(p_lat, COL_V, cs_chan, four_tabs, tmm_lat)
        x_new = _merge(x, p_lat, COL_V, ya, yf, yf_lat_spec, mod3, lat_row, conv_w, wa_b, wc_b, wf_b,
                       wo_b, l, tmm_lat, final_norm_w.reshape(1, d) if last else None)

        if not last:
            yac = _self_attention(lambda_qk, subln_w, l, lam_init, p_ctx, COL_Q)
            yfc = _fourier(p_ctx, COL_Q, cs_chan, w_pos_ctx, lc)
            ctx3 = ctx_flat.reshape(b, lc, d)
            ctx_flat = _merge(ctx3, p_ctx, COL_Q, yac, yfc, yf_ctx_spec, mod3, ctx_row, conv_w, wa_b, wc_b, wf_b,
                              wo_b, l, tmm_ctx).reshape(1, b * lc, d)
        x = x_new

    return x
```

```python
import functools
import math

import jax
import jax.numpy as jnp
from jax import lax
from jax.experimental import pallas as pl
from jax.experimental.pallas import tpu as pltpu

F32 = jnp.float32
BF16 = jnp.bfloat16

N_HEADS = 8
HEAD_DIM = 64
V_DIM = 2 * HEAD_DIM
FGROUP_DIM = 128
GRID_W = 64
ROPE_BASE = 10000.0
NORM_EPS = 1e-6
SUBLN_EPS = 1e-5
N_PROJ_BLOCKS = 13
COL_Q, COL_K, COL_V, COL_ZA, COL_XIN, COL_BG, COL_CG, COL_ZC, COL_UF, COL_ZF, COL_GL = range(11)

LANES = 128
SUBLANES = 8
VMEM_LIMIT_BYTES = 56 * 1024 * 1024
MOD_ROWS_PAD = 8
ATTN_SUB_ROWS = 256
ATTN_CARRY_BLOCKS = 2
Q_SCALE = math.log2(math.e) / math.sqrt(HEAD_DIM)


def _cparams(sem):
    return pltpu.CompilerParams(dimension_semantics=sem, vmem_limit_bytes=VMEM_LIMIT_BYTES)


def _mod_kernel(cond_ref, w_ref, b_ref, o_ref):
    cond = cond_ref[...]
    a = cond * jax.nn.sigmoid(cond)
    o_ref[...] = jnp.dot(a, w_ref[...], preferred_element_type=F32,
                         precision=lax.Precision.HIGHEST) + b_ref[...]


def _modulation(cond, w_mod, b_mod):
    depth, d, w3 = w_mod.shape
    rows = cond.shape[0]
    tn = d
    return pl.pallas_call(
        _mod_kernel,
        grid=(depth, w3 // tn),
        in_specs=[
            pl.BlockSpec((rows, d), lambda l, j: (0, 0)),
            pl.BlockSpec((None, d, tn), lambda l, j: (l, 0, j)),
            pl.BlockSpec((None, 1, tn), lambda l, j: (l, 0, j)),
        ],
        out_specs=pl.BlockSpec((None, rows, tn), lambda l, j: (l, 0, j)),
        out_shape=jax.ShapeDtypeStruct((depth, rows, w3), F32),
        compiler_params=_cparams(("parallel", "parallel")),
        name="modulation",
    )(cond, w_mod, b_mod.reshape(depth, 1, w3))


def _proj_kernel(x_ref, sh_ref, sc_ref, nw_ref, w_ref, *rest, rope, scale_first):
    if rope:
        cos_ref, sin_ref, o_ref, hout_ref, h_ref = rest
    else:
        o_ref, h_ref = rest
    j = pl.program_id(2)

    @pl.when(j == 0)
    def _():
        x = x_ref[0]
        y = x * lax.rsqrt(jnp.mean(x * x, axis=-1, keepdims=True) + NORM_EPS)
        h = ((y * nw_ref[...]) * (1.0 + sc_ref[0]) + sh_ref[0]).astype(BF16)
        h_ref[...] = h
        if rope:
            hout_ref[0] = h

    acc = jnp.dot(h_ref[...], w_ref[...], preferred_element_type=F32)
    tn = acc.shape[1]

    if rope:
        cos = cos_ref[...]
        sin = sin_ref[...]
        for hh in range(tn // LANES):
            a = acc[:, hh * LANES:(hh + 1) * LANES]
            r = a * cos + pltpu.roll(a, LANES // 2, axis=1) * sin
            o_ref[0, :, hh * LANES:(hh + 1) * LANES] = r.astype(BF16)
    elif scale_first:
        scale = jnp.where(j == 0, Q_SCALE, 1.0).astype(F32)
        o_ref[0] = (acc * scale).astype(BF16)
    else:
        o_ref[0] = acc.astype(BF16)


def _proj(x, mod3, mod_row, norm_w, w_in_b, layer, col0, ncols, rope_tabs, tm):
    bx, sx, d = x.shape
    tn = d
    rope = rope_tabs is not None
    assert not rope or (col0, ncols) == (COL_Q, 2)
    in_specs = [
        pl.BlockSpec((1, tm, d), lambda b, i, j: (b, i, 0)),
        pl.BlockSpec((1, 1, d), lambda b, i, j: (mod_row(b), 0, 0)),
        pl.BlockSpec((1, 1, d), lambda b, i, j: (mod_row(b), 0, 1)),
        pl.BlockSpec((None, 1, d), lambda b, i, j: (layer, 0, 0)),
        pl.BlockSpec((None, d, tn), lambda b, i, j: (layer, 0, col0 + j)),
    ]
    args = [x, mod3, mod3, norm_w, w_in_b]
    if rope:
        cos_t, sin_t = rope_tabs
        tab_spec = pl.BlockSpec((None, tm, LANES), lambda b, i, j: (j, i, 0))
        in_specs += [tab_spec, tab_spec]
        args += [cos_t, sin_t]
    out_specs = pl.BlockSpec((1, tm, tn), lambda b, i, j: (b, i, j))
    out_shape = jax.ShapeDtypeStruct((bx, sx, ncols * tn), BF16)
    if rope:
        out_specs = [out_specs, pl.BlockSpec((1, tm, d), lambda b, i, j: (b, i, 0))]
        out_shape = [out_shape, jax.ShapeDtypeStruct((bx, sx, d), BF16)]
    return pl.pallas_call(
        functools.partial(_proj_kernel, rope=rope, scale_first=(col0 == COL_Q)),
        grid=(bx, sx // tm, ncols),
        in_specs=in_specs,
        out_specs=out_specs,
        out_shape=out_shape,
        scratch_shapes=[pltpu.VMEM((tm, d), BF16)],
        compiler_params=_cparams(("parallel", "parallel", "arbitrary")),
        name="proj_rope" if rope else "proj",
    )(*args)


def _matmul_kernel(h_ref, w_ref, o_ref):
    o_ref[0] = jnp.dot(h_ref[0], w_ref[...], preferred_element_type=F32).astype(BF16)


def _proj_from_normalised(h, w_in_b, layer, col0, ncols, tm):
    bx, sx, d = h.shape
    tn = d
    return pl.pallas_call(
        _matmul_kernel,
        grid=(bx, sx // tm, ncols),
        in_specs=[
            pl.BlockSpec((1, tm, d), lambda b, i, j: (b, i, 0)),
            pl.BlockSpec((None, d, tn), lambda b, i, j: (layer, 0, col0 + j)),
        ],
        out_specs=pl.BlockSpec((1, tm, tn), lambda b, i, j: (b, i, j)),
        out_shape=jax.ShapeDtypeStruct((bx, sx, ncols * tn), BF16),
        compiler_params=_cparams(("parallel", "parallel", "arbitrary")),
        name="proj_rest",
    )(h, w_in_b)


def _attn_kernel(lq_ref, sw_ref, q_ref, k_ref, v_ref, z_ref, kc_ref, vc_ref, qn_ref, kn_ref, kcn_ref,
                 o_ref, kall_ref, vext_ref, kalln_ref, scarry_ref, *, lam_init):
    n_lat = k_ref.shape[1]
    n_all = kall_ref.shape[0]

    def gather_keys(dst_ref, lat_ref, ctx_ref):
        dst_ref[0:n_lat, :] = lat_ref[0]
        dst_ref[n_lat:n_all, :] = ctx_ref[0]

    gather_keys(kall_ref, k_ref, kc_ref)
    vext_ref[0:n_lat, 0:V_DIM] = v_ref[0]
    vext_ref[n_lat:n_all, 0:V_DIM] = vc_ref[0]
    ones_lane = lax.broadcasted_iota(jnp.int32, (n_all, V_DIM), 1) == 0
    vext_ref[:, V_DIM:2 * V_DIM] = jnp.where(ones_lane, 1.0, 0.0).astype(BF16)

    lq = lq_ref[...].astype(F32)
    lam = (jnp.exp(jnp.sum(lq[0:1] * lq[1:2], axis=-1, keepdims=True))
           - jnp.exp(jnp.sum(lq[2:3] * lq[3:4], axis=-1, keepdims=True)) + lam_init)

    tq = q_ref.shape[1]
    sub = min(tq, ATTN_SUB_ROWS)
    lane = lax.broadcasted_iota(jnp.int32, (sub, V_DIM), 1)

    def scores(q, keys_ref):
        zero = jnp.zeros_like(q)
        map1 = (lane % HEAD_DIM) < HEAD_DIM // 2
        qq = jnp.concatenate([jnp.where(map1, q, zero), jnp.where(map1, zero, q)], axis=0)
        return lax.dot_general(qq, keys_ref[...], (((1,), (1,)), ((), ())),
                               preferred_element_type=F32)

    def finish(j, get_s):
        m = jnp.max(get_s(), axis=-1, keepdims=True)
        p = jnp.exp2(get_s() - m).astype(BF16)
        nd = jnp.dot(p, vext_ref[...], preferred_element_type=F32)
        o = nd[:, 0:V_DIM] / nd[:, V_DIM:V_DIM + 1]
        o = o[:sub] - lam * o[sub:]
        y = o * lax.rsqrt(jnp.mean(o * o, axis=-1, keepdims=True) + SUBLN_EPS)
        y = (y * sw_ref[...]) * (1.0 - lam_init)
        z = z_ref[0, j * sub:(j + 1) * sub, :].astype(F32)
        o_ref[0, j * sub:(j + 1) * sub, :] = (y * _silu(z)).astype(BF16)

    def value(v):
        return lambda: v

    def rows(ref, j):
        return ref[0, j * sub:(j + 1) * sub, :]

    n_sub = tq // sub
    depth = scarry_ref.shape[0]

    @pl.when((pl.program_id(0) == 0) & (pl.program_id(1) == 0))
    def _():
        for d in range(depth):
            scarry_ref[d] = scores(rows(q_ref, d), kall_ref)

    gather_keys(kalln_ref, kn_ref, kcn_ref)
    pending = [(lambda d=d: scarry_ref[d]) for d in range(depth)]

    for j in range(n_sub):
        if j + depth < n_sub:
            pending.append(value(scores(rows(q_ref, j + depth), kall_ref)))
        else:
            pending.append(value(scores(rows(qn_ref, j + depth - n_sub), kalln_ref)))
        finish(j, pending.pop(0))
    for d in range(depth):
        scarry_ref[d] = pending[d]()


def _self_attn_kernel(lq_ref, sw_ref, q_ref, k_ref, v_ref, z_ref, o_ref, *, lam_init):
    n = q_ref.shape[1]
    lq = lq_ref[...].astype(F32)
    lam = (jnp.exp(jnp.sum(lq[0:1] * lq[1:2], axis=-1, keepdims=True))
           - jnp.exp(jnp.sum(lq[2:3] * lq[3:4], axis=-1, keepdims=True)) + lam_init)
    lane = lax.broadcasted_iota(jnp.int32, (n, V_DIM), 1)
    map1 = (lane % HEAD_DIM) < HEAD_DIM // 2
    ones_cols = jnp.where(lane == 0, 1.0, 0.0).astype(BF16)
    heads = [slice(h * V_DIM, (h + 1) * V_DIM) for h in range(q_ref.shape[2] // V_DIM)]

    def scores(cols):
        q = q_ref[0, :, cols]
        zero = jnp.zeros_like(q)
        qq = jnp.concatenate([jnp.where(map1, q, zero), jnp.where(map1, zero, q)], axis=0)
        return lax.dot_general(qq, k_ref[0, :, cols], (((1,), (1,)), ((), ())),
                               preferred_element_type=F32)

    s_all = [scores(cols) for cols in heads]
    p_all = [jnp.exp2(s - jnp.max(s, axis=-1, keepdims=True)).astype(BF16) for s in s_all]
    nd_all = [jnp.dot(p, jnp.concatenate([v_ref[0, :, cols], ones_cols], axis=1),
                      preferred_element_type=F32) for p, cols in zip(p_all, heads)]
    for nd, cols in zip(nd_all, heads):
        o = nd[:, 0:V_DIM] / nd[:, V_DIM:V_DIM + 1]
        o = o[:n] - lam * o[n:]
        y = o * lax.rsqrt(jnp.mean(o * o, axis=-1, keepdims=True) + SUBLN_EPS)
        y = (y * sw_ref[...]) * (1.0 - lam_init)
        o_ref[0, :, cols] = (y * _silu(z_ref[0, :, cols].astype(F32))).astype(BF16)


def _self_attention(lambda_qk, subln_w, layer, lam_init, p, col0):
    b, n, _ = p.shape
    d = N_HEADS * V_DIM

    def col(cb):
        return pl.BlockSpec((1, n, d), lambda bb: (bb, 0, cb - col0))

    return pl.pallas_call(
        functools.partial(_self_attn_kernel, lam_init=lam_init),
        grid=(b,),
        in_specs=[
            pl.BlockSpec((None, 4, HEAD_DIM), lambda bb: (layer, 0, 0)),
            pl.BlockSpec((None, 1, V_DIM), lambda bb: (layer, 0, 0)),
            col(COL_Q), col(COL_K), col(COL_V), col(COL_ZA),
        ],
        out_specs=pl.BlockSpec((1, n, d), lambda bb: (bb, 0, 0)),
        out_shape=jax.ShapeDtypeStruct((b, n, d), BF16),
        compiler_params=_cparams(("parallel",)),
        name="attn_self",
    )(lambda_qk, subln_w, p, p, p, p)


def _attention(lambda_qk, subln_w, layer, lam_init, q, k, v, z, kc, vc):
    b, s, _ = q[0].shape
    lc = kc[0].shape[1]
    n_keys = s + lc
    nh = N_HEADS
    sub = min(s, ATTN_SUB_ROWS)
    depth = min(ATTN_CARRY_BLOCKS, s // sub)

    def head_cols(rows, col):
        return pl.BlockSpec((1, rows, V_DIM), lambda bb, h: (bb, 0, col * nh + h))

    def next_head_cols(rows, col):
        def index(bb, h):
            flat = jnp.minimum(bb * nh + h + 1, b * nh - 1)
            return (flat // nh, 0, col * nh + flat % nh)
        return pl.BlockSpec((1, rows, V_DIM), index)

    in_specs = [
        pl.BlockSpec((None, 4, HEAD_DIM), lambda bb, h: (layer, 0, 0)),
        pl.BlockSpec((None, 1, V_DIM), lambda bb, h: (layer, 0, 0)),
        head_cols(s, q[1]), head_cols(s, k[1]), head_cols(s, v[1]), head_cols(s, z[1]),
        head_cols(lc, kc[1]), head_cols(lc, vc[1]),
        next_head_cols(depth * sub, q[1]), next_head_cols(s, k[1]), next_head_cols(lc, kc[1]),
    ]
    return pl.pallas_call(
        functools.partial(_attn_kernel, lam_init=lam_init),
        grid=(b, nh),
        in_specs=in_specs,
        out_specs=head_cols(s, 0),
        out_shape=jax.ShapeDtypeStruct((b, s, nh * V_DIM), BF16),
        scratch_shapes=[
            pltpu.VMEM((n_keys, V_DIM), BF16),
            pltpu.VMEM((n_keys, 2 * V_DIM), BF16),
            pltpu.VMEM((n_keys, V_DIM), BF16),
            pltpu.VMEM((depth, 2 * sub, n_keys), F32),
        ],
        compiler_params=_cparams(("arbitrary", "arbitrary")),
        name="attn_latent",
    )(lambda_qk, subln_w, q[0], k[0], v[0], z[0], kc[0], vc[0], q[0], k[0], kc[0])


def _fourier_kernel(u_ref, cs_ref, w_ref, z_ref, o_ref, ab_ref):
    r = pl.program_id(1)
    n = u_ref.shape[1]

    @pl.when(r == 0)
    def _():
        for g in range(u_ref.shape[2] // FGROUP_DIM):
            cols = slice(g * FGROUP_DIM, (g + 1) * FGROUP_DIM)
            ab = jnp.dot(u_ref[0, :, cols], cs_ref[...], preferred_element_type=F32)
            ab_ref[0:n, cols] = ab[:, :FGROUP_DIM].astype(BF16)
            ab_ref[n:2 * n, cols] = ab[:, FGROUP_DIM:].astype(BF16)

    y = jnp.dot(w_ref[...], ab_ref[...], preferred_element_type=F32)
    o_ref[0] = (y * _silu(z_ref[0].astype(F32))).astype(BF16)


def _dft_cos_sin(n):
    k = jnp.arange(n, dtype=jnp.int32)
    ang = ((k[:, None] * k[None, :]) % n).astype(F32) * (2.0 * math.pi / n)
    scale = 1.0 / math.sqrt(n)
    return jnp.cos(ang) * scale, jnp.sin(ang) * scale


def _fourier(p, col0, cs_chan, w_pos, tr):
    b, n, _ = p.shape
    d = N_HEADS * V_DIM
    return pl.pallas_call(
        _fourier_kernel,
        grid=(b, n // tr),
        in_specs=[
            pl.BlockSpec((1, n, d), lambda bb, r: (bb, 0, COL_UF - col0)),
            pl.BlockSpec((FGROUP_DIM, 2 * FGROUP_DIM), lambda bb, r: (0, 0)),
            pl.BlockSpec((tr, 2 * n), lambda bb, r: (r, 0)),
            pl.BlockSpec((1, tr, d), lambda bb, r: (bb, r, COL_ZF - col0)),
        ],
        out_specs=pl.BlockSpec((1, tr, d), lambda bb, r: (bb, r, 0)),
        out_shape=jax.ShapeDtypeStruct((b, n, d), BF16),
        scratch_shapes=[pltpu.VMEM((2 * n, d), BF16)],
        compiler_params=_cparams(("parallel", "arbitrary")),
        name="fourier",
    )(p, cs_chan, w_pos, p)


def _fourier_half_kernel(u_ref, cs_ref, c_ref, s_ref, rev_ref, zlo_ref, zhi_ref, o_ref, a_ref, b_ref):
    r = pl.program_id(1)
    tr = o_ref.shape[3]

    @pl.when(r == 0)
    def _():
        for g in range(u_ref.shape[2] // FGROUP_DIM):
            cols = slice(g * FGROUP_DIM, (g + 1) * FGROUP_DIM)
            ab = jnp.dot(u_ref[0, :, cols], cs_ref[...], preferred_element_type=F32)
            a_ref[:, cols] = ab[:, :FGROUP_DIM].astype(BF16)
            b_ref[:, cols] = ab[:, FGROUP_DIM:].astype(BF16)

    p = jnp.dot(c_ref[...], a_ref[...], preferred_element_type=F32)
    q = jnp.dot(s_ref[...], b_ref[...], preferred_element_type=F32)
    o_ref[0, 0, 0] = ((p[:tr] - q[:tr]) * _silu(zlo_ref[0].astype(F32))).astype(BF16)
    hi = jnp.dot(rev_ref[...], (p + q).astype(BF16), preferred_element_type=F32)
    o_ref[0, 0, 1] = (hi * _silu(zhi_ref[0].astype(F32))).astype(BF16)


def _fourier_half_tables(n, tr):
    ext = 2 * SUBLANES
    n_tiles = n // (2 * tr)
    rows = (jnp.arange(n_tiles, dtype=jnp.int32)[:, None] * tr
            + jnp.arange(tr + ext, dtype=jnp.int32)[None, :])
    k = jnp.arange(n, dtype=jnp.int32)
    ang = ((rows[:, :, None] * k[None, None, :]) % n).astype(F32) * (2.0 * math.pi / n)
    scale = 1.0 / math.sqrt(n)
    i = jnp.arange(tr, dtype=jnp.int32)[:, None]
    j = jnp.arange(tr + ext, dtype=jnp.int32)[None, :]
    rev = (j == tr - i).astype(BF16)
    return (jnp.cos(ang) * scale).astype(BF16), (jnp.sin(ang) * scale).astype(BF16), rev


def _fourier_half(p, col0, cs_chan, tabs, tr):
    b, n, _ = p.shape
    d = N_HEADS * V_DIM
    c_t, s_t, rev = tabs
    n_tiles, rows_ext, _ = c_t.shape
    return pl.pallas_call(
        _fourier_half_kernel,
        grid=(b, n_tiles),
        in_specs=[
            pl.BlockSpec((1, n, d), lambda bb, r: (bb, 0, COL_UF - col0)),
            pl.BlockSpec((FGROUP_DIM, 2 * FGROUP_DIM), lambda bb, r: (0, 0)),
            pl.BlockSpec((None, rows_ext, n), lambda bb, r: (r, 0, 0)),
            pl.BlockSpec((None, rows_ext, n), lambda bb, r: (r, 0, 0)),
            pl.BlockSpec((tr, rows_ext), lambda bb, r: (0, 0)),
            pl.BlockSpec((1, tr, d), lambda bb, r: (bb, r, COL_ZF - col0)),
            pl.BlockSpec((1, tr, d), lambda bb, r: (bb, 2 * n_tiles - 1 - r, COL_ZF - col0)),
        ],
        out_specs=pl.BlockSpec((1, 1, 2, tr, d), lambda bb, r: (bb, r, 0, 0, 0)),
        out_shape=jax.ShapeDtypeStruct((b, n_tiles, 2, tr, d), BF16),
        scratch_shapes=[pltpu.VMEM((n, d), BF16), pltpu.VMEM((n, d), BF16)],
        compiler_params=_cparams(("parallel", "arbitrary")),
        name="fourier_half",
    )(p, cs_chan, c_t, s_t, rev, p, p)


def _sigmoid(z):
    return 0.5 * jnp.tanh(0.5 * z) + 0.5


def _silu(z):
    h = 0.5 * z
    return h * jnp.tanh(h) + h


def _merge_kernel(ya_ref, yf_ref, xin_ref, bg_ref, cg_ref, zc_ref,
                  ga_ref, gc_ref, gf_ref, xin_p_ref, cg_p_ref, xin_n_ref, cg_n_ref,
                  x_ref, gate_ref, cw_ref, wa_ref, wc_ref, wf_ref, wo_ref, *rest):
    fnw_ref, o_ref = rest if len(rest) == 2 else (None, rest[0])
    i = pl.program_id(1)
    tm = x_ref.shape[1]
    pad = SUBLANES

    def branch(y_gated, w_ref, g_ref):
        t = jnp.dot(y_gated, w_ref[...], preferred_element_type=F32)
        return _sigmoid(g_ref[0].astype(F32)) * t

    t_a = jnp.dot(ya_ref[0], wa_ref[...], preferred_element_type=F32)
    merged_af = (_sigmoid(ga_ref[0].astype(F32)) * t_a
                 + branch(yf_ref[...], wf_ref, gf_ref))
    bits = pltpu.bitcast(t_a[0:SUBLANES, 0:LANES], jnp.uint32)
    zero = pltpu.bitcast((bits >> 16) >> 16, F32)[0:1, 0:1]

    u = cg_ref[0].astype(F32) * xin_ref[0].astype(F32)
    up = cg_p_ref[0, pad - 1:pad, :].astype(F32) * xin_p_ref[0, pad - 1:pad, :].astype(F32)
    un = cg_n_ref[0, 0:1, :].astype(F32) * xin_n_ref[0, 0:1, :].astype(F32)
    up = jnp.where(i == 0, 0.0, up)
    un = jnp.where(i == pl.num_programs(1) - 1, 0.0, un)
    row = lax.broadcasted_iota(jnp.int32, (tm, 1), 0)
    u_prev = jnp.where(row == 0, up, pltpu.roll(u, 1, axis=0))
    u_next = jnp.where(row == tm - 1, un, pltpu.roll(u, tm - 1, axis=0))
    cw = cw_ref[...] + zero
    conv = cw[0:1] * u_prev + cw[1:2] * u + cw[2:3] * u_next
    y_c = bg_ref[0].astype(F32) * conv
    yc_gated = (y_c * _silu(zc_ref[0].astype(F32))).astype(BF16)
    merged = merged_af + branch(yc_gated, wc_ref, gc_ref)
    out = jnp.dot(merged.astype(BF16), wo_ref[...], preferred_element_type=F32)
    x_new = x_ref[0] + gate_ref[0] * out
    if fnw_ref is not None:
        y = x_new * lax.rsqrt(jnp.mean(x_new * x_new, axis=-1, keepdims=True) + NORM_EPS)
        x_new = y * fnw_ref[...]
    o_ref[0] = x_new


def _merge(x, p, col0, ya, yf, yf_spec, mod3, mod_row, conv_w, wa, wc, wf, wo, layer, tm,
           final_norm_w=None):
    bx, sx, d = x.shape
    nb = tm // SUBLANES
    last_halo = sx // SUBLANES - 1

    def whole(b, i):
        return (b, i, 0)

    def col(cb):
        return pl.BlockSpec((1, tm, d), lambda b, i: (b, i, cb - col0))

    def halo_prev(cb):
        return pl.BlockSpec((1, SUBLANES, d),
                            lambda b, i: (b, jnp.maximum(i * nb - 1, 0), cb - col0))

    def halo_next(cb):
        return pl.BlockSpec((1, SUBLANES, d),
                            lambda b, i: (b, jnp.minimum((i + 1) * nb, last_halo), cb - col0))

    def weight():
        return pl.BlockSpec((None, d, d), lambda b, i: (layer, 0, 0), pipeline_mode=pl.Buffered(1))

    in_specs = [
        pl.BlockSpec((1, tm, d), whole), yf_spec,
        col(COL_XIN), col(COL_BG), col(COL_CG), col(COL_ZC),
        col(COL_GL), col(COL_GL + 1), col(COL_GL + 2),
        halo_prev(COL_XIN), halo_prev(COL_CG), halo_next(COL_XIN), halo_next(COL_CG),
        pl.BlockSpec((1, tm, d), whole),
        pl.BlockSpec((1, 1, d), lambda b, i: (mod_row(b), 0, 2)),
        pl.BlockSpec((None, 3, d), lambda b, i: (layer, 0, 0)),
        weight(), weight(), weight(), weight(),
    ]
    args = [ya, yf, p, p, p, p, p, p, p, p, p, p, p, x, mod3, conv_w, wa, wc, wf, wo]
    if final_norm_w is not None:
        in_specs.append(pl.BlockSpec((1, d), lambda b, i: (0, 0)))
        args.append(final_norm_w)
    return pl.pallas_call(
        _merge_kernel,
        grid=(bx, sx // tm),
        in_specs=in_specs,
        out_specs=pl.BlockSpec((1, tm, d), lambda b, i: (b, i, 0)),
        out_shape=jax.ShapeDtypeStruct((bx, sx, d), F32),
        compiler_params=_cparams(("parallel", "parallel")),
        name="merge",
    )(*args)


def _rope_tables(n_tokens):
    axis_dim = HEAD_DIM // 2
    rows = n_tokens // GRID_W
    row = jnp.repeat(jnp.arange(rows), GRID_W).astype(F32)
    col = jnp.tile(jnp.arange(GRID_W), rows).astype(F32)
    inv_freq = ROPE_BASE ** (-jnp.arange(0, axis_dim, 2, dtype=F32) / axis_dim)
    ang_r = row[:, None] * inv_freq
    ang_c = col[:, None] * inv_freq
    ang = jnp.concatenate([ang_r, ang_c] * 4, axis=-1)
    sign = jnp.where(jnp.arange(V_DIM) < V_DIM // 2, -1.0, 1.0).astype(F32)
    cos = jnp.cos(ang)
    sin = jnp.sin(ang) * sign
    return jnp.stack([cos * Q_SCALE, cos]), jnp.stack([sin * Q_SCALE, sin])


def _head_layout(w_qk):
    depth, d, w = w_qk.shape
    t = w_qk.reshape(depth, d, w // V_DIM, 2, 2, 2, HEAD_DIM // 4)
    return t.transpose(0, 1, 2, 5, 3, 4, 6).reshape(depth, d, w)


def _fourier_tables(n):
    cn, sn = _dft_cos_sin(n)
    return jnp.concatenate([cn, -sn], axis=1).astype(BF16)


def _pick_tile(n, target):
    t = min(n, target)
    while n % t:
        t //= 2
    return t


def kernel(x, c, ctx, c_ctx, norm_w, w_mod, b_mod, w_in, lambda_qk, subln_w, conv_w,
           w_attn_o, w_conv_o, w_four_o, w_out, final_norm_w):
    b, s, d = x.shape
    lc = ctx.shape[1]
    depth = w_in.shape[0]
    assert d == N_HEADS * V_DIM and w_in.shape[2] == N_PROJ_BLOCKS * d
    assert s % GRID_W == 0 and s % LANES == 0 and lc % SUBLANES == 0

    pad = (-(b + 1)) % MOD_ROWS_PAD
    cond = jnp.concatenate([c, c_ctx[None, :], jnp.zeros((pad, d), F32)], axis=0)
    mod = _modulation(cond, w_mod, b_mod)
    n_rows = cond.shape[0]

    norm_w = norm_w.reshape(depth, 1, d)
    subln_w = subln_w.reshape(depth, 1, V_DIM)
    n_qk = 2 * d
    w_in_b = w_in.astype(BF16)
    w_in_b = w_in_b.at[:, :, :n_qk].set(_head_layout(w_in_b[:, :, :n_qk]))
    wa_b, wc_b, wf_b, wo_b = (w.astype(BF16) for w in (w_attn_o, w_conv_o, w_four_o, w_out))

    rope_tabs = _rope_tables(s)
    cc, sc = _dft_cos_sin(FGROUP_DIM)
    cs_chan = jnp.concatenate([cc, sc], axis=1).astype(BF16)
    w_pos_ctx = _fourier_tables(lc)

    ctx_flat = ctx.reshape(1, b * lc, d)
    lat_row = lambda bb: bb
    ctx_row = lambda bb: b

    tm_lat = _pick_tile(s, 2048)
    tm_ctx = _pick_tile(b * lc, 2048)
    tmm_lat = _pick_tile(s // 2, 512)
    tmm_ctx = _pick_tile(lc, 256)
    four_tabs = _fourier_half_tables(s, tmm_lat)
    n_ftiles = s // (2 * tmm_lat)
    yf_lat_spec = pl.BlockSpec(
        (None, None, None, tmm_lat, d),
        lambda bb, i: (bb, jnp.where(i < n_ftiles, i, 2 * n_ftiles - 1 - i),
                       jnp.where(i < n_ftiles, 0, 1), 0, 0))
    yf_ctx_spec = pl.BlockSpec((None, tmm_ctx, d), lambda bb, i: (bb, i, 0))

    for l in range(depth):
        last = l == depth - 1
        lam_init = 0.8 - 0.6 * math.exp(-0.3 * l)
        mod3 = mod[l].reshape(n_rows, 1, 3 * d)

        p_qk, h_lat = _proj(x, mod3, lat_row, norm_w, w_in_b, l, COL_Q, 2, rope_tabs, tm_lat)
        p_lat = _proj_from_normalised(h_lat, w_in_b, l, COL_V, N_PROJ_BLOCKS - COL_V, tm_lat)
        c0, cn = (COL_K, 2) if last else (COL_Q, N_PROJ_BLOCKS)
        p_ctx = _proj(ctx_flat, mod3, ctx_row, norm_w, w_in_b, l, c0, cn, None, tm_ctx).reshape(b, lc, -1)

        ya = _attention(lambda_qk, subln_w, l, lam_init, (p_qk, COL_Q), (p_qk, COL_K),
                        (p_lat, 0), (p_lat, COL_ZA - COL_V),
                        (p_ctx, COL_K - c0), (p_ctx, COL_V - c0))
        yf = _fourier_half(p_lat, COL_V, cs_chan, four_tabs, tmm_lat)
        x_new = _merge(x, p_lat, COL_V, ya, yf, yf_lat_spec, mod3, lat_row, conv_w, wa_b, wc_b, wf_b,
                       wo_b, l, tmm_lat, final_norm_w.reshape(1, d) if last else None)

        if not last:
            yac = _self_attention(lambda_qk, subln_w, l, lam_init, p_ctx, COL_Q)
            yfc = _fourier(p_ctx, COL_Q, cs_chan, w_pos_ctx, lc)
            ctx3 = ctx_flat.reshape(b, lc, d)
            ctx_flat = _merge(ctx3, p_ctx, COL_Q, yac, yfc, yf_ctx_spec, mod3, ctx_row, conv_w, wa_b, wc_b, wf_b,
                              wo_b, l, tmm_ctx).reshape(1, b * lc, d)
        x = x_new

    return x
```

```python
import functools
import math

import jax
import jax.numpy as jnp
from jax import lax
from jax.experimental import pallas as pl
from jax.experimental.pallas import tpu as pltpu

F32 = jnp.float32
BF16 = jnp.bfloat16

N_HEADS = 8
HEAD_DIM = 64
V_DIM = 2 * HEAD_DIM
FGROUP_DIM = 128
GRID_W = 64
ROPE_BASE = 10000.0
NORM_EPS = 1e-6
SUBLN_EPS = 1e-5
N_PROJ_BLOCKS = 13
COL_Q, COL_K, COL_V, COL_ZA, COL_XIN, COL_BG, COL_CG, COL_ZC, COL_UF, COL_ZF, COL_GL = range(11)

LANES = 128
SUBLANES = 8
VMEM_LIMIT_BYTES = 56 * 1024 * 1024
MOD_ROWS_PAD = 8
ATTN_SUB_ROWS = 256
ATTN_CARRY_BLOCKS = 2
Q_SCALE = math.log2(math.e) / math.sqrt(HEAD_DIM)


def _cparams(sem):
    return pltpu.CompilerParams(dimension_semantics=sem, vmem_limit_bytes=VMEM_LIMIT_BYTES)


def _mod_kernel(cond_ref, w_ref, b_ref, o_ref):
    cond = cond_ref[...]
    a = cond * jax.nn.sigmoid(cond)
    o_ref[...] = jnp.dot(a, w_ref[...], preferred_element_type=F32,
                         precision=lax.Precision.HIGHEST) + b_ref[...]


def _modulation(cond, w_mod, b_mod):
    depth, d, w3 = w_mod.shape
    rows = cond.shape[0]
    tn = d
    return pl.pallas_call(
        _mod_kernel,
        grid=(depth, w3 // tn),
        in_specs=[
            pl.BlockSpec((rows, d), lambda l, j: (0, 0)),
            pl.BlockSpec((None, d, tn), lambda l, j: (l, 0, j)),
            pl.BlockSpec((None, 1, tn), lambda l, j: (l, 0, j)),
        ],
        out_specs=pl.BlockSpec((None, rows, tn), lambda l, j: (l, 0, j)),
        out_shape=jax.ShapeDtypeStruct((depth, rows, w3), F32),
        compiler_params=_cparams(("parallel", "parallel")),
        name="modulation",
    )(cond, w_mod, b_mod.reshape(depth, 1, w3))


def _store_block(acc, o_ref, j, rope_refs, scale_first):
    if rope_refs is not None:
        cos = rope_refs[0][...]
        sin = rope_refs[1][...]
        for hh in range(acc.shape[1] // LANES):
            a = acc[:, hh * LANES:(hh + 1) * LANES]
            r = a * cos + pltpu.roll(a, LANES // 2, axis=1) * sin
            o_ref[0, :, hh * LANES:(hh + 1) * LANES] = r.astype(BF16)
    elif scale_first:
        scale = jnp.where(j == 0, Q_SCALE, 1.0).astype(F32)
        o_ref[0] = (acc * scale).astype(BF16)
    else:
        o_ref[0] = acc.astype(BF16)


def _proj_kernel(x_ref, sh_ref, sc_ref, nw_ref, w_ref, o_ref, *rest, scale_first):
    h_ref = rest[-1]
    j = pl.program_id(2)

    @pl.when(j == 0)
    def _():
        x = x_ref[0]
        y = x * lax.rsqrt(jnp.mean(x * x, axis=-1, keepdims=True) + NORM_EPS)
        h = ((y * nw_ref[...]) * (1.0 + sc_ref[0]) + sh_ref[0]).astype(BF16)
        h_ref[...] = h
        if len(rest) == 2:
            rest[0][0] = h

    acc = jnp.dot(h_ref[...], w_ref[...], preferred_element_type=F32)
    _store_block(acc, o_ref, j, None, scale_first)


def _proj(x, mod3, mod_row, norm_w, w_in_b, layer, col0, ncols, tm, emit_h=False):
    bx, sx, d = x.shape
    tn = d
    in_specs = [
        pl.BlockSpec((1, tm, d), lambda b, i, j: (b, i, 0)),
        pl.BlockSpec((1, 1, d), lambda b, i, j: (mod_row(b), 0, 0)),
        pl.BlockSpec((1, 1, d), lambda b, i, j: (mod_row(b), 0, 1)),
        pl.BlockSpec((None, 1, d), lambda b, i, j: (layer, 0, 0)),
        pl.BlockSpec((None, d, tn), lambda b, i, j: (layer, 0, col0 + j)),
    ]
    out_specs = pl.BlockSpec((1, tm, tn), lambda b, i, j: (b, i, j))
    out_shape = jax.ShapeDtypeStruct((bx, sx, ncols * tn), BF16)
    if emit_h:
        out_specs = [out_specs, pl.BlockSpec((1, tm, d), lambda b, i, j: (b, i, 0))]
        out_shape = [out_shape, jax.ShapeDtypeStruct((bx, sx, d), BF16)]
    return pl.pallas_call(
        functools.partial(_proj_kernel, scale_first=(col0 == COL_Q)),
        grid=(bx, sx // tm, ncols),
        in_specs=in_specs,
        out_specs=out_specs,
        out_shape=out_shape,
        scratch_shapes=[pltpu.VMEM((tm, d), BF16)],
        compiler_params=_cparams(("parallel", "parallel", "arbitrary")),
        name="proj",
    )(x, mod3, mod3, norm_w, w_in_b)


def _proj_qk_kernel(h_ref, w_ref, cos_ref, sin_ref, o_ref):
    acc = jnp.dot(h_ref[0], w_ref[...], preferred_element_type=F32)
    _store_block(acc, o_ref, pl.program_id(2), (cos_ref, sin_ref), False)


def _proj_qk(h, w_in_b, layer, rope_tabs, tm):
    bx, sx, d = h.shape
    tn = d
    cos_t, sin_t = rope_tabs
    tab_spec = pl.BlockSpec((None, tm, LANES), lambda b, i, j: (j, i, 0))
    return pl.pallas_call(
        _proj_qk_kernel,
        grid=(bx, sx // tm, 2),
        in_specs=[
            pl.BlockSpec((1, tm, d), lambda b, i, j: (b, i, 0)),
            pl.BlockSpec((None, d, tn), lambda b, i, j: (layer, 0, COL_Q + j)),
            tab_spec, tab_spec,
        ],
        out_specs=pl.BlockSpec((1, tm, tn), lambda b, i, j: (b, i, j)),
        out_shape=jax.ShapeDtypeStruct((bx, sx, 2 * tn), BF16),
        compiler_params=_cparams(("parallel", "parallel", "arbitrary")),
        name="proj_rope",
    )(h, w_in_b, cos_t, sin_t)


def _attn_kernel(lq_ref, sw_ref, q_ref, k_ref, v_ref, z_ref, kc_ref, vc_ref, qn_ref, kn_ref, kcn_ref,
                 o_ref, kall_ref, vext_ref, kalln_ref, scarry_ref, *, lam_init):
    n_lat = k_ref.shape[1]
    n_all = kall_ref.shape[0]

    def gather_keys(dst_ref, lat_ref, ctx_ref):
        dst_ref[0:n_lat, :] = lat_ref[0]
        dst_ref[n_lat:n_all, :] = ctx_ref[0]

    gather_keys(kall_ref, k_ref, kc_ref)
    vext_ref[0:n_lat, 0:V_DIM] = v_ref[0]
    vext_ref[n_lat:n_all, 0:V_DIM] = vc_ref[0]
    ones_lane = lax.broadcasted_iota(jnp.int32, (n_all, V_DIM), 1) == 0
    vext_ref[:, V_DIM:2 * V_DIM] = jnp.where(ones_lane, 1.0, 0.0).astype(BF16)

    lq = lq_ref[...].astype(F32)
    lam = (jnp.exp(jnp.sum(lq[0:1] * lq[1:2], axis=-1, keepdims=True))
           - jnp.exp(jnp.sum(lq[2:3] * lq[3:4], axis=-1, keepdims=True)) + lam_init)

    tq = q_ref.shape[1]
    sub = min(tq, ATTN_SUB_ROWS)
    lane = lax.broadcasted_iota(jnp.int32, (sub, V_DIM), 1)

    def scores(q, keys_ref):
        zero = jnp.zeros_like(q)
        map1 = (lane % HEAD_DIM) < HEAD_DIM // 2
        qq = jnp.concatenate([jnp.where(map1, q, zero), jnp.where(map1, zero, q)], axis=0)
        return lax.dot_general(qq, keys_ref[...], (((1,), (1,)), ((), ())),
                               preferred_element_type=F32)

    def finish(j, get_s):
        m = jnp.max(get_s(), axis=-1, keepdims=True)
        p = jnp.exp2(get_s() - m).astype(BF16)
        nd = jnp.dot(p, vext_ref[...], preferred_element_type=F32)
        o = nd[:, 0:V_DIM] / nd[:, V_DIM:V_DIM + 1]
        o = o[:sub] - lam * o[sub:]
        y = o * lax.rsqrt(jnp.mean(o * o, axis=-1, keepdims=True) + SUBLN_EPS)
        y = (y * sw_ref[...]) * (1.0 - lam_init)
        z = z_ref[0, j * sub:(j + 1) * sub, :].astype(F32)
        o_ref[0, j * sub:(j + 1) * sub, :] = (y * _silu(z)).astype(BF16)

    def value(v):
        return lambda: v

    def rows(ref, j):
        return ref[0, j * sub:(j + 1) * sub, :]

    n_sub = tq // sub
    depth = scarry_ref.shape[0]

    @pl.when((pl.program_id(0) == 0) & (pl.program_id(1) == 0))
    def _():
        for d in range(depth):
            scarry_ref[d] = scores(rows(q_ref, d), kall_ref)

    gather_keys(kalln_ref, kn_ref, kcn_ref)
    pending = [(lambda d=d: scarry_ref[d]) for d in range(depth)]

    for j in range(n_sub):
        if j + depth < n_sub:
            pending.append(value(scores(rows(q_ref, j + depth), kall_ref)))
        else:
            pending.append(value(scores(rows(qn_ref, j + depth - n_sub), kalln_ref)))
        finish(j, pending.pop(0))
    for d in range(depth):
        scarry_ref[d] = pending[d]()


def _self_attn_kernel(lq_ref, sw_ref, q_ref, k_ref, v_ref, z_ref, o_ref, *, lam_init):
    n = q_ref.shape[1]
    lq = lq_ref[...].astype(F32)
    lam = (jnp.exp(jnp.sum(lq[0:1] * lq[1:2], axis=-1, keepdims=True))
           - jnp.exp(jnp.sum(lq[2:3] * lq[3:4], axis=-1, keepdims=True)) + lam_init)
    lane = lax.broadcasted_iota(jnp.int32, (n, V_DIM), 1)
    map1 = (lane % HEAD_DIM) < HEAD_DIM // 2
    ones_cols = jnp.where(lane == 0, 1.0, 0.0).astype(BF16)
    heads = [slice(h * V_DIM, (h + 1) * V_DIM) for h in range(q_ref.shape[2] // V_DIM)]

    def scores(cols):
        q = q_ref[0, :, cols]
        zero = jnp.zeros_like(q)
        qq = jnp.concatenate([jnp.where(map1, q, zero), jnp.where(map1, zero, q)], axis=0)
        return lax.dot_general(qq, k_ref[0, :, cols], (((1,), (1,)), ((), ())),
                               preferred_element_type=F32)

    s_all = [scores(cols) for cols in heads]
    p_all = [jnp.exp2(s - jnp.max(s, axis=-1, keepdims=True)).astype(BF16) for s in s_all]
    nd_all = [jnp.dot(p, jnp.concatenate([v_ref[0, :, cols], ones_cols], axis=1),
                      preferred_element_type=F32) for p, cols in zip(p_all, heads)]
    for nd, cols in zip(nd_all, heads):
        o = nd[:, 0:V_DIM] / nd[:, V_DIM:V_DIM + 1]
        o = o[:n] - lam * o[n:]
        y = o * lax.rsqrt(jnp.mean(o * o, axis=-1, keepdims=True) + SUBLN_EPS)
        y = (y * sw_ref[...]) * (1.0 - lam_init)
        o_ref[0, :, cols] = (y * _silu(z_ref[0, :, cols].astype(F32))).astype(BF16)


def _self_attention(lambda_qk, subln_w, layer, lam_init, p, col0):
    b, n, _ = p.shape
    d = N_HEADS * V_DIM

    def col(cb):
        return pl.BlockSpec((1, n, d), lambda bb: (bb, 0, cb - col0))

    return pl.pallas_call(
        functools.partial(_self_attn_kernel, lam_init=lam_init),
        grid=(b,),
        in_specs=[
            pl.BlockSpec((None, 4, HEAD_DIM), lambda bb: (layer, 0, 0)),
            pl.BlockSpec((None, 1, V_DIM), lambda bb: (layer, 0, 0)),
            col(COL_Q), col(COL_K), col(COL_V), col(COL_ZA),
        ],
        out_specs=pl.BlockSpec((1, n, d), lambda bb: (bb, 0, 0)),
        out_shape=jax.ShapeDtypeStruct((b, n, d), BF16),
        compiler_params=_cparams(("parallel",)),
        name="attn_self",
    )(lambda_qk, subln_w, p, p, p, p)


def _attention(lambda_qk, subln_w, layer, lam_init, q, k, v, z, kc, vc):
    b, s, _ = q[0].shape
    lc = kc[0].shape[1]
    n_keys = s + lc
    nh = N_HEADS
    sub = min(s, ATTN_SUB_ROWS)
    depth = min(ATTN_CARRY_BLOCKS, s // sub)

    def head_cols(rows, col):
        return pl.BlockSpec((1, rows, V_DIM), lambda bb, h: (bb, 0, col * nh + h))

    def next_head_cols(rows, col):
        def index(bb, h):
            flat = jnp.minimum(bb * nh + h + 1, b * nh - 1)
            return (flat // nh, 0, col * nh + flat % nh)
        return pl.BlockSpec((1, rows, V_DIM), index)

    in_specs = [
        pl.BlockSpec((None, 4, HEAD_DIM), lambda bb, h: (layer, 0, 0)),
        pl.BlockSpec((None, 1, V_DIM), lambda bb, h: (layer, 0, 0)),
        head_cols(s, q[1]), head_cols(s, k[1]), head_cols(s, v[1]), head_cols(s, z[1]),
        head_cols(lc, kc[1]), head_cols(lc, vc[1]),
        next_head_cols(depth * sub, q[1]), next_head_cols(s, k[1]), next_head_cols(lc, kc[1]),
    ]
    return pl.pallas_call(
        functools.partial(_attn_kernel, lam_init=lam_init),
        grid=(b, nh),
        in_specs=in_specs,
        out_specs=head_cols(s, 0),
        out_shape=jax.ShapeDtypeStruct((b, s, nh * V_DIM), BF16),
        scratch_shapes=[
            pltpu.VMEM((n_keys, V_DIM), BF16),
            pltpu.VMEM((n_keys, 2 * V_DIM), BF16),
            pltpu.VMEM((n_keys, V_DIM), BF16),
            pltpu.VMEM((depth, 2 * sub, n_keys), F32),
        ],
        compiler_params=_cparams(("arbitrary", "arbitrary")),
        name="attn_latent",
    )(lambda_qk, subln_w, q[0], k[0], v[0], z[0], kc[0], vc[0], q[0], k[0], kc[0])


def _fourier_kernel(u_ref, cs_ref, w_ref, z_ref, o_ref, ab_ref):
    r = pl.program_id(1)
    n = u_ref.shape[1]

    @pl.when(r == 0)
    def _():
        for g in range(u_ref.shape[2] // FGROUP_DIM):
            cols = slice(g * FGROUP_DIM, (g + 1) * FGROUP_DIM)
            ab = jnp.dot(u_ref[0, :, cols], cs_ref[...], preferred_element_type=F32)
            ab_ref[0:n, cols] = ab[:, :FGROUP_DIM].astype(BF16)
            ab_ref[n:2 * n, cols] = ab[:, FGROUP_DIM:].astype(BF16)

    y = jnp.dot(w_ref[...], ab_ref[...], preferred_element_type=F32)
    o_ref[0] = (y * _silu(z_ref[0].astype(F32))).astype(BF16)


def _dft_cos_sin(n):
    k = jnp.arange(n, dtype=jnp.int32)
    ang = ((k[:, None] * k[None, :]) % n).astype(F32) * (2.0 * math.pi / n)
    scale = 1.0 / math.sqrt(n)
    return jnp.cos(ang) * scale, jnp.sin(ang) * scale


def _fourier(p, col0, cs_chan, w_pos, tr):
    b, n, _ = p.shape
    d = N_HEADS * V_DIM
    return pl.pallas_call(
        _fourier_kernel,
        grid=(b, n // tr),
        in_specs=[
            pl.BlockSpec((1, n, d), lambda bb, r: (bb, 0, COL_UF - col0)),
            pl.BlockSpec((FGROUP_DIM, 2 * FGROUP_DIM), lambda bb, r: (0, 0)),
            pl.BlockSpec((tr, 2 * n), lambda bb, r: (r, 0)),
            pl.BlockSpec((1, tr, d), lambda bb, r: (bb, r, COL_ZF - col0)),
        ],
        out_specs=pl.BlockSpec((1, tr, d), lambda bb, r: (bb, r, 0)),
        out_shape=jax.ShapeDtypeStruct((b, n, d), BF16),
        scratch_shapes=[pltpu.VMEM((2 * n, d), BF16)],
        compiler_params=_cparams(("parallel", "arbitrary")),
        name="fourier",
    )(p, cs_chan, w_pos, p)


def _fourier_half_kernel(u_ref, cs_ref, c_ref, s_ref, rev_ref, zlo_ref, zhi_ref, o_ref, a_ref, b_ref):
    r = pl.program_id(1)
    tr = o_ref.shape[3]

    @pl.when(r == 0)
    def _():
        for g in range(u_ref.shape[2] // FGROUP_DIM):
            cols = slice(g * FGROUP_DIM, (g + 1) * FGROUP_DIM)
            ab = jnp.dot(u_ref[0, :, cols], cs_ref[...], preferred_element_type=F32)
            a_ref[:, cols] = ab[:, :FGROUP_DIM].astype(BF16)
            b_ref[:, cols] = ab[:, FGROUP_DIM:].astype(BF16)

    p = jnp.dot(c_ref[...], a_ref[...], preferred_element_type=F32)
    q = jnp.dot(s_ref[...], b_ref[...], preferred_element_type=F32)
    o_ref[0, 0, 0] = ((p[:tr] - q[:tr]) * _silu(zlo_ref[0].astype(F32))).astype(BF16)
    hi = jnp.dot(rev_ref[...], (p + q).astype(BF16), preferred_element_type=F32)
    o_ref[0, 0, 1] = (hi * _silu(zhi_ref[0].astype(F32))).astype(BF16)


def _fourier_half_tables(n, tr):
    ext = 2 * SUBLANES
    n_tiles = n // (2 * tr)
    rows = (jnp.arange(n_tiles, dtype=jnp.int32)[:, None] * tr
            + jnp.arange(tr + ext, dtype=jnp.int32)[None, :])
    k = jnp.arange(n, dtype=jnp.int32)
    ang = ((rows[:, :, None] * k[None, None, :]) % n).astype(F32) * (2.0 * math.pi / n)
    scale = 1.0 / math.sqrt(n)
    i = jnp.arange(tr, dtype=jnp.int32)[:, None]
    j = jnp.arange(tr + ext, dtype=jnp.int32)[None, :]
    rev = (j == tr - i).astype(BF16)
    return (jnp.cos(ang) * scale).astype(BF16), (jnp.sin(ang) * scale).astype(BF16), rev


def _fourier_half(p, col0, cs_chan, tabs, tr):
    b, n, _ = p.shape
    d = N_HEADS * V_DIM
    c_t, s_t, rev = tabs
    n_tiles, rows_ext, _ = c_t.shape
    return pl.pallas_call(
        _fourier_half_kernel,
        grid=(b, n_tiles),
        in_specs=[
            pl.BlockSpec((1, n, d), lambda bb, r: (bb, 0, COL_UF - col0)),
            pl.BlockSpec((FGROUP_DIM, 2 * FGROUP_DIM), lambda bb, r: (0, 0)),
            pl.BlockSpec((None, rows_ext, n), lambda bb, r: (r, 0, 0)),
            pl.BlockSpec((None, rows_ext, n), lambda bb, r: (r, 0, 0)),
            pl.BlockSpec((tr, rows_ext), lambda bb, r: (0, 0)),
            pl.BlockSpec((1, tr, d), lambda bb, r: (bb, r, COL_ZF - col0)),
            pl.BlockSpec((1, tr, d), lambda bb, r: (bb, 2 * n_tiles - 1 - r, COL_ZF - col0)),
        ],
        out_specs=pl.BlockSpec((1, 1, 2, tr, d), lambda bb, r: (bb, r, 0, 0, 0)),
        out_shape=jax.ShapeDtypeStruct((b, n_tiles, 2, tr, d), BF16),
        scratch_shapes=[pltpu.VMEM((n, d), BF16), pltpu.VMEM((n, d), BF16)],
        compiler_params=_cparams(("parallel", "arbitrary")),
        name="fourier_half",
    )(p, cs_chan, c_t, s_t, rev, p, p)


def _sigmoid(z):
    return 0.5 * jnp.tanh(0.5 * z) + 0.5


def _silu(z):
    h = 0.5 * z
    return h * jnp.tanh(h) + h


def _merge_kernel(ya_ref, yf_ref, xin_ref, bg_ref, cg_ref, zc_ref,
                  ga_ref, gc_ref, gf_ref, xin_p_ref, cg_p_ref, xin_n_ref, cg_n_ref,
                  x_ref, gate_ref, cw_ref, wa_ref, wc_ref, wf_ref, wo_ref, *rest):
    fnw_ref, o_ref = rest if len(rest) == 2 else (None, rest[0])
    i = pl.program_id(1)
    tm = x_ref.shape[1]
    pad = SUBLANES

    def branch(y_gated, w_ref, g_ref):
        t = jnp.dot(y_gated, w_ref[...], preferred_element_type=F32)
        return _sigmoid(g_ref[0].astype(F32)) * t

    t_a = jnp.dot(ya_ref[0], wa_ref[...], preferred_element_type=F32)
    merged_af = (_sigmoid(ga_ref[0].astype(F32)) * t_a
                 + branch(yf_ref[...], wf_ref, gf_ref))
    bits = pltpu.bitcast(t_a[0:SUBLANES, 0:LANES], jnp.uint32)
    zero = pltpu.bitcast((bits >> 16) >> 16, F32)[0:1, 0:1]

    u = cg_ref[0].astype(F32) * xin_ref[0].astype(F32)
    up = cg_p_ref[0, pad - 1:pad, :].astype(F32) * xin_p_ref[0, pad - 1:pad, :].astype(F32)
    un = cg_n_ref[0, 0:1, :].astype(F32) * xin_n_ref[0, 0:1, :].astype(F32)
    up = jnp.where(i == 0, 0.0, up)
    un = jnp.where(i == pl.num_programs(1) - 1, 0.0, un)
    row = lax.broadcasted_iota(jnp.int32, (tm, 1), 0)
    u_prev = jnp.where(row == 0, up, pltpu.roll(u, 1, axis=0))
    u_next = jnp.where(row == tm - 1, un, pltpu.roll(u, tm - 1, axis=0))
    cw = cw_ref[...] + zero
    conv = cw[0:1] * u_prev + cw[1:2] * u + cw[2:3] * u_next
    y_c = bg_ref[0].astype(F32) * conv
    yc_gated = (y_c * _silu(zc_ref[0].astype(F32))).astype(BF16)
    merged = merged_af + branch(yc_gated, wc_ref, gc_ref)
    out = jnp.dot(merged.astype(BF16), wo_ref[...], preferred_element_type=F32)
    x_new = x_ref[0] + gate_ref[0] * out
    if fnw_ref is not None:
        y = x_new * lax.rsqrt(jnp.mean(x_new * x_new, axis=-1, keepdims=True) + NORM_EPS)
        x_new = y * fnw_ref[...]
    o_ref[0] = x_new


def _merge(x, p, col0, ya, yf, yf_spec, mod3, mod_row, conv_w, wa, wc, wf, wo, layer, tm,
           final_norm_w=None):
    bx, sx, d = x.shape
    nb = tm // SUBLANES
    last_halo = sx // SUBLANES - 1

    def whole(b, i):
        return (b, i, 0)

    def col(cb):
        return pl.BlockSpec((1, tm, d), lambda b, i: (b, i, cb - col0))

    def halo_prev(cb):
        return pl.BlockSpec((1, SUBLANES, d),
                            lambda b, i: (b, jnp.maximum(i * nb - 1, 0), cb - col0))

    def halo_next(cb):
        return pl.BlockSpec((1, SUBLANES, d),
                            lambda b, i: (b, jnp.minimum((i + 1) * nb, last_halo), cb - col0))

    def weight():
        return pl.BlockSpec((None, d, d), lambda b, i: (layer, 0, 0), pipeline_mode=pl.Buffered(1))

    in_specs = [
        pl.BlockSpec((1, tm, d), whole), yf_spec,
        col(COL_XIN), col(COL_BG), col(COL_CG), col(COL_ZC),
        col(COL_GL), col(COL_GL + 1), col(COL_GL + 2),
        halo_prev(COL_XIN), halo_prev(COL_CG), halo_next(COL_XIN), halo_next(COL_CG),
        pl.BlockSpec((1, tm, d), whole),
        pl.BlockSpec((1, 1, d), lambda b, i: (mod_row(b), 0, 2)),
        pl.BlockSpec((None, 3, d), lambda b, i: (layer, 0, 0)),
        weight(), weight(), weight(), weight(),
    ]
    args = [ya, yf, p, p, p, p, p, p, p, p, p, p, p, x, mod3, conv_w, wa, wc, wf, wo]
    if final_norm_w is not None:
        in_specs.append(pl.BlockSpec((1, d), lambda b, i: (0, 0)))
        args.append(final_norm_w)
    return pl.pallas_call(
        _merge_kernel,
        grid=(bx, sx // tm),
        in_specs=in_specs,
        out_specs=pl.BlockSpec((1, tm, d), lambda b, i: (b, i, 0)),
        out_shape=jax.ShapeDtypeStruct((bx, sx, d), F32),
        compiler_params=_cparams(("parallel", "parallel")),
        name="merge",
    )(*args)


def _rope_tables(n_tokens):
    axis_dim = HEAD_DIM // 2
    rows = n_tokens // GRID_W
    row = jnp.repeat(jnp.arange(rows), GRID_W).astype(F32)
    col = jnp.tile(jnp.arange(GRID_W), rows).astype(F32)
    inv_freq = ROPE_BASE ** (-jnp.arange(0, axis_dim, 2, dtype=F32) / axis_dim)
    ang_r = row[:, None] * inv_freq
    ang_c = col[:, None] * inv_freq
    ang = jnp.concatenate([ang_r, ang_c] * 4, axis=-1)
    sign = jnp.where(jnp.arange(V_DIM) < V_DIM // 2, -1.0, 1.0).astype(F32)
    cos = jnp.cos(ang)
    sin = jnp.sin(ang) * sign
    return jnp.stack([cos * Q_SCALE, cos]), jnp.stack([sin * Q_SCALE, sin])


def _head_layout(w_qk):
    depth, d, w = w_qk.shape
    t = w_qk.reshape(depth, d, w // V_DIM, 2, 2, 2, HEAD_DIM // 4)
    return t.transpose(0, 1, 2, 5, 3, 4, 6).reshape(depth, d, w)


def _fourier_tables(n):
    cn, sn = _dft_cos_sin(n)
    return jnp.concatenate([cn, -sn], axis=1).astype(BF16)


def _pick_tile(n, target):
    t = min(n, target)
    while n % t:
        t //= 2
    return t


def kernel(x, c, ctx, c_ctx, norm_w, w_mod, b_mod, w_in, lambda_qk, subln_w, conv_w,
           w_attn_o, w_conv_o, w_four_o, w_out, final_norm_w):
    b, s, d = x.shape
    lc = ctx.shape[1]
    depth = w_in.shape[0]
    assert d == N_HEADS * V_DIM and w_in.shape[2] == N_PROJ_BLOCKS * d
    assert s % GRID_W == 0 and s % LANES == 0 and lc % SUBLANES == 0

    pad = (-(b + 1)) % MOD_ROWS_PAD
    cond = jnp.concatenate([c, c_ctx[None, :], jnp.zeros((pad, d), F32)], axis=0)
    mod = _modulation(cond, w_mod, b_mod)
    n_rows = cond.shape[0]

    norm_w = norm_w.reshape(depth, 1, d)
    subln_w = subln_w.reshape(depth, 1, V_DIM)
    n_qk = 2 * d
    w_in_b = w_in.astype(BF16)
    w_in_b = w_in_b.at[:, :, :n_qk].set(_head_layout(w_in_b[:, :, :n_qk]))
    wa_b, wc_b, wf_b, wo_b = (w.astype(BF16) for w in (w_attn_o, w_conv_o, w_four_o, w_out))

    rope_tabs = _rope_tables(s)
    cc, sc = _dft_cos_sin(FGROUP_DIM)
    cs_chan = jnp.concatenate([cc, sc], axis=1).astype(BF16)
    w_pos_ctx = _fourier_tables(lc)

    ctx_flat = ctx.reshape(1, b * lc, d)
    lat_row = lambda bb: bb
    ctx_row = lambda bb: b

    tm_lat = _pick_tile(s, 2048)
    tm_ctx = _pick_tile(b * lc, 2048)
    tmm_lat = _pick_tile(s // 2, 512)
    tmm_ctx = _pick_tile(lc, 256)
    four_tabs = _fourier_half_tables(s, tmm_lat)
    n_ftiles = s // (2 * tmm_lat)
    yf_lat_spec = pl.BlockSpec(
        (None, None, None, tmm_lat, d),
        lambda bb, i: (bb, jnp.where(i < n_ftiles, i, 2 * n_ftiles - 1 - i),
                       jnp.where(i < n_ftiles, 0, 1), 0, 0))
    yf_ctx_spec = pl.BlockSpec((None, tmm_ctx, d), lambda bb, i: (bb, i, 0))

    for l in range(depth):
        last = l == depth - 1
        lam_init = 0.8 - 0.6 * math.exp(-0.3 * l)
        mod3 = mod[l].reshape(n_rows, 1, 3 * d)

        p_lat, h_lat = _proj(x, mod3, lat_row, norm_w, w_in_b, l, COL_V, N_PROJ_BLOCKS - COL_V, tm_lat,
                             emit_h=True)
        p_qk = _proj_qk(h_lat, w_in_b, l, rope_tabs, tm_lat)
        c0, cn = (COL_K, 2) if last else (COL_Q, N_PROJ_BLOCKS)
        p_ctx = _proj(ctx_flat, mod3, ctx_row, norm_w, w_in_b, l, c0, cn, tm_ctx).reshape(b, lc, -1)

        ya = _attention(lambda_qk, subln_w, l, lam_init, (p_qk, COL_Q), (p_qk, COL_K),
                        (p_lat, 0), (p_lat, COL_ZA - COL_V),
                        (p_ctx, COL_K - c0), (p_ctx, COL_V - c0))
        yf = _fourier_half(p_lat, COL_V, cs_chan, four_tabs, tmm_lat)
        x_new = _merge(x, p_lat, COL_V, ya, yf, yf_lat_spec, mod3, lat_row, conv_w, wa_b, wc_b, wf_b,
                       wo_b, l, tmm_lat, final_norm_w.reshape(1, d) if last else None)

        if not last:
            yac = _self_attention(lambda_qk, subln_w, l, lam_init, p_ctx, COL_Q)
            yfc = _fourier(p_ctx, COL_Q, cs_chan, w_pos_ctx, lc)
            ctx3 = ctx_flat.reshape(b, lc, d)
            ctx_flat = _merge(ctx3, p_ctx, COL_Q, yac, yfc, yf_ctx_spec, mod3, ctx_row, conv_w, wa_b, wc_b, wf_b,
                              wo_b, l, tmm_ctx).reshape(1, b * lc, d)
        x = x_new

    return x
```

```python
import functools
import math

import jax
import jax.numpy as jnp
from jax import lax
from jax.experimental import pallas as pl
from jax.experimental.pallas import tpu as pltpu

F32 = jnp.float32
BF16 = jnp.bfloat16

N_HEADS = 8
HEAD_DIM = 64
V_DIM = 2 * HEAD_DIM
FGROUP_DIM = 128
GRID_W = 64
ROPE_BASE = 10000.0
NORM_EPS = 1e-6
SUBLN_EPS = 1e-5
N_PROJ_BLOCKS = 13
COL_Q, COL_K, COL_V, COL_ZA, COL_XIN, COL_BG, COL_CG, COL_ZC, COL_UF, COL_ZF, COL_GL = range(11)

LANES = 128
SUBLANES = 8
VMEM_LIMIT_BYTES = 56 * 1024 * 1024
MOD_ROWS_PAD = 8
ATTN_SUB_ROWS = 256
ATTN_CARRY_BLOCKS = 2
ATTN_HEADS_PER_STEP = 2
Q_SCALE = math.log2(math.e) / math.sqrt(HEAD_DIM)


def _cparams(sem):
    return pltpu.CompilerParams(dimension_semantics=sem, vmem_limit_bytes=VMEM_LIMIT_BYTES)


def _mod_kernel(cond_ref, w_ref, b_ref, o_ref):
    cond = cond_ref[...]
    a = cond * jax.nn.sigmoid(cond)
    o_ref[...] = jnp.dot(a, w_ref[...], preferred_element_type=F32,
                         precision=lax.Precision.HIGHEST) + b_ref[...]


def _modulation(cond, w_mod, b_mod):
    depth, d, w3 = w_mod.shape
    rows = cond.shape[0]
    tn = d
    return pl.pallas_call(
        _mod_kernel,
        grid=(depth, w3 // tn),
        in_specs=[
            pl.BlockSpec((rows, d), lambda l, j: (0, 0)),
            pl.BlockSpec((None, d, tn), lambda l, j: (l, 0, j)),
            pl.BlockSpec((None, 1, tn), lambda l, j: (l, 0, j)),
        ],
        out_specs=pl.BlockSpec((None, rows, tn), lambda l, j: (l, 0, j)),
        out_shape=jax.ShapeDtypeStruct((depth, rows, w3), F32),
        compiler_params=_cparams(("parallel", "parallel")),
        name="modulation",
    )(cond, w_mod, b_mod.reshape(depth, 1, w3))


def _store_block(acc, o_ref, j, rope_refs, scale_first):
    if rope_refs is not None:
        cos = rope_refs[0][...]
        sin = rope_refs[1][...]
        for hh in range(acc.shape[1] // LANES):
            a = acc[:, hh * LANES:(hh + 1) * LANES]
            r = a * cos + pltpu.roll(a, LANES // 2, axis=1) * sin
            o_ref[0, :, hh * LANES:(hh + 1) * LANES] = r.astype(BF16)
    elif scale_first:
        scale = jnp.where(j == 0, Q_SCALE, 1.0).astype(F32)
        o_ref[0] = (acc * scale).astype(BF16)
    else:
        o_ref[0] = acc.astype(BF16)


def _proj_kernel(x_ref, sh_ref, sc_ref, nw_ref, w_ref, o_ref, *rest, scale_first):
    h_ref = rest[-1]
    j = pl.program_id(2)

    @pl.when(j == 0)
    def _():
        x = x_ref[0]
        y = x * lax.rsqrt(jnp.mean(x * x, axis=-1, keepdims=True) + NORM_EPS)
        h = ((y * nw_ref[...]) * (1.0 + sc_ref[0]) + sh_ref[0]).astype(BF16)
        h_ref[...] = h
        if len(rest) == 2:
            rest[0][0] = h

    acc = jnp.dot(h_ref[...], w_ref[...], preferred_element_type=F32)
    _store_block(acc, o_ref, j, None, scale_first)


def _proj(x, mod3, mod_row, norm_w, w_in_b, layer, col0, ncols, tm, emit_h=False):
    bx, sx, d = x.shape
    tn = d
    in_specs = [
        pl.BlockSpec((1, tm, d), lambda b, i, j: (b, i, 0)),
        pl.BlockSpec((1, 1, d), lambda b, i, j: (mod_row(b), 0, 0)),
        pl.BlockSpec((1, 1, d), lambda b, i, j: (mod_row(b), 0, 1)),
        pl.BlockSpec((None, 1, d), lambda b, i, j: (layer, 0, 0)),
        pl.BlockSpec((None, d, tn), lambda b, i, j: (layer, 0, col0 + j)),
    ]
    out_specs = pl.BlockSpec((1, tm, tn), lambda b, i, j: (b, i, j))
    out_shape = jax.ShapeDtypeStruct((bx, sx, ncols * tn), BF16)
    if emit_h:
        out_specs = [out_specs, pl.BlockSpec((1, tm, d), lambda b, i, j: (b, i, 0))]
        out_shape = [out_shape, jax.ShapeDtypeStruct((bx, sx, d), BF16)]
    return pl.pallas_call(
        functools.partial(_proj_kernel, scale_first=(col0 == COL_Q)),
        grid=(bx, sx // tm, ncols),
        in_specs=in_specs,
        out_specs=out_specs,
        out_shape=out_shape,
        scratch_shapes=[pltpu.VMEM((tm, d), BF16)],
        compiler_params=_cparams(("parallel", "parallel", "arbitrary")),
        name="proj",
    )(x, mod3, mod3, norm_w, w_in_b)


def _proj_qk_kernel(h_ref, w_ref, cos_ref, sin_ref, o_ref):
    acc = jnp.dot(h_ref[0], w_ref[...], preferred_element_type=F32)
    _store_block(acc, o_ref, pl.program_id(2), (cos_ref, sin_ref), False)


def _proj_qk(h, w_in_b, layer, rope_tabs, tm):
    bx, sx, d = h.shape
    tn = d
    cos_t, sin_t = rope_tabs
    tab_spec = pl.BlockSpec((None, tm, LANES), lambda b, i, j: (j, i, 0))
    return pl.pallas_call(
        _proj_qk_kernel,
        grid=(bx, sx // tm, 2),
        in_specs=[
            pl.BlockSpec((1, tm, d), lambda b, i, j: (b, i, 0)),
            pl.BlockSpec((None, d, tn), lambda b, i, j: (layer, 0, COL_Q + j)),
            tab_spec, tab_spec,
        ],
        out_specs=pl.BlockSpec((1, tm, tn), lambda b, i, j: (b, i, j)),
        out_shape=jax.ShapeDtypeStruct((bx, sx, 2 * tn), BF16),
        compiler_params=_cparams(("parallel", "parallel", "arbitrary")),
        name="proj_rope",
    )(h, w_in_b, cos_t, sin_t)


def _attn_kernel(lq_ref, sw_ref, q_ref, k_ref, v_ref, z_ref, kc_ref, vc_ref, qn_ref, kn_ref, kcn_ref,
                 o_ref, kall_ref, vext_ref, kalln_ref, scarry_ref, *, lam_init):
    n_lat = k_ref.shape[1]
    n_heads, n_all = kall_ref.shape[0], kall_ref.shape[1]

    def head(ref, g, row_slice=slice(None)):
        return ref[0, row_slice, g * V_DIM:(g + 1) * V_DIM]

    ones_lane = lax.broadcasted_iota(jnp.int32, (n_all, V_DIM), 1) == 0
    for g in range(n_heads):
        kall_ref[g, 0:n_lat, :] = head(k_ref, g)
        kall_ref[g, n_lat:n_all, :] = head(kc_ref, g)
        vext_ref[g, 0:n_lat, 0:V_DIM] = head(v_ref, g)
        vext_ref[g, n_lat:n_all, 0:V_DIM] = head(vc_ref, g)
        vext_ref[g, :, V_DIM:2 * V_DIM] = jnp.where(ones_lane, 1.0, 0.0).astype(BF16)
    kalln_ref[0:n_lat, :] = kn_ref[0]
    kalln_ref[n_lat:n_all, :] = kcn_ref[0]

    lq = lq_ref[...].astype(F32)
    lam = (jnp.exp(jnp.sum(lq[0:1] * lq[1:2], axis=-1, keepdims=True))
           - jnp.exp(jnp.sum(lq[2:3] * lq[3:4], axis=-1, keepdims=True)) + lam_init)

    tq = q_ref.shape[1]
    sub = min(tq, ATTN_SUB_ROWS)
    n_sub = tq // sub
    lane = lax.broadcasted_iota(jnp.int32, (sub, V_DIM), 1)

    def scores(q, keys):
        zero = jnp.zeros_like(q)
        map1 = (lane % HEAD_DIM) < HEAD_DIM // 2
        qq = jnp.concatenate([jnp.where(map1, q, zero), jnp.where(map1, zero, q)], axis=0)
        return lax.dot_general(qq, keys, (((1,), (1,)), ((), ())), preferred_element_type=F32)

    def block_scores(t):
        g, j = divmod(t, n_sub)
        rows = slice(j * sub, (j + 1) * sub)
        if g < n_heads:
            return scores(head(q_ref, g, rows), kall_ref[g])
        return scores(qn_ref[0, rows, :], kalln_ref[...])

    def finish(t, get_s):
        g, j = divmod(t, n_sub)
        rows = slice(j * sub, (j + 1) * sub)
        m = jnp.max(get_s(), axis=-1, keepdims=True)
        p = jnp.exp2(get_s() - m).astype(BF16)
        nd = jnp.dot(p, vext_ref[g], preferred_element_type=F32)
        o = nd[:, 0:V_DIM] / nd[:, V_DIM:V_DIM + 1]
        o = o[:sub] - lam * o[sub:]
        y = o * lax.rsqrt(jnp.mean(o * o, axis=-1, keepdims=True) + SUBLN_EPS)
        y = (y * sw_ref[...]) * (1.0 - lam_init)
        z = head(z_ref, g, rows).astype(F32)
        o_ref[0, rows, g * V_DIM:(g + 1) * V_DIM] = (y * _silu(z)).astype(BF16)

    def value(v):
        return lambda: v

    depth = scarry_ref.shape[0]

    @pl.when((pl.program_id(0) == 0) & (pl.program_id(1) == 0))
    def _():
        for d in range(depth):
            scarry_ref[d] = block_scores(d)

    pending = [(lambda d=d: scarry_ref[d]) for d in range(depth)]

    for t in range(n_heads * n_sub):
        pending.append(value(block_scores(t + depth)))
        finish(t, pending.pop(0))
    for d in range(depth):
        scarry_ref[d] = pending[d]()


def _self_attn_kernel(lq_ref, sw_ref, q_ref, k_ref, v_ref, z_ref, o_ref, *, lam_init):
    n = q_ref.shape[1]
    lq = lq_ref[...].astype(F32)
    lam = (jnp.exp(jnp.sum(lq[0:1] * lq[1:2], axis=-1, keepdims=True))
           - jnp.exp(jnp.sum(lq[2:3] * lq[3:4], axis=-1, keepdims=True)) + lam_init)
    lane = lax.broadcasted_iota(jnp.int32, (n, V_DIM), 1)
    map1 = (lane % HEAD_DIM) < HEAD_DIM // 2
    ones_cols = jnp.where(lane == 0, 1.0, 0.0).astype(BF16)
    heads = [slice(h * V_DIM, (h + 1) * V_DIM) for h in range(q_ref.shape[2] // V_DIM)]

    def scores(cols):
        q = q_ref[0, :, cols]
        zero = jnp.zeros_like(q)
        qq = jnp.concatenate([jnp.where(map1, q, zero), jnp.where(map1, zero, q)], axis=0)
        return lax.dot_general(qq, k_ref[0, :, cols], (((1,), (1,)), ((), ())),
                               preferred_element_type=F32)

    s_all = [scores(cols) for cols in heads]
    p_all = [jnp.exp2(s - jnp.max(s, axis=-1, keepdims=True)).astype(BF16) for s in s_all]
    nd_all = [jnp.dot(p, jnp.concatenate([v_ref[0, :, cols], ones_cols], axis=1),
                      preferred_element_type=F32) for p, cols in zip(p_all, heads)]
    for nd, cols in zip(nd_all, heads):
        o = nd[:, 0:V_DIM] / nd[:, V_DIM:V_DIM + 1]
        o = o[:n] - lam * o[n:]
        y = o * lax.rsqrt(jnp.mean(o * o, axis=-1, keepdims=True) + SUBLN_EPS)
        y = (y * sw_ref[...]) * (1.0 - lam_init)
        o_ref[0, :, cols] = (y * _silu(z_ref[0, :, cols].astype(F32))).astype(BF16)


def _self_attention(lambda_qk, subln_w, layer, lam_init, p, col0):
    b, n, _ = p.shape
    d = N_HEADS * V_DIM

    def col(cb):
        return pl.BlockSpec((1, n, d), lambda bb: (bb, 0, cb - col0))

    return pl.pallas_call(
        functools.partial(_self_attn_kernel, lam_init=lam_init),
        grid=(b,),
        in_specs=[
            pl.BlockSpec((None, 4, HEAD_DIM), lambda bb: (layer, 0, 0)),
            pl.BlockSpec((None, 1, V_DIM), lambda bb: (layer, 0, 0)),
            col(COL_Q), col(COL_K), col(COL_V), col(COL_ZA),
        ],
        out_specs=pl.BlockSpec((1, n, d), lambda bb: (bb, 0, 0)),
        out_shape=jax.ShapeDtypeStruct((b, n, d), BF16),
        compiler_params=_cparams(("parallel",)),
        name="attn_self",
    )(lambda_qk, subln_w, p, p, p, p)


def _attention(lambda_qk, subln_w, layer, lam_init, q, k, v, z, kc, vc):
    b, s, _ = q[0].shape
    lc = kc[0].shape[1]
    n_keys = s + lc
    nh = N_HEADS
    gh = ATTN_HEADS_PER_STEP
    n_groups = nh // gh
    sub = min(s, ATTN_SUB_ROWS)
    depth = min(ATTN_CARRY_BLOCKS, s // sub)

    def group_cols(rows, col):
        return pl.BlockSpec((1, rows, gh * V_DIM), lambda bb, g: (bb, 0, col * n_groups + g))

    def next_head_cols(rows, col):
        def index(bb, g):
            flat = jnp.minimum(bb * n_groups + g + 1, b * n_groups - 1)
            return (flat // n_groups, 0, col * nh + (flat % n_groups) * gh)
        return pl.BlockSpec((1, rows, V_DIM), index)

    in_specs = [
        pl.BlockSpec((None, 4, HEAD_DIM), lambda bb, g: (layer, 0, 0)),
        pl.BlockSpec((None, 1, V_DIM), lambda bb, g: (layer, 0, 0)),
        group_cols(s, q[1]), group_cols(s, k[1]), group_cols(s, v[1]), group_cols(s, z[1]),
        group_cols(lc, kc[1]), group_cols(lc, vc[1]),
        next_head_cols(depth * sub, q[1]), next_head_cols(s, k[1]), next_head_cols(lc, kc[1]),
    ]
    return pl.pallas_call(
        functools.partial(_attn_kernel, lam_init=lam_init),
        grid=(b, n_groups),
        in_specs=in_specs,
        out_specs=group_cols(s, 0),
        out_shape=jax.ShapeDtypeStruct((b, s, nh * V_DIM), BF16),
        scratch_shapes=[
            pltpu.VMEM((gh, n_keys, V_DIM), BF16),
            pltpu.VMEM((gh, n_keys, 2 * V_DIM), BF16),
            pltpu.VMEM((n_keys, V_DIM), BF16),
            pltpu.VMEM((depth, 2 * sub, n_keys), F32),
        ],
        compiler_params=_cparams(("arbitrary", "arbitrary")),
        name="attn_latent",
    )(lambda_qk, subln_w, q[0], k[0], v[0], z[0], kc[0], vc[0], q[0], k[0], kc[0])


def _fourier_kernel(u_ref, cs_ref, w_ref, z_ref, o_ref, ab_ref):
    r = pl.program_id(1)
    n = u_ref.shape[1]

    @pl.when(r == 0)
    def _():
        for g in range(u_ref.shape[2] // FGROUP_DIM):
            cols = slice(g * FGROUP_DIM, (g + 1) * FGROUP_DIM)
            ab = jnp.dot(u_ref[0, :, cols], cs_ref[...], preferred_element_type=F32)
            ab_ref[0:n, cols] = ab[:, :FGROUP_DIM].astype(BF16)
            ab_ref[n:2 * n, cols] = ab[:, FGROUP_DIM:].astype(BF16)

    y = jnp.dot(w_ref[...], ab_ref[...], preferred_element_type=F32)
    o_ref[0] = (y * _silu(z_ref[0].astype(F32))).astype(BF16)


def _dft_cos_sin(n):
    k = jnp.arange(n, dtype=jnp.int32)
    ang = ((k[:, None] * k[None, :]) % n).astype(F32) * (2.0 * math.pi / n)
    scale = 1.0 / math.sqrt(n)
    return jnp.cos(ang) * scale, jnp.sin(ang) * scale


def _fourier(p, col0, cs_chan, w_pos, tr):
    b, n, _ = p.shape
    d = N_HEADS * V_DIM
    return pl.pallas_call(
        _fourier_kernel,
        grid=(b, n // tr),
        in_specs=[
            pl.BlockSpec((1, n, d), lambda bb, r: (bb, 0, COL_UF - col0)),
            pl.BlockSpec((FGROUP_DIM, 2 * FGROUP_DIM), lambda bb, r: (0, 0)),
            pl.BlockSpec((tr, 2 * n), lambda bb, r: (r, 0)),
            pl.BlockSpec((1, tr, d), lambda bb, r: (bb, r, COL_ZF - col0)),
        ],
        out_specs=pl.BlockSpec((1, tr, d), lambda bb, r: (bb, r, 0)),
        out_shape=jax.ShapeDtypeStruct((b, n, d), BF16),
        scratch_shapes=[pltpu.VMEM((2 * n, d), BF16)],
        compiler_params=_cparams(("parallel", "arbitrary")),
        name="fourier",
    )(p, cs_chan, w_pos, p)


def _fourier_half_kernel(u_ref, cs_ref, c_ref, s_ref, rev_ref, zlo_ref, zhi_ref, o_ref, a_ref, b_ref):
    r = pl.program_id(1)
    tr = o_ref.shape[3]

    @pl.when(r == 0)
    def _():
        for g in range(u_ref.shape[2] // FGROUP_DIM):
            cols = slice(g * FGROUP_DIM, (g + 1) * FGROUP_DIM)
            ab = jnp.dot(u_ref[0, :, cols], cs_ref[...], preferred_element_type=F32)
            a_ref[:, cols] = ab[:, :FGROUP_DIM].astype(BF16)
            b_ref[:, cols] = ab[:, FGROUP_DIM:].astype(BF16)

    p = jnp.dot(c_ref[...], a_ref[...], preferred_element_type=F32)
    q = jnp.dot(s_ref[...], b_ref[...], preferred_element_type=F32)
    o_ref[0, 0, 0] = ((p[:tr] - q[:tr]) * _silu(zlo_ref[0].astype(F32))).astype(BF16)
    hi = jnp.dot(rev_ref[...], (p + q).astype(BF16), preferred_element_type=F32)
    o_ref[0, 0, 1] = (hi * _silu(zhi_ref[0].astype(F32))).astype(BF16)


def _fourier_half_tables(n, tr):
    ext = 2 * SUBLANES
    n_tiles = n // (2 * tr)
    rows = (jnp.arange(n_tiles, dtype=jnp.int32)[:, None] * tr
            + jnp.arange(tr + ext, dtype=jnp.int32)[None, :])
    k = jnp.arange(n, dtype=jnp.int32)
    ang = ((rows[:, :, None] * k[None, None, :]) % n).astype(F32) * (2.0 * math.pi / n)
    scale = 1.0 / math.sqrt(n)
    i = jnp.arange(tr, dtype=jnp.int32)[:, None]
    j = jnp.arange(tr + ext, dtype=jnp.int32)[None, :]
    rev = (j == tr - i).astype(BF16)
    return (jnp.cos(ang) * scale).astype(BF16), (jnp.sin(ang) * scale).astype(BF16), rev


def _fourier_half(p, col0, cs_chan, tabs, tr):
    b, n, _ = p.shape
    d = N_HEADS * V_DIM
    c_t, s_t, rev = tabs
    n_tiles, rows_ext, _ = c_t.shape
    return pl.pallas_call(
        _fourier_half_kernel,
        grid=(b, n_tiles),
        in_specs=[
            pl.BlockSpec((1, n, d), lambda bb, r: (bb, 0, COL_UF - col0)),
            pl.BlockSpec((FGROUP_DIM, 2 * FGROUP_DIM), lambda bb, r: (0, 0)),
            pl.BlockSpec((None, rows_ext, n), lambda bb, r: (r, 0, 0)),
            pl.BlockSpec((None, rows_ext, n), lambda bb, r: (r, 0, 0)),
            pl.BlockSpec((tr, rows_ext), lambda bb, r: (0, 0)),
            pl.BlockSpec((1, tr, d), lambda bb, r: (bb, r, COL_ZF - col0)),
            pl.BlockSpec((1, tr, d), lambda bb, r: (bb, 2 * n_tiles - 1 - r, COL_ZF - col0)),
        ],
        out_specs=pl.BlockSpec((1, 1, 2, tr, d), lambda bb, r: (bb, r, 0, 0, 0)),
        out_shape=jax.ShapeDtypeStruct((b, n_tiles, 2, tr, d), BF16),
        scratch_shapes=[pltpu.VMEM((n, d), BF16), pltpu.VMEM((n, d), BF16)],
        compiler_params=_cparams(("parallel", "arbitrary")),
        name="fourier_half",
    )(p, cs_chan, c_t, s_t, rev, p, p)


def _sigmoid(z):
    return 0.5 * jnp.tanh(0.5 * z) + 0.5


def _silu(z):
    h = 0.5 * z
    return h * jnp.tanh(h) + h


def _merge_kernel(ya_ref, yf_ref, xin_ref, bg_ref, cg_ref, zc_ref,
                  ga_ref, gc_ref, gf_ref, xin_p_ref, cg_p_ref, xin_n_ref, cg_n_ref,
                  x_ref, gate_ref, cw_ref, wa_ref, wc_ref, wf_ref, wo_ref, *rest):
    fnw_ref, o_ref = rest if len(rest) == 2 else (None, rest[0])
    i = pl.program_id(1)
    tm = x_ref.shape[1]
    pad = SUBLANES

    def branch(y_gated, w_ref, g_ref):
        t = jnp.dot(y_gated, w_ref[...], preferred_element_type=F32)
        return _sigmoid(g_ref[0].astype(F32)) * t

    t_a = jnp.dot(ya_ref[0], wa_ref[...], preferred_element_type=F32)
    merged_af = (_sigmoid(ga_ref[0].astype(F32)) * t_a
                 + branch(yf_ref[...], wf_ref, gf_ref))
    bits = pltpu.bitcast(t_a[0:SUBLANES, 0:LANES], jnp.uint32)
    zero = pltpu.bitcast((bits >> 16) >> 16, F32)[0:1, 0:1]

    u = cg_ref[0].astype(F32) * xin_ref[0].astype(F32)
    up = cg_p_ref[0, pad - 1:pad, :].astype(F32) * xin_p_ref[0, pad - 1:pad, :].astype(F32)
    un = cg_n_ref[0, 0:1, :].astype(F32) * xin_n_ref[0, 0:1, :].astype(F32)
    up = jnp.where(i == 0, 0.0, up)
    un = jnp.where(i == pl.num_programs(1) - 1, 0.0, un)
    row = lax.broadcasted_iota(jnp.int32, (tm, 1), 0)
    u_prev = jnp.where(row == 0, up, pltpu.roll(u, 1, axis=0))
    u_next = jnp.where(row == tm - 1, un, pltpu.roll(u, tm - 1, axis=0))
    cw = cw_ref[...] + zero
    conv = cw[0:1] * u_prev + cw[1:2] * u + cw[2:3] * u_next
    y_c = bg_ref[0].astype(F32) * conv
    yc_gated = (y_c * _silu(zc_ref[0].astype(F32))).astype(BF16)
    merged = merged_af + branch(yc_gated, wc_ref, gc_ref)
    out = jnp.dot(merged.astype(BF16), wo_ref[...], preferred_element_type=F32)
    x_new = x_ref[0] + gate_ref[0] * out
    if fnw_ref is not None:
        y = x_new * lax.rsqrt(jnp.mean(x_new * x_new, axis=-1, keepdims=True) + NORM_EPS)
        x_new = y * fnw_ref[...]
    o_ref[0] = x_new


def _merge(x, p, col0, ya, yf, yf_spec, mod3, mod_row, conv_w, wa, wc, wf, wo, layer, tm,
           final_norm_w=None):
    bx, sx, d = x.shape
    nb = tm // SUBLANES
    last_halo = sx // SUBLANES - 1

    def whole(b, i):
        return (b, i, 0)

    def col(cb):
        return pl.BlockSpec((1, tm, d), lambda b, i: (b, i, cb - col0))

    def halo_prev(cb):
        return pl.BlockSpec((1, SUBLANES, d),
                            lambda b, i: (b, jnp.maximum(i * nb - 1, 0), cb - col0))

    def halo_next(cb):
        return pl.BlockSpec((1, SUBLANES, d),
                            lambda b, i: (b, jnp.minimum((i + 1) * nb, last_halo), cb - col0))

    def weight():
        return pl.BlockSpec((None, d, d), lambda b, i: (layer, 0, 0), pipeline_mode=pl.Buffered(1))

    in_specs = [
        pl.BlockSpec((1, tm, d), whole), yf_spec,
        col(COL_XIN), col(COL_BG), col(COL_CG), col(COL_ZC),
        col(COL_GL), col(COL_GL + 1), col(COL_GL + 2),
        halo_prev(COL_XIN), halo_prev(COL_CG), halo_next(COL_XIN), halo_next(COL_CG),
        pl.BlockSpec((1, tm, d), whole),
        pl.BlockSpec((1, 1, d), lambda b, i: (mod_row(b), 0, 2)),
        pl.BlockSpec((None, 3, d), lambda b, i: (layer, 0, 0)),
        weight(), weight(), weight(), weight(),
    ]
    args = [ya, yf, p, p, p, p, p, p, p, p, p, p, p, x, mod3, conv_w, wa, wc, wf, wo]
    if final_norm_w is not None:
        in_specs.append(pl.BlockSpec((1, d), lambda b, i: (0, 0)))
        args.append(final_norm_w)
    return pl.pallas_call(
        _merge_kernel,
        grid=(bx, sx // tm),
        in_specs=in_specs,
        out_specs=pl.BlockSpec((1, tm, d), lambda b, i: (b, i, 0)),
        out_shape=jax.ShapeDtypeStruct((bx, sx, d), F32),
        compiler_params=_cparams(("parallel", "parallel")),
        name="merge",
    )(*args)


def _rope_tables(n_tokens):
    axis_dim = HEAD_DIM // 2
    rows = n_tokens // GRID_W
    row = jnp.repeat(jnp.arange(rows), GRID_W).astype(F32)
    col = jnp.tile(jnp.arange(GRID_W), rows).astype(F32)
    inv_freq = ROPE_BASE ** (-jnp.arange(0, axis_dim, 2, dtype=F32) / axis_dim)
    ang_r = row[:, None] * inv_freq
    ang_c = col[:, None] * inv_freq
    ang = jnp.concatenate([ang_r, ang_c] * 4, axis=-1)
    sign = jnp.where(jnp.arange(V_DIM) < V_DIM // 2, -1.0, 1.0).astype(F32)
    cos = jnp.cos(ang)
    sin = jnp.sin(ang) * sign
    return jnp.stack([cos * Q_SCALE, cos]), jnp.stack([sin * Q_SCALE, sin])


def _head_layout(w_qk):
    depth, d, w = w_qk.shape
    t = w_qk.reshape(depth, d, w // V_DIM, 2, 2, 2, HEAD_DIM // 4)
    return t.transpose(0, 1, 2, 5, 3, 4, 6).reshape(depth, d, w)


def _fourier_tables(n):
    cn, sn = _dft_cos_sin(n)
    return jnp.concatenate([cn, -sn], axis=1).astype(BF16)


def _pick_tile(n, target):
    t = min(n, target)
    while n % t:
        t //= 2
    return t


def kernel(x, c, ctx, c_ctx, norm_w, w_mod, b_mod, w_in, lambda_qk, subln_w, conv_w,
           w_attn_o, w_conv_o, w_four_o, w_out, final_norm_w):
    b, s, d = x.shape
    lc = ctx.shape[1]
    depth = w_in.shape[0]
    assert d == N_HEADS * V_DIM and w_in.shape[2] == N_PROJ_BLOCKS * d
    assert s % GRID_W == 0 and s % LANES == 0 and lc % SUBLANES == 0

    pad = (-(b + 1)) % MOD_ROWS_PAD
    cond = jnp.concatenate([c, c_ctx[None, :], jnp.zeros((pad, d), F32)], axis=0)
    mod = _modulation(cond, w_mod, b_mod)
    n_rows = cond.shape[0]

    norm_w = norm_w.reshape(depth, 1, d)
    subln_w = subln_w.reshape(depth, 1, V_DIM)
    n_qk = 2 * d
    w_in_b = w_in.astype(BF16)
    w_in_b = w_in_b.at[:, :, :n_qk].set(_head_layout(w_in_b[:, :, :n_qk]))
    wa_b, wc_b, wf_b, wo_b = (w.astype(BF16) for w in (w_attn_o, w_conv_o, w_four_o, w_out))

    rope_tabs = _rope_tables(s)
    cc, sc = _dft_cos_sin(FGROUP_DIM)
    cs_chan = jnp.concatenate([cc, sc], axis=1).astype(BF16)
    w_pos_ctx = _fourier_tables(lc)

    ctx_flat = ctx.reshape(1, b * lc, d)
    lat_row = lambda bb: bb
    ctx_row = lambda bb: b

    tm_lat = _pick_tile(s, 2048)
    tm_ctx = _pick_tile(b * lc, 2048)
    tmm_lat = _pick_tile(s // 2, 512)
    tmm_ctx = _pick_tile(lc, 256)
    four_tabs = _fourier_half_tables(s, tmm_lat)
    n_ftiles = s // (2 * tmm_lat)
    yf_lat_spec = pl.BlockSpec(
        (None, None, None, tmm_lat, d),
        lambda bb, i: (bb, jnp.where(i < n_ftiles, i, 2 * n_ftiles - 1 - i),
                       jnp.where(i < n_ftiles, 0, 1), 0, 0))
    yf_ctx_spec = pl.BlockSpec((None, tmm_ctx, d), lambda bb, i: (bb, i, 0))

    for l in range(depth):
        last = l == depth - 1
        lam_init = 0.8 - 0.6 * math.exp(-0.3 * l)
        mod3 = mod[l].reshape(n_rows, 1, 3 * d)

        p_lat, h_lat = _proj(x, mod3, lat_row, norm_w, w_in_b, l, COL_V, N_PROJ_BLOCKS - COL_V, tm_lat,
                             emit_h=True)
        p_qk = _proj_qk(h_lat, w_in_b, l, rope_tabs, tm_lat)
        c0, cn = (COL_K, 2) if last else (COL_Q, N_PROJ_BLOCKS)
        p_ctx = _proj(ctx_flat, mod3, ctx_row, norm_w, w_in_b, l, c0, cn, tm_ctx).reshape(b, lc, -1)

        ya = _attention(lambda_qk, subln_w, l, lam_init, (p_qk, COL_Q), (p_qk, COL_K),
                        (p_lat, 0), (p_lat, COL_ZA - COL_V),
                        (p_ctx, COL_K - c0), (p_ctx, COL_V - c0))
        yf = _fourier_half(p_lat, COL_V, cs_chan, four_tabs, tmm_lat)
        x_new = _merge(x, p_lat, COL_V, ya, yf, yf_lat_spec, mod3, lat_row, conv_w, wa_b, wc_b, wf_b,
                       wo_b, l, tmm_lat, final_norm_w.reshape(1, d) if last else None)

        if not last:
            yac = _self_attention(lambda_qk, subln_w, l, lam_init, p_ctx, COL_Q)
            yfc = _fourier(p_ctx, COL_Q, cs_chan, w_pos_ctx, lc)
            ctx3 = ctx_flat.reshape(b, lc, d)
            ctx_flat = _merge(ctx3, p_ctx, COL_Q, yac, yfc, yf_ctx_spec, mod3, ctx_row, conv_w, wa_b, wc_b, wf_b,
                              wo_b, l, tmm_ctx).reshape(1, b * lc, d)
        x = x_new

    return x
```

```python
import functools
import math

import jax
import jax.numpy as jnp
from jax import lax
from jax.experimental import pallas as pl
from jax.experimental.pallas import tpu as pltpu

F32 = jnp.float32
BF16 = jnp.bfloat16

N_HEADS = 8
HEAD_DIM = 64
V_DIM = 2 * HEAD_DIM
FGROUP_DIM = 128
GRID_W = 64
ROPE_BASE = 10000.0
NORM_EPS = 1e-6
SUBLN_EPS = 1e-5
N_PROJ_BLOCKS = 13
COL_Q, COL_K, COL_V, COL_ZA, COL_XIN, COL_BG, COL_CG, COL_ZC, COL_UF, COL_ZF, COL_GL = range(11)

LANES = 128
SUBLANES = 8
VMEM_LIMIT_BYTES = 56 * 1024 * 1024
MOD_ROWS_PAD = 8
ATTN_SUB_ROWS = 256
ATTN_CARRY_BLOCKS = 2
ATTN_HEADS_PER_STEP = 2
Q_SCALE = math.log2(math.e) / math.sqrt(HEAD_DIM)


def _cparams(sem):
    return pltpu.CompilerParams(dimension_semantics=sem, vmem_limit_bytes=VMEM_LIMIT_BYTES)


def _mod_kernel(cond_ref, w_ref, b_ref, o_ref):
    cond = cond_ref[...]
    a = cond * jax.nn.sigmoid(cond)
    o_ref[...] = jnp.dot(a, w_ref[...], preferred_element_type=F32,
                         precision=lax.Precision.HIGHEST) + b_ref[...]


def _modulation(cond, w_mod, b_mod):
    depth, d, w3 = w_mod.shape
    rows = cond.shape[0]
    tn = d
    return pl.pallas_call(
        _mod_kernel,
        grid=(depth, w3 // tn),
        in_specs=[
            pl.BlockSpec((rows, d), lambda l, j: (0, 0)),
            pl.BlockSpec((None, d, tn), lambda l, j: (l, 0, j)),
            pl.BlockSpec((None, 1, tn), lambda l, j: (l, 0, j)),
        ],
        out_specs=pl.BlockSpec((None, rows, tn), lambda l, j: (l, 0, j)),
        out_shape=jax.ShapeDtypeStruct((depth, rows, w3), F32),
        compiler_params=_cparams(("parallel", "parallel")),
        name="modulation",
    )(cond, w_mod, b_mod.reshape(depth, 1, w3))


def _store_block(acc, o_ref, j, rope_refs, scale_first):
    if rope_refs is not None:
        cos = rope_refs[0][...]
        sin = rope_refs[1][...]
        for hh in range(acc.shape[1] // LANES):
            a = acc[:, hh * LANES:(hh + 1) * LANES]
            r = a * cos + pltpu.roll(a, LANES // 2, axis=1) * sin
            o_ref[0, :, hh * LANES:(hh + 1) * LANES] = r.astype(BF16)
    elif scale_first:
        scale = jnp.where(j == 0, Q_SCALE, 1.0).astype(F32)
        o_ref[0] = (acc * scale).astype(BF16)
    else:
        o_ref[0] = acc.astype(BF16)


def _proj_kernel(x_ref, sh_ref, sc_ref, nw_ref, w_ref, o_ref, *rest, scale_first):
    h_ref = rest[-1]
    j = pl.program_id(2)

    @pl.when(j == 0)
    def _():
        x = x_ref[0]
        y = x * lax.rsqrt(jnp.mean(x * x, axis=-1, keepdims=True) + NORM_EPS)
        h = ((y * nw_ref[...]) * (1.0 + sc_ref[0]) + sh_ref[0]).astype(BF16)
        h_ref[...] = h
        if len(rest) == 2:
            rest[0][0] = h

    acc = jnp.dot(h_ref[...], w_ref[...], preferred_element_type=F32)
    _store_block(acc, o_ref, j, None, scale_first)


def _proj(x, mod3, mod_row, norm_w, w_in_b, layer, col0, ncols, tm, emit_h=False):
    bx, sx, d = x.shape
    tn = d
    in_specs = [
        pl.BlockSpec((1, tm, d), lambda b, i, j: (b, i, 0)),
        pl.BlockSpec((1, 1, d), lambda b, i, j: (mod_row(b), 0, 0)),
        pl.BlockSpec((1, 1, d), lambda b, i, j: (mod_row(b), 0, 1)),
        pl.BlockSpec((None, 1, d), lambda b, i, j: (layer, 0, 0)),
        pl.BlockSpec((None, d, tn), lambda b, i, j: (layer, 0, col0 + j)),
    ]
    out_specs = pl.BlockSpec((1, tm, tn), lambda b, i, j: (b, i, j))
    out_shape = jax.ShapeDtypeStruct((bx, sx, ncols * tn), BF16)
    if emit_h:
        out_specs = [out_specs, pl.BlockSpec((1, tm, d), lambda b, i, j: (b, i, 0))]
        out_shape = [out_shape, jax.ShapeDtypeStruct((bx, sx, d), BF16)]
    return pl.pallas_call(
        functools.partial(_proj_kernel, scale_first=(col0 == COL_Q)),
        grid=(bx, sx // tm, ncols),
        in_specs=in_specs,
        out_specs=out_specs,
        out_shape=out_shape,
        scratch_shapes=[pltpu.VMEM((tm, d), BF16)],
        compiler_params=_cparams(("parallel", "parallel", "arbitrary")),
        name="proj",
    )(x, mod3, mod3, norm_w, w_in_b)


def _proj_rest_kernel(h_ref, w_ref, o_ref):
    o_ref[0] = jnp.dot(h_ref[0], w_ref[...], preferred_element_type=F32).astype(BF16)


def _proj_rest(h, w_in_b, layer, col0, ncols, tm):
    bx, sx, d = h.shape
    tn = d
    return pl.pallas_call(
        _proj_rest_kernel,
        grid=(bx, sx // tm, ncols),
        in_specs=[
            pl.BlockSpec((1, tm, d), lambda b, i, j: (b, i, 0)),
            pl.BlockSpec((None, d, tn), lambda b, i, j: (layer, 0, col0 + j)),
        ],
        out_specs=pl.BlockSpec((1, tm, tn), lambda b, i, j: (b, i, j)),
        out_shape=jax.ShapeDtypeStruct((bx, sx, ncols * tn), BF16),
        compiler_params=_cparams(("parallel", "parallel", "arbitrary")),
        name="proj_rest",
    )(h, w_in_b)


def _proj_qk_kernel(h_ref, w_ref, cos_ref, sin_ref, o_ref):
    acc = jnp.dot(h_ref[0], w_ref[...], preferred_element_type=F32)
    _store_block(acc, o_ref, pl.program_id(2), (cos_ref, sin_ref), False)


def _proj_qk(h, w_in_b, layer, rope_tabs, tm):
    bx, sx, d = h.shape
    tn = d
    cos_t, sin_t = rope_tabs
    tab_spec = pl.BlockSpec((None, tm, LANES), lambda b, i, j: (j, i, 0))
    return pl.pallas_call(
        _proj_qk_kernel,
        grid=(bx, sx // tm, 2),
        in_specs=[
            pl.BlockSpec((1, tm, d), lambda b, i, j: (b, i, 0)),
            pl.BlockSpec((None, d, tn), lambda b, i, j: (layer, 0, COL_Q + j)),
            tab_spec, tab_spec,
        ],
        out_specs=pl.BlockSpec((1, tm, tn), lambda b, i, j: (b, i, j)),
        out_shape=jax.ShapeDtypeStruct((bx, sx, 2 * tn), BF16),
        compiler_params=_cparams(("parallel", "parallel", "arbitrary")),
        name="proj_rope",
    )(h, w_in_b, cos_t, sin_t)


def _attn_kernel(lq_ref, sw_ref, q_ref, k_ref, v_ref, z_ref, kc_ref, vc_ref, qn_ref, kn_ref, kcn_ref,
                 o_ref, kall_ref, vext_ref, kalln_ref, scarry_ref, *, lam_init):
    n_lat = k_ref.shape[1]
    n_heads, n_all = kall_ref.shape[0], kall_ref.shape[1]

    def head(ref, g, row_slice=slice(None)):
        return ref[0, row_slice, g * V_DIM:(g + 1) * V_DIM]

    ones_lane = lax.broadcasted_iota(jnp.int32, (n_all, V_DIM), 1) == 0
    for g in range(n_heads):
        kall_ref[g, 0:n_lat, :] = head(k_ref, g)
        kall_ref[g, n_lat:n_all, :] = head(kc_ref, g)
        vext_ref[g, 0:n_lat, 0:V_DIM] = head(v_ref, g)
        vext_ref[g, n_lat:n_all, 0:V_DIM] = head(vc_ref, g)
        vext_ref[g, :, V_DIM:2 * V_DIM] = jnp.where(ones_lane, 1.0, 0.0).astype(BF16)
    kalln_ref[0:n_lat, :] = kn_ref[0]
    kalln_ref[n_lat:n_all, :] = kcn_ref[0]

    lq = lq_ref[...].astype(F32)
    lam = (jnp.exp(jnp.sum(lq[0:1] * lq[1:2], axis=-1, keepdims=True))
           - jnp.exp(jnp.sum(lq[2:3] * lq[3:4], axis=-1, keepdims=True)) + lam_init)

    tq = q_ref.shape[1]
    sub = min(tq, ATTN_SUB_ROWS)
    n_sub = tq // sub
    lane = lax.broadcasted_iota(jnp.int32, (sub, V_DIM), 1)

    def scores(q, keys):
        zero = jnp.zeros_like(q)
        map1 = (lane % HEAD_DIM) < HEAD_DIM // 2
        qq = jnp.concatenate([jnp.where(map1, q, zero), jnp.where(map1, zero, q)], axis=0)
        return lax.dot_general(qq, keys, (((1,), (1,)), ((), ())), preferred_element_type=F32)

    def block_scores(t):
        g, j = divmod(t, n_sub)
        rows = slice(j * sub, (j + 1) * sub)
        if g < n_heads:
            return scores(head(q_ref, g, rows), kall_ref[g])
        return scores(qn_ref[0, rows, :], kalln_ref[...])

    def finish(t, get_s):
        g, j = divmod(t, n_sub)
        rows = slice(j * sub, (j + 1) * sub)
        m = jnp.max(get_s(), axis=-1, keepdims=True)
        p = jnp.exp2(get_s() - m).astype(BF16)
        nd = jnp.dot(p, vext_ref[g], preferred_element_type=F32)
        o = nd[:, 0:V_DIM] / nd[:, V_DIM:V_DIM + 1]
        o = o[:sub] - lam * o[sub:]
        y = o * lax.rsqrt(jnp.mean(o * o, axis=-1, keepdims=True) + SUBLN_EPS)
        y = (y * sw_ref[...]) * (1.0 - lam_init)
        z = head(z_ref, g, rows).astype(F32)
        o_ref[0, rows, g * V_DIM:(g + 1) * V_DIM] = (y * _silu(z)).astype(BF16)

    def value(v):
        return lambda: v

    depth = scarry_ref.shape[0]

    @pl.when((pl.program_id(0) == 0) & (pl.program_id(1) == 0))
    def _():
        for d in range(depth):
            scarry_ref[d] = block_scores(d)

    pending = [(lambda d=d: scarry_ref[d]) for d in range(depth)]

    for t in range(n_heads * n_sub):
        pending.append(value(block_scores(t + depth)))
        finish(t, pending.pop(0))
    for d in range(depth):
        scarry_ref[d] = pending[d]()


def _self_attn_kernel(lq_ref, sw_ref, q_ref, k_ref, v_ref, z_ref, o_ref, *, lam_init):
    n = q_ref.shape[1]
    lq = lq_ref[...].astype(F32)
    lam = (jnp.exp(jnp.sum(lq[0:1] * lq[1:2], axis=-1, keepdims=True))
           - jnp.exp(jnp.sum(lq[2:3] * lq[3:4], axis=-1, keepdims=True)) + lam_init)
    lane = lax.broadcasted_iota(jnp.int32, (n, V_DIM), 1)
    map1 = (lane % HEAD_DIM) < HEAD_DIM // 2
    ones_cols = jnp.where(lane == 0, 1.0, 0.0).astype(BF16)
    heads = [slice(h * V_DIM, (h + 1) * V_DIM) for h in range(q_ref.shape[2] // V_DIM)]

    def scores(cols):
        q = q_ref[0, :, cols]
        zero = jnp.zeros_like(q)
        qq = jnp.concatenate([jnp.where(map1, q, zero), jnp.where(map1, zero, q)], axis=0)
        return lax.dot_general(qq, k_ref[0, :, cols], (((1,), (1,)), ((), ())),
                               preferred_element_type=F32)

    s_all = [scores(cols) for cols in heads]
    p_all = [jnp.exp2(s - jnp.max(s, axis=-1, keepdims=True)).astype(BF16) for s in s_all]
    nd_all = [jnp.dot(p, jnp.concatenate([v_ref[0, :, cols], ones_cols], axis=1),
                      preferred_element_type=F32) for p, cols in zip(p_all, heads)]
    for nd, cols in zip(nd_all, heads):
        o = nd[:, 0:V_DIM] / nd[:, V_DIM:V_DIM + 1]
        o = o[:n] - lam * o[n:]
        y = o * lax.rsqrt(jnp.mean(o * o, axis=-1, keepdims=True) + SUBLN_EPS)
        y = (y * sw_ref[...]) * (1.0 - lam_init)
        o_ref[0, :, cols] = (y * _silu(z_ref[0, :, cols].astype(F32))).astype(BF16)


def _self_attention(lambda_qk, subln_w, layer, lam_init, p, col0):
    b, n, _ = p.shape
    d = N_HEADS * V_DIM

    def col(cb):
        return pl.BlockSpec((1, n, d), lambda bb: (bb, 0, cb - col0))

    return pl.pallas_call(
        functools.partial(_self_attn_kernel, lam_init=lam_init),
        grid=(b,),
        in_specs=[
            pl.BlockSpec((None, 4, HEAD_DIM), lambda bb: (layer, 0, 0)),
            pl.BlockSpec((None, 1, V_DIM), lambda bb: (layer, 0, 0)),
            col(COL_Q), col(COL_K), col(COL_V), col(COL_ZA),
        ],
        out_specs=pl.BlockSpec((1, n, d), lambda bb: (bb, 0, 0)),
        out_shape=jax.ShapeDtypeStruct((b, n, d), BF16),
        compiler_params=_cparams(("parallel",)),
        name="attn_self",
    )(lambda_qk, subln_w, p, p, p, p)


def _attention(lambda_qk, subln_w, layer, lam_init, q, k, v, z, kc, vc):
    b, s, _ = q[0].shape
    lc = kc[0].shape[1]
    n_keys = s + lc
    nh = N_HEADS
    gh = ATTN_HEADS_PER_STEP
    n_groups = nh // gh
    sub = min(s, ATTN_SUB_ROWS)
    depth = min(ATTN_CARRY_BLOCKS, s // sub)

    def group_cols(rows, col):
        return pl.BlockSpec((1, rows, gh * V_DIM), lambda bb, g: (bb, 0, col * n_groups + g))

    def next_head_cols(rows, col):
        def index(bb, g):
            flat = jnp.minimum(bb * n_groups + g + 1, b * n_groups - 1)
            return (flat // n_groups, 0, col * nh + (flat % n_groups) * gh)
        return pl.BlockSpec((1, rows, V_DIM), index)

    in_specs = [
        pl.BlockSpec((None, 4, HEAD_DIM), lambda bb, g: (layer, 0, 0)),
        pl.BlockSpec((None, 1, V_DIM), lambda bb, g: (layer, 0, 0)),
        group_cols(s, q[1]), group_cols(s, k[1]), group_cols(s, v[1]), group_cols(s, z[1]),
        group_cols(lc, kc[1]), group_cols(lc, vc[1]),
        next_head_cols(depth * sub, q[1]), next_head_cols(s, k[1]), next_head_cols(lc, kc[1]),
    ]
    return pl.pallas_call(
        functools.partial(_attn_kernel, lam_init=lam_init),
        grid=(b, n_groups),
        in_specs=in_specs,
        out_specs=group_cols(s, 0),
        out_shape=jax.ShapeDtypeStruct((b, s, nh * V_DIM), BF16),
        scratch_shapes=[
            pltpu.VMEM((gh, n_keys, V_DIM), BF16),
            pltpu.VMEM((gh, n_keys, 2 * V_DIM), BF16),
            pltpu.VMEM((n_keys, V_DIM), BF16),
            pltpu.VMEM((depth, 2 * sub, n_keys), F32),
        ],
        compiler_params=_cparams(("arbitrary", "arbitrary")),
        name="attn_latent",
    )(lambda_qk, subln_w, q[0], k[0], v[0], z[0], kc[0], vc[0], q[0], k[0], kc[0])


def _fourier_kernel(u_ref, cs_ref, w_ref, z_ref, o_ref, ab_ref):
    r = pl.program_id(1)
    n = u_ref.shape[1]

    @pl.when(r == 0)
    def _():
        for g in range(u_ref.shape[2] // FGROUP_DIM):
            cols = slice(g * FGROUP_DIM, (g + 1) * FGROUP_DIM)
            ab = jnp.dot(u_ref[0, :, cols], cs_ref[...], preferred_element_type=F32)
            ab_ref[0:n, cols] = ab[:, :FGROUP_DIM].astype(BF16)
            ab_ref[n:2 * n, cols] = ab[:, FGROUP_DIM:].astype(BF16)

    y = jnp.dot(w_ref[...], ab_ref[...], preferred_element_type=F32)
    o_ref[0] = (y * _silu(z_ref[0].astype(F32))).astype(BF16)


def _dft_cos_sin(n):
    k = jnp.arange(n, dtype=jnp.int32)
    ang = ((k[:, None] * k[None, :]) % n).astype(F32) * (2.0 * math.pi / n)
    scale = 1.0 / math.sqrt(n)
    return jnp.cos(ang) * scale, jnp.sin(ang) * scale


def _fourier(p, col0, cs_chan, w_pos, tr):
    b, n, _ = p.shape
    d = N_HEADS * V_DIM
    return pl.pallas_call(
        _fourier_kernel,
        grid=(b, n // tr),
        in_specs=[
            pl.BlockSpec((1, n, d), lambda bb, r: (bb, 0, COL_UF - col0)),
            pl.BlockSpec((FGROUP_DIM, 2 * FGROUP_DIM), lambda bb, r: (0, 0)),
            pl.BlockSpec((tr, 2 * n), lambda bb, r: (r, 0)),
            pl.BlockSpec((1, tr, d), lambda bb, r: (bb, r, COL_ZF - col0)),
        ],
        out_specs=pl.BlockSpec((1, tr, d), lambda bb, r: (bb, r, 0)),
        out_shape=jax.ShapeDtypeStruct((b, n, d), BF16),
        scratch_shapes=[pltpu.VMEM((2 * n, d), BF16)],
        compiler_params=_cparams(("parallel", "arbitrary")),
        name="fourier",
    )(p, cs_chan, w_pos, p)


def _fourier_half_kernel(u_ref, cs_ref, c_ref, s_ref, rev_ref, zlo_ref, zhi_ref, o_ref, a_ref, b_ref):
    r = pl.program_id(1)
    tr = o_ref.shape[3]

    @pl.when(r == 0)
    def _():
        for g in range(u_ref.shape[2] // FGROUP_DIM):
            cols = slice(g * FGROUP_DIM, (g + 1) * FGROUP_DIM)
            ab = jnp.dot(u_ref[0, :, cols], cs_ref[...], preferred_element_type=F32)
            a_ref[:, cols] = ab[:, :FGROUP_DIM].astype(BF16)
            b_ref[:, cols] = ab[:, FGROUP_DIM:].astype(BF16)

    p = jnp.dot(c_ref[...], a_ref[...], preferred_element_type=F32)
    q = jnp.dot(s_ref[...], b_ref[...], preferred_element_type=F32)
    o_ref[0, 0, 0] = ((p[:tr] - q[:tr]) * _silu(zlo_ref[0].astype(F32))).astype(BF16)
    hi = jnp.dot(rev_ref[...], (p + q).astype(BF16), preferred_element_type=F32)
    o_ref[0, 0, 1] = (hi * _silu(zhi_ref[0].astype(F32))).astype(BF16)


def _fourier_half_tables(n, tr):
    ext = 2 * SUBLANES
    n_tiles = n // (2 * tr)
    rows = (jnp.arange(n_tiles, dtype=jnp.int32)[:, None] * tr
            + jnp.arange(tr + ext, dtype=jnp.int32)[None, :])
    k = jnp.arange(n, dtype=jnp.int32)
    ang = ((rows[:, :, None] * k[None, None, :]) % n).astype(F32) * (2.0 * math.pi / n)
    scale = 1.0 / math.sqrt(n)
    i = jnp.arange(tr, dtype=jnp.int32)[:, None]
    j = jnp.arange(tr + ext, dtype=jnp.int32)[None, :]
    rev = (j == tr - i).astype(BF16)
    return (jnp.cos(ang) * scale).astype(BF16), (jnp.sin(ang) * scale).astype(BF16), rev


def _fourier_half(p, col0, cs_chan, tabs, tr):
    b, n, _ = p.shape
    d = N_HEADS * V_DIM
    c_t, s_t, rev = tabs
    n_tiles, rows_ext, _ = c_t.shape
    return pl.pallas_call(
        _fourier_half_kernel,
        grid=(b, n_tiles),
        in_specs=[
            pl.BlockSpec((1, n, d), lambda bb, r: (bb, 0, COL_UF - col0)),
            pl.BlockSpec((FGROUP_DIM, 2 * FGROUP_DIM), lambda bb, r: (0, 0)),
            pl.BlockSpec((None, rows_ext, n), lambda bb, r: (r, 0, 0)),
            pl.BlockSpec((None, rows_ext, n), lambda bb, r: (r, 0, 0)),
            pl.BlockSpec((tr, rows_ext), lambda bb, r: (0, 0)),
            pl.BlockSpec((1, tr, d), lambda bb, r: (bb, r, COL_ZF - col0)),
            pl.BlockSpec((1, tr, d), lambda bb, r: (bb, 2 * n_tiles - 1 - r, COL_ZF - col0)),
        ],
        out_specs=pl.BlockSpec((1, 1, 2, tr, d), lambda bb, r: (bb, r, 0, 0, 0)),
        out_shape=jax.ShapeDtypeStruct((b, n_tiles, 2, tr, d), BF16),
        scratch_shapes=[pltpu.VMEM((n, d), BF16), pltpu.VMEM((n, d), BF16)],
        compiler_params=_cparams(("parallel", "arbitrary")),
        name="fourier_half",
    )(p, cs_chan, c_t, s_t, rev, p, p)


def _sigmoid(z):
    return 0.5 * jnp.tanh(0.5 * z) + 0.5


def _silu(z):
    h = 0.5 * z
    return h * jnp.tanh(h) + h


def _merge_kernel(ya_ref, yf_ref, xin_ref, bg_ref, cg_ref, zc_ref,
                  ga_ref, gc_ref, gf_ref, xin_p_ref, cg_p_ref, xin_n_ref, cg_n_ref,
                  x_ref, gate_ref, cw_ref, wa_ref, wc_ref, wf_ref, wo_ref, *rest, mode):
    o_ref = rest[-2] if mode == "next_norm" else rest[-1]
    i = pl.program_id(1)
    tm = x_ref.shape[1]
    pad = SUBLANES

    def branch(y_gated, w_ref, g_ref):
        t = jnp.dot(y_gated, w_ref[...], preferred_element_type=F32)
        return _sigmoid(g_ref[0].astype(F32)) * t

    t_a = jnp.dot(ya_ref[0], wa_ref[...], preferred_element_type=F32)
    merged_af = (_sigmoid(ga_ref[0].astype(F32)) * t_a
                 + branch(yf_ref[...], wf_ref, gf_ref))
    bits = pltpu.bitcast(t_a[0:SUBLANES, 0:LANES], jnp.uint32)
    zero = pltpu.bitcast((bits >> 16) >> 16, F32)[0:1, 0:1]

    u = cg_ref[0].astype(F32) * xin_ref[0].astype(F32)
    up = cg_p_ref[0, pad - 1:pad, :].astype(F32) * xin_p_ref[0, pad - 1:pad, :].astype(F32)
    un = cg_n_ref[0, 0:1, :].astype(F32) * xin_n_ref[0, 0:1, :].astype(F32)
    up = jnp.where(i == 0, 0.0, up)
    un = jnp.where(i == pl.num_programs(1) - 1, 0.0, un)
    row = lax.broadcasted_iota(jnp.int32, (tm, 1), 0)
    u_prev = jnp.where(row == 0, up, pltpu.roll(u, 1, axis=0))
    u_next = jnp.where(row == tm - 1, un, pltpu.roll(u, tm - 1, axis=0))
    cw = cw_ref[...] + zero
    conv = cw[0:1] * u_prev + cw[1:2] * u + cw[2:3] * u_next
    y_c = bg_ref[0].astype(F32) * conv
    yc_gated = (y_c * _silu(zc_ref[0].astype(F32))).astype(BF16)
    merged = merged_af + branch(yc_gated, wc_ref, gc_ref)
    out = jnp.dot(merged.astype(BF16), wo_ref[...], preferred_element_type=F32)
    x_new = x_ref[0] + gate_ref[0] * out
    if mode == "plain":
        o_ref[0] = x_new
        return
    y = x_new * lax.rsqrt(jnp.mean(x_new * x_new, axis=-1, keepdims=True) + NORM_EPS)
    if mode == "final_norm":
        o_ref[0] = y * rest[0][...]
    else:
        nw_ref, shn_ref, scn_ref, _, hout_ref = rest
        o_ref[0] = x_new
        hout_ref[0] = ((y * nw_ref[...]) * (1.0 + scn_ref[0]) + shn_ref[0]).astype(BF16)


def _merge(x, p, col0, ya, yf, yf_spec, mod3, mod_row, conv_w, wa, wc, wf, wo, layer, tm,
           final_norm_w=None, next_norm=None):
    bx, sx, d = x.shape
    nb = tm // SUBLANES
    last_halo = sx // SUBLANES - 1

    def whole(b, i):
        return (b, i, 0)

    def col(cb):
        return pl.BlockSpec((1, tm, d), lambda b, i: (b, i, cb - col0))

    def halo_prev(cb):
        return pl.BlockSpec((1, SUBLANES, d),
                            lambda b, i: (b, jnp.maximum(i * nb - 1, 0), cb - col0))

    def halo_next(cb):
        return pl.BlockSpec((1, SUBLANES, d),
                            lambda b, i: (b, jnp.minimum((i + 1) * nb, last_halo), cb - col0))

    def weight():
        return pl.BlockSpec((None, d, d), lambda b, i: (layer, 0, 0), pipeline_mode=pl.Buffered(1))

    in_specs = [
        pl.BlockSpec((1, tm, d), whole), yf_spec,
        col(COL_XIN), col(COL_BG), col(COL_CG), col(COL_ZC),
        col(COL_GL), col(COL_GL + 1), col(COL_GL + 2),
        halo_prev(COL_XIN), halo_prev(COL_CG), halo_next(COL_XIN), halo_next(COL_CG),
        pl.BlockSpec((1, tm, d), whole),
        pl.BlockSpec((1, 1, d), lambda b, i: (mod_row(b), 0, 2)),
        pl.BlockSpec((None, 3, d), lambda b, i: (layer, 0, 0)),
        weight(), weight(), weight(), weight(),
    ]
    args = [ya, yf, p, p, p, p, p, p, p, p, p, p, p, x, mod3, conv_w, wa, wc, wf, wo]
    out_specs = pl.BlockSpec((1, tm, d), whole)
    out_shape = jax.ShapeDtypeStruct((bx, sx, d), F32)
    mode = "plain"
    if final_norm_w is not None:
        mode = "final_norm"
        in_specs.append(pl.BlockSpec((1, d), lambda b, i: (0, 0)))
        args.append(final_norm_w)
    elif next_norm is not None:
        mode = "next_norm"
        norm_w, mod3_next = next_norm
        in_specs += [
            pl.BlockSpec((None, 1, d), lambda b, i: (layer + 1, 0, 0)),
            pl.BlockSpec((1, 1, d), lambda b, i: (mod_row(b), 0, 0)),
            pl.BlockSpec((1, 1, d), lambda b, i: (mod_row(b), 0, 1)),
        ]
        args += [norm_w, mod3_next, mod3_next]
        out_specs = [out_specs, pl.BlockSpec((1, tm, d), whole)]
        out_shape = [out_shape, jax.ShapeDtypeStruct((bx, sx, d), BF16)]
    return pl.pallas_call(
        functools.partial(_merge_kernel, mode=mode),
        grid=(bx, sx // tm),
        in_specs=in_specs,
        out_specs=out_specs,
        out_shape=out_shape,
        compiler_params=_cparams(("parallel", "parallel")),
        name="merge",
    )(*args)


def _rope_tables(n_tokens):
    axis_dim = HEAD_DIM // 2
    rows = n_tokens // GRID_W
    row = jnp.repeat(jnp.arange(rows), GRID_W).astype(F32)
    col = jnp.tile(jnp.arange(GRID_W), rows).astype(F32)
    inv_freq = ROPE_BASE ** (-jnp.arange(0, axis_dim, 2, dtype=F32) / axis_dim)
    ang_r = row[:, None] * inv_freq
    ang_c = col[:, None] * inv_freq
    ang = jnp.concatenate([ang_r, ang_c] * 4, axis=-1)
    sign = jnp.where(jnp.arange(V_DIM) < V_DIM // 2, -1.0, 1.0).astype(F32)
    cos = jnp.cos(ang)
    sin = jnp.sin(ang) * sign
    return jnp.stack([cos * Q_SCALE, cos]), jnp.stack([sin * Q_SCALE, sin])


def _head_layout(w_qk):
    depth, d, w = w_qk.shape
    t = w_qk.reshape(depth, d, w // V_DIM, 2, 2, 2, HEAD_DIM // 4)
    return t.transpose(0, 1, 2, 5, 3, 4, 6).reshape(depth, d, w)


def _fourier_tables(n):
    cn, sn = _dft_cos_sin(n)
    return jnp.concatenate([cn, -sn], axis=1).astype(BF16)


def _pick_tile(n, target):
    t = min(n, target)
    while n % t:
        t //= 2
    return t


def kernel(x, c, ctx, c_ctx, norm_w, w_mod, b_mod, w_in, lambda_qk, subln_w, conv_w,
           w_attn_o, w_conv_o, w_four_o, w_out, final_norm_w):
    b, s, d = x.shape
    lc = ctx.shape[1]
    depth = w_in.shape[0]
    assert d == N_HEADS * V_DIM and w_in.shape[2] == N_PROJ_BLOCKS * d
    assert s % GRID_W == 0 and s % LANES == 0 and lc % SUBLANES == 0

    pad = (-(b + 1)) % MOD_ROWS_PAD
    cond = jnp.concatenate([c, c_ctx[None, :], jnp.zeros((pad, d), F32)], axis=0)
    mod = _modulation(cond, w_mod, b_mod)
    n_rows = cond.shape[0]

    norm_w = norm_w.reshape(depth, 1, d)
    subln_w = subln_w.reshape(depth, 1, V_DIM)
    n_qk = 2 * d
    w_in_b = w_in.astype(BF16)
    w_in_b = w_in_b.at[:, :, :n_qk].set(_head_layout(w_in_b[:, :, :n_qk]))
    wa_b, wc_b, wf_b, wo_b = (w.astype(BF16) for w in (w_attn_o, w_conv_o, w_four_o, w_out))

    rope_tabs = _rope_tables(s)
    cc, sc = _dft_cos_sin(FGROUP_DIM)
    cs_chan = jnp.concatenate([cc, sc], axis=1).astype(BF16)
    w_pos_ctx = _fourier_tables(lc)

    ctx_flat = ctx.reshape(1, b * lc, d)
    lat_row = lambda bb: bb
    ctx_row = lambda bb: b

    tm_lat = _pick_tile(s, 2048)
    tm_ctx = _pick_tile(b * lc, 2048)
    tmm_lat = _pick_tile(s // 2, 512)
    tmm_ctx = _pick_tile(lc, 256)
    four_tabs = _fourier_half_tables(s, tmm_lat)
    n_ftiles = s // (2 * tmm_lat)
    yf_lat_spec = pl.BlockSpec(
        (None, None, None, tmm_lat, d),
        lambda bb, i: (bb, jnp.where(i < n_ftiles, i, 2 * n_ftiles - 1 - i),
                       jnp.where(i < n_ftiles, 0, 1), 0, 0))
    yf_ctx_spec = pl.BlockSpec((None, tmm_ctx, d), lambda bb, i: (bb, i, 0))

    h_lat = None
    for l in range(depth):
        last = l == depth - 1
        lam_init = 0.8 - 0.6 * math.exp(-0.3 * l)
        mod3 = mod[l].reshape(n_rows, 1, 3 * d)

        if h_lat is None:
            p_lat, h_lat = _proj(x, mod3, lat_row, norm_w, w_in_b, l, COL_V, N_PROJ_BLOCKS - COL_V,
                                 tm_lat, emit_h=True)
        else:
            p_lat = _proj_rest(h_lat, w_in_b, l, COL_V, N_PROJ_BLOCKS - COL_V, tm_lat)
        p_qk = _proj_qk(h_lat, w_in_b, l, rope_tabs, tm_lat)
        c0, cn = (COL_K, 2) if last else (COL_Q, N_PROJ_BLOCKS)
        p_ctx = _proj(ctx_flat, mod3, ctx_row, norm_w, w_in_b, l, c0, cn, tm_ctx).reshape(b, lc, -1)

        ya = _attention(lambda_qk, subln_w, l, lam_init, (p_qk, COL_Q), (p_qk, COL_K),
                        (p_lat, 0), (p_lat, COL_ZA - COL_V),
                        (p_ctx, COL_K - c0), (p_ctx, COL_V - c0))
        yf = _fourier_half(p_lat, COL_V, cs_chan, four_tabs, tmm_lat)
        if last:
            x_new = _merge(x, p_lat, COL_V, ya, yf, yf_lat_spec, mod3, lat_row, conv_w, wa_b, wc_b,
                           wf_b, wo_b, l, tmm_lat, final_norm_w=final_norm_w.reshape(1, d))
        else:
            mod3_next = mod[l + 1].reshape(n_rows, 1, 3 * d)
            x_new, h_lat = _merge(x, p_lat, COL_V, ya, yf, yf_lat_spec, mod3, lat_row, conv_w, wa_b,
                                  wc_b, wf_b, wo_b, l, tmm_lat, next_norm=(norm_w, mod3_next))

        if not last:
            yac = _self_attention(lambda_qk, subln_w, l, lam_init, p_ctx, COL_Q)
            yfc = _fourier(p_ctx, COL_Q, cs_chan, w_pos_ctx, lc)
            ctx3 = ctx_flat.reshape(b, lc, d)
            ctx_flat = _merge(ctx3, p_ctx, COL_Q, yac, yfc, yf_ctx_spec, mod3, ctx_row, conv_w, wa_b, wc_b, wf_b,
                              wo_b, l, tmm_ctx).reshape(1, b * lc, d)
        x = x_new

    return x
```

```python
import functools
import math

import jax
import jax.numpy as jnp
from jax import lax
from jax.experimental import pallas as pl
from jax.experimental.pallas import tpu as pltpu

F32 = jnp.float32
BF16 = jnp.bfloat16

N_HEADS = 8
HEAD_DIM = 64
V_DIM = 2 * HEAD_DIM
FGROUP_DIM = 128
GRID_W = 64
ROPE_BASE = 10000.0
NORM_EPS = 1e-6
SUBLN_EPS = 1e-5
N_PROJ_BLOCKS = 13
COL_Q, COL_K, COL_V, COL_ZA, COL_XIN, COL_BG, COL_CG, COL_ZC, COL_UF, COL_ZF, COL_GL = range(11)

LANES = 128
SUBLANES = 8
VMEM_LIMIT_BYTES = 56 * 1024 * 1024
MOD_ROWS_PAD = 8
ATTN_SUB_ROWS = 256
ATTN_CARRY_BLOCKS = 2
ATTN_HEADS_PER_STEP = 2
Q_SCALE = math.log2(math.e) / math.sqrt(HEAD_DIM)


def _cparams(sem):
    return pltpu.CompilerParams(dimension_semantics=sem, vmem_limit_bytes=VMEM_LIMIT_BYTES)


def _mod_kernel(cond_ref, w_ref, b_ref, o_ref):
    cond = cond_ref[...]
    a = cond * jax.nn.sigmoid(cond)
    o_ref[...] = jnp.dot(a, w_ref[...], preferred_element_type=F32,
                         precision=lax.Precision.HIGHEST) + b_ref[...]


def _modulation(cond, w_mod, b_mod):
    depth, d, w3 = w_mod.shape
    rows = cond.shape[0]
    tn = d
    return pl.pallas_call(
        _mod_kernel,
        grid=(depth, w3 // tn),
        in_specs=[
            pl.BlockSpec((rows, d), lambda l, j: (0, 0)),
            pl.BlockSpec((None, d, tn), lambda l, j: (l, 0, j)),
            pl.BlockSpec((None, 1, tn), lambda l, j: (l, 0, j)),
        ],
        out_specs=pl.BlockSpec((None, rows, tn), lambda l, j: (l, 0, j)),
        out_shape=jax.ShapeDtypeStruct((depth, rows, w3), F32),
        compiler_params=_cparams(("parallel", "parallel")),
        name="modulation",
    )(cond, w_mod, b_mod.reshape(depth, 1, w3))


def _store_block(acc, o_ref, j, rope_refs, scale_first):
    if rope_refs is not None:
        cos = rope_refs[0][...]
        sin = rope_refs[1][...]
        for hh in range(acc.shape[1] // LANES):
            a = acc[:, hh * LANES:(hh + 1) * LANES]
            r = a * cos + pltpu.roll(a, LANES // 2, axis=1) * sin
            o_ref[0, :, hh * LANES:(hh + 1) * LANES] = r.astype(BF16)
    elif scale_first:
        scale = jnp.where(j == 0, Q_SCALE, 1.0).astype(F32)
        o_ref[0] = (acc * scale).astype(BF16)
    else:
        o_ref[0] = acc.astype(BF16)


def _proj_kernel(x_ref, sh_ref, sc_ref, nw_ref, w_ref, o_ref, *rest, scale_first):
    h_ref = rest[-1]
    j = pl.program_id(2)

    @pl.when(j == 0)
    def _():
        x = x_ref[0]
        y = x * lax.rsqrt(jnp.mean(x * x, axis=-1, keepdims=True) + NORM_EPS)
        h = ((y * nw_ref[...]) * (1.0 + sc_ref[0]) + sh_ref[0]).astype(BF16)
        h_ref[...] = h
        if len(rest) == 2:
            rest[0][0] = h

    acc = jnp.dot(h_ref[...], w_ref[...], preferred_element_type=F32)
    _store_block(acc, o_ref, j, None, scale_first)


def _proj(x, mod3, mod_row, norm_w, w_in_b, layer, col0, ncols, tm, emit_h=False):
    bx, sx, d = x.shape
    tn = d
    in_specs = [
        pl.BlockSpec((1, tm, d), lambda b, i, j: (b, i, 0)),
        pl.BlockSpec((1, 1, d), lambda b, i, j: (mod_row(b), 0, 0)),
        pl.BlockSpec((1, 1, d), lambda b, i, j: (mod_row(b), 0, 1)),
        pl.BlockSpec((None, 1, d), lambda b, i, j: (layer, 0, 0)),
        pl.BlockSpec((None, d, tn), lambda b, i, j: (layer, 0, col0 + j)),
    ]
    out_specs = pl.BlockSpec((1, tm, tn), lambda b, i, j: (b, i, j))
    out_shape = jax.ShapeDtypeStruct((bx, sx, ncols * tn), BF16)
    if emit_h:
        out_specs = [out_specs, pl.BlockSpec((1, tm, d), lambda b, i, j: (b, i, 0))]
        out_shape = [out_shape, jax.ShapeDtypeStruct((bx, sx, d), BF16)]
    return pl.pallas_call(
        functools.partial(_proj_kernel, scale_first=(col0 == COL_Q)),
        grid=(bx, sx // tm, ncols),
        in_specs=in_specs,
        out_specs=out_specs,
        out_shape=out_shape,
        scratch_shapes=[pltpu.VMEM((tm, d), BF16)],
        compiler_params=_cparams(("parallel", "parallel", "arbitrary")),
        name="proj",
    )(x, mod3, mod3, norm_w, w_in_b)


def _proj_rest_kernel(h_ref, w_ref, o_ref):
    o_ref[0] = jnp.dot(h_ref[0], w_ref[...], preferred_element_type=F32).astype(BF16)


def _proj_rest(h, w_in_b, layer, col0, ncols, tm):
    bx, sx, d = h.shape
    tn = d
    return pl.pallas_call(
        _proj_rest_kernel,
        grid=(bx, sx // tm, ncols),
        in_specs=[
            pl.BlockSpec((1, tm, d), lambda b, i, j: (b, i, 0)),
            pl.BlockSpec((None, d, tn), lambda b, i, j: (layer, 0, col0 + j)),
        ],
        out_specs=pl.BlockSpec((1, tm, tn), lambda b, i, j: (b, i, j)),
        out_shape=jax.ShapeDtypeStruct((bx, sx, ncols * tn), BF16),
        compiler_params=_cparams(("parallel", "parallel", "arbitrary")),
        name="proj_rest",
    )(h, w_in_b)


def _proj_qk_kernel(h_ref, w_ref, cos_ref, sin_ref, o_ref):
    acc = jnp.dot(h_ref[0], w_ref[...], preferred_element_type=F32)
    _store_block(acc, o_ref, pl.program_id(2), (cos_ref, sin_ref), False)


def _proj_qk(h, w_in_b, layer, rope_tabs, tm):
    bx, sx, d = h.shape
    tn = d
    cos_t, sin_t = rope_tabs
    tab_spec = pl.BlockSpec((None, tm, LANES), lambda b, i, j: (j, i, 0))
    return pl.pallas_call(
        _proj_qk_kernel,
        grid=(bx, sx // tm, 2),
        in_specs=[
            pl.BlockSpec((1, tm, d), lambda b, i, j: (b, i, 0)),
            pl.BlockSpec((None, d, tn), lambda b, i, j: (layer, 0, COL_Q + j)),
            tab_spec, tab_spec,
        ],
        out_specs=pl.BlockSpec((1, tm, tn), lambda b, i, j: (b, i, j)),
        out_shape=jax.ShapeDtypeStruct((bx, sx, 2 * tn), BF16),
        compiler_params=_cparams(("parallel", "parallel", "arbitrary")),
        name="proj_rope",
    )(h, w_in_b, cos_t, sin_t)


def _attn_kernel(lq_ref, sw_ref, q_ref, k_ref, v_ref, z_ref, kc_ref, vc_ref, qn_ref, kn_ref, kcn_ref,
                 o_ref, kall_ref, vext_ref, kalln_ref, scarry_ref, *, lam_init):
    n_lat = k_ref.shape[1]
    n_heads, n_all = kall_ref.shape[0], kall_ref.shape[1]

    def head(ref, g, row_slice=slice(None)):
        return ref[0, row_slice, g * V_DIM:(g + 1) * V_DIM]

    ones_lane = lax.broadcasted_iota(jnp.int32, (n_all, V_DIM), 1) == 0
    for g in range(n_heads):
        kall_ref[g, 0:n_lat, :] = head(k_ref, g)
        kall_ref[g, n_lat:n_all, :] = head(kc_ref, g)
        vext_ref[g, 0:n_lat, 0:V_DIM] = head(v_ref, g)
        vext_ref[g, n_lat:n_all, 0:V_DIM] = head(vc_ref, g)
        vext_ref[g, :, V_DIM:2 * V_DIM] = jnp.where(ones_lane, 1.0, 0.0).astype(BF16)
    kalln_ref[0:n_lat, :] = kn_ref[0]
    kalln_ref[n_lat:n_all, :] = kcn_ref[0]

    lq = lq_ref[...].astype(F32)
    lam = (jnp.exp(jnp.sum(lq[0:1] * lq[1:2], axis=-1, keepdims=True))
           - jnp.exp(jnp.sum(lq[2:3] * lq[3:4], axis=-1, keepdims=True)) + lam_init)

    tq = q_ref.shape[1]
    sub = min(tq, ATTN_SUB_ROWS)
    n_sub = tq // sub
    lane = lax.broadcasted_iota(jnp.int32, (sub, V_DIM), 1)

    def scores(q, keys):
        zero = jnp.zeros_like(q)
        map1 = (lane % HEAD_DIM) < HEAD_DIM // 2
        qq = jnp.concatenate([jnp.where(map1, q, zero), jnp.where(map1, zero, q)], axis=0)
        return lax.dot_general(qq, keys, (((1,), (1,)), ((), ())), preferred_element_type=F32)

    def block_scores(t):
        g, j = divmod(t, n_sub)
        rows = slice(j * sub, (j + 1) * sub)
        if g < n_heads:
            return scores(head(q_ref, g, rows), kall_ref[g])
        return scores(qn_ref[0, rows, :], kalln_ref[...])

    def finish(t, get_s):
        g, j = divmod(t, n_sub)
        rows = slice(j * sub, (j + 1) * sub)
        m = jnp.max(get_s(), axis=-1, keepdims=True)
        p = jnp.exp2(get_s() - m).astype(BF16)
        nd = jnp.dot(p, vext_ref[g], preferred_element_type=F32)
        o = nd[:, 0:V_DIM] / nd[:, V_DIM:V_DIM + 1]
        o = o[:sub] - lam * o[sub:]
        y = o * lax.rsqrt(jnp.mean(o * o, axis=-1, keepdims=True) + SUBLN_EPS)
        y = (y * sw_ref[...]) * (1.0 - lam_init)
        z = head(z_ref, g, rows).astype(F32)
        o_ref[0, rows, g * V_DIM:(g + 1) * V_DIM] = (y * _silu(z)).astype(BF16)

    def value(v):
        return lambda: v

    depth = scarry_ref.shape[0]

    @pl.when((pl.program_id(0) == 0) & (pl.program_id(1) == 0))
    def _():
        for d in range(depth):
            scarry_ref[d] = block_scores(d)

    pending = [(lambda d=d: scarry_ref[d]) for d in range(depth)]

    for t in range(n_heads * n_sub):
        pending.append(value(block_scores(t + depth)))
        finish(t, pending.pop(0))
    for d in range(depth):
        scarry_ref[d] = pending[d]()


def _self_attn_kernel(lq_ref, sw_ref, q_ref, k_ref, v_ref, z_ref, o_ref, *, lam_init):
    n = q_ref.shape[1]
    lq = lq_ref[...].astype(F32)
    lam = (jnp.exp(jnp.sum(lq[0:1] * lq[1:2], axis=-1, keepdims=True))
           - jnp.exp(jnp.sum(lq[2:3] * lq[3:4], axis=-1, keepdims=True)) + lam_init)
    lane = lax.broadcasted_iota(jnp.int32, (n, V_DIM), 1)
    map1 = (lane % HEAD_DIM) < HEAD_DIM // 2
    ones_cols = jnp.where(lane == 0, 1.0, 0.0).astype(BF16)
    heads = [slice(h * V_DIM, (h + 1) * V_DIM) for h in range(q_ref.shape[2] // V_DIM)]

    def scores(cols):
        q = q_ref[0, :, cols]
        zero = jnp.zeros_like(q)
        qq = jnp.concatenate([jnp.where(map1, q, zero), jnp.where(map1, zero, q)], axis=0)
        return lax.dot_general(qq, k_ref[0, :, cols], (((1,), (1,)), ((), ())),
                               preferred_element_type=F32)

    s_all = [scores(cols) for cols in heads]
    p_all = [jnp.exp2(s - jnp.max(s, axis=-1, keepdims=True)).astype(BF16) for s in s_all]
    nd_all = [jnp.dot(p, jnp.concatenate([v_ref[0, :, cols], ones_cols], axis=1),
                      preferred_element_type=F32) for p, cols in zip(p_all, heads)]
    for nd, cols in zip(nd_all, heads):
        o = nd[:, 0:V_DIM] / nd[:, V_DIM:V_DIM + 1]
        o = o[:n] - lam * o[n:]
        y = o * lax.rsqrt(jnp.mean(o * o, axis=-1, keepdims=True) + SUBLN_EPS)
        y = (y * sw_ref[...]) * (1.0 - lam_init)
        o_ref[0, :, cols] = (y * _silu(z_ref[0, :, cols].astype(F32))).astype(BF16)


def _self_attention(lambda_qk, subln_w, layer, lam_init, p, col0):
    b, n, _ = p.shape
    d = N_HEADS * V_DIM

    def col(cb):
        return pl.BlockSpec((1, n, d), lambda bb: (bb, 0, cb - col0))

    return pl.pallas_call(
        functools.partial(_self_attn_kernel, lam_init=lam_init),
        grid=(b,),
        in_specs=[
            pl.BlockSpec((None, 4, HEAD_DIM), lambda bb: (layer, 0, 0)),
            pl.BlockSpec((None, 1, V_DIM), lambda bb: (layer, 0, 0)),
            col(COL_Q), col(COL_K), col(COL_V), col(COL_ZA),
        ],
        out_specs=pl.BlockSpec((1, n, d), lambda bb: (bb, 0, 0)),
        out_shape=jax.ShapeDtypeStruct((b, n, d), BF16),
        compiler_params=_cparams(("parallel",)),
        name="attn_self",
    )(lambda_qk, subln_w, p, p, p, p)


def _attention(lambda_qk, subln_w, layer, lam_init, q, k, v, z, kc, vc):
    b, s, _ = q[0].shape
    lc = kc[0].shape[1]
    n_keys = s + lc
    nh = N_HEADS
    gh = ATTN_HEADS_PER_STEP
    n_groups = nh // gh
    sub = min(s, ATTN_SUB_ROWS)
    depth = min(ATTN_CARRY_BLOCKS, s // sub)

    def group_cols(rows, col):
        return pl.BlockSpec((1, rows, gh * V_DIM), lambda bb, g: (bb, 0, col * n_groups + g))

    def next_head_cols(rows, col):
        def index(bb, g):
            flat = jnp.minimum(bb * n_groups + g + 1, b * n_groups - 1)
            return (flat // n_groups, 0, col * nh + (flat % n_groups) * gh)
        return pl.BlockSpec((1, rows, V_DIM), index)

    in_specs = [
        pl.BlockSpec((None, 4, HEAD_DIM), lambda bb, g: (layer, 0, 0)),
        pl.BlockSpec((None, 1, V_DIM), lambda bb, g: (layer, 0, 0)),
        group_cols(s, q[1]), group_cols(s, k[1]), group_cols(s, v[1]), group_cols(s, z[1]),
        group_cols(lc, kc[1]), group_cols(lc, vc[1]),
        next_head_cols(depth * sub, q[1]), next_head_cols(s, k[1]), next_head_cols(lc, kc[1]),
    ]
    return pl.pallas_call(
        functools.partial(_attn_kernel, lam_init=lam_init),
        grid=(b, n_groups),
        in_specs=in_specs,
        out_specs=group_cols(s, 0),
        out_shape=jax.ShapeDtypeStruct((b, s, nh * V_DIM), BF16),
        scratch_shapes=[
            pltpu.VMEM((gh, n_keys, V_DIM), BF16),
            pltpu.VMEM((gh, n_keys, 2 * V_DIM), BF16),
            pltpu.VMEM((n_keys, V_DIM), BF16),
            pltpu.VMEM((depth, 2 * sub, n_keys), F32),
        ],
        compiler_params=_cparams(("arbitrary", "arbitrary")),
        name="attn_latent",
    )(lambda_qk, subln_w, q[0], k[0], v[0], z[0], kc[0], vc[0], q[0], k[0], kc[0])


def _fourier_kernel(u_ref, cs_ref, w_ref, z_ref, o_ref, ab_ref):
    r = pl.program_id(1)
    n = u_ref.shape[1]

    @pl.when(r == 0)
    def _():
        for g in range(u_ref.shape[2] // FGROUP_DIM):
            cols = slice(g * FGROUP_DIM, (g + 1) * FGROUP_DIM)
            ab = jnp.dot(u_ref[0, :, cols], cs_ref[...], preferred_element_type=F32)
            ab_ref[0:n, cols] = ab[:, :FGROUP_DIM].astype(BF16)
            ab_ref[n:2 * n, cols] = ab[:, FGROUP_DIM:].astype(BF16)

    y = jnp.dot(w_ref[...], ab_ref[...], preferred_element_type=F32)
    o_ref[0] = (y * _silu(z_ref[0].astype(F32))).astype(BF16)


def _dft_cos_sin(n):
    k = jnp.arange(n, dtype=jnp.int32)
    ang = ((k[:, None] * k[None, :]) % n).astype(F32) * (2.0 * math.pi / n)
    scale = 1.0 / math.sqrt(n)
    return jnp.cos(ang) * scale, jnp.sin(ang) * scale


def _fourier(p, col0, cs_chan, w_pos, tr):
    b, n, _ = p.shape
    d = N_HEADS * V_DIM
    return pl.pallas_call(
        _fourier_kernel,
        grid=(b, n // tr),
        in_specs=[
            pl.BlockSpec((1, n, d), lambda bb, r: (bb, 0, COL_UF - col0)),
            pl.BlockSpec((FGROUP_DIM, 2 * FGROUP_DIM), lambda bb, r: (0, 0)),
            pl.BlockSpec((tr, 2 * n), lambda bb, r: (r, 0)),
            pl.BlockSpec((1, tr, d), lambda bb, r: (bb, r, COL_ZF - col0)),
        ],
        out_specs=pl.BlockSpec((1, tr, d), lambda bb, r: (bb, r, 0)),
        out_shape=jax.ShapeDtypeStruct((b, n, d), BF16),
        scratch_shapes=[pltpu.VMEM((2 * n, d), BF16)],
        compiler_params=_cparams(("parallel", "arbitrary")),
        name="fourier",
    )(p, cs_chan, w_pos, p)


def _fourier_half_kernel(u_ref, cs_ref, c_ref, s_ref, rev_ref, zlo_ref, zhi_ref, o_ref, a_ref, b_ref):
    r = pl.program_id(1)
    tr = o_ref.shape[3]

    @pl.when(r == 0)
    def _():
        for g in range(u_ref.shape[2] // FGROUP_DIM):
            cols = slice(g * FGROUP_DIM, (g + 1) * FGROUP_DIM)
            ab = jnp.dot(u_ref[0, :, cols], cs_ref[...], preferred_element_type=F32)
            a_ref[:, cols] = ab[:, :FGROUP_DIM].astype(BF16)
            b_ref[:, cols] = ab[:, FGROUP_DIM:].astype(BF16)

    p = jnp.dot(c_ref[...], a_ref[...], preferred_element_type=F32)
    q = jnp.dot(s_ref[...], b_ref[...], preferred_element_type=F32)
    o_ref[0, 0, 0] = ((p[:tr] - q[:tr]) * _silu(zlo_ref[0].astype(F32))).astype(BF16)
    src = (p + q).astype(BF16)
    hi = jnp.dot(rev_ref[...], src[:tr], preferred_element_type=F32)
    first = lax.broadcasted_iota(jnp.int32, (tr, 1), 0) == 0
    hi = jnp.where(first, src[tr:tr + 1].astype(F32), hi)
    o_ref[0, 0, 1] = (hi * _silu(zhi_ref[0].astype(F32))).astype(BF16)


def _fourier_half_tables(n, tr):
    ext = 2 * SUBLANES
    n_tiles = n // (2 * tr)
    rows = (jnp.arange(n_tiles, dtype=jnp.int32)[:, None] * tr
            + jnp.arange(tr + ext, dtype=jnp.int32)[None, :])
    k = jnp.arange(n, dtype=jnp.int32)
    ang = ((rows[:, :, None] * k[None, None, :]) % n).astype(F32) * (2.0 * math.pi / n)
    scale = 1.0 / math.sqrt(n)
    i = jnp.arange(tr, dtype=jnp.int32)[:, None]
    j = jnp.arange(tr, dtype=jnp.int32)[None, :]
    rev = (j == tr - i).astype(BF16)
    return (jnp.cos(ang) * scale).astype(BF16), (jnp.sin(ang) * scale).astype(BF16), rev


def _fourier_half(p, col0, cs_chan, tabs, tr):
    b, n, _ = p.shape
    d = N_HEADS * V_DIM
    c_t, s_t, rev = tabs
    n_tiles, rows_ext, _ = c_t.shape
    return pl.pallas_call(
        _fourier_half_kernel,
        grid=(b, n_tiles),
        in_specs=[
            pl.BlockSpec((1, n, d), lambda bb, r: (bb, 0, COL_UF - col0)),
            pl.BlockSpec((FGROUP_DIM, 2 * FGROUP_DIM), lambda bb, r: (0, 0)),
            pl.BlockSpec((None, rows_ext, n), lambda bb, r: (r, 0, 0)),
            pl.BlockSpec((None, rows_ext, n), lambda bb, r: (r, 0, 0)),
            pl.BlockSpec((tr, tr), lambda bb, r: (0, 0)),
            pl.BlockSpec((1, tr, d), lambda bb, r: (bb, r, COL_ZF - col0)),
            pl.BlockSpec((1, tr, d), lambda bb, r: (bb, 2 * n_tiles - 1 - r, COL_ZF - col0)),
        ],
        out_specs=pl.BlockSpec((1, 1, 2, tr, d), lambda bb, r: (bb, r, 0, 0, 0)),
        out_shape=jax.ShapeDtypeStruct((b, n_tiles, 2, tr, d), BF16),
        scratch_shapes=[pltpu.VMEM((n, d), BF16), pltpu.VMEM((n, d), BF16)],
        compiler_params=_cparams(("parallel", "arbitrary")),
        name="fourier_half",
    )(p, cs_chan, c_t, s_t, rev, p, p)


def _sigmoid(z):
    return 0.5 * jnp.tanh(0.5 * z) + 0.5


def _silu(z):
    h = 0.5 * z
    return h * jnp.tanh(h) + h


def _merge_kernel(ya_ref, yf_ref, xin_ref, bg_ref, cg_ref, zc_ref,
                  ga_ref, gc_ref, gf_ref, xin_p_ref, cg_p_ref, xin_n_ref, cg_n_ref,
                  x_ref, gate_ref, cw_ref, wa_ref, wc_ref, wf_ref, wo_ref, *rest, mode):
    o_ref = rest[-2] if mode == "next_norm" else rest[-1]
    i = pl.program_id(1)
    tm = x_ref.shape[1]
    pad = SUBLANES

    def branch(y_gated, w_ref, g_ref):
        t = jnp.dot(y_gated, w_ref[...], preferred_element_type=F32)
        return _sigmoid(g_ref[0].astype(F32)) * t

    t_a = jnp.dot(ya_ref[0], wa_ref[...], preferred_element_type=F32)
    merged_af = (_sigmoid(ga_ref[0].astype(F32)) * t_a
                 + branch(yf_ref[...], wf_ref, gf_ref))
    bits = pltpu.bitcast(t_a[0:SUBLANES, 0:LANES], jnp.uint32)
    zero = pltpu.bitcast((bits >> 16) >> 16, F32)[0:1, 0:1]

    u = cg_ref[0].astype(F32) * xin_ref[0].astype(F32)
    up = cg_p_ref[0, pad - 1:pad, :].astype(F32) * xin_p_ref[0, pad - 1:pad, :].astype(F32)
    un = cg_n_ref[0, 0:1, :].astype(F32) * xin_n_ref[0, 0:1, :].astype(F32)
    up = jnp.where(i == 0, 0.0, up)
    un = jnp.where(i == pl.num_programs(1) - 1, 0.0, un)
    row = lax.broadcasted_iota(jnp.int32, (tm, 1), 0)
    u_prev = jnp.where(row == 0, up, pltpu.roll(u, 1, axis=0))
    u_next = jnp.where(row == tm - 1, un, pltpu.roll(u, tm - 1, axis=0))
    cw = cw_ref[...] + zero
    conv = cw[0:1] * u_prev + cw[1:2] * u + cw[2:3] * u_next
    y_c = bg_ref[0].astype(F32) * conv
    yc_gated = (y_c * _silu(zc_ref[0].astype(F32))).astype(BF16)
    merged = merged_af + branch(yc_gated, wc_ref, gc_ref)
    out = jnp.dot(merged.astype(BF16), wo_ref[...], preferred_element_type=F32)
    x_new = x_ref[0] + gate_ref[0] * out
    if mode == "plain":
        o_ref[0] = x_new
        return
    y = x_new * lax.rsqrt(jnp.mean(x_new * x_new, axis=-1, keepdims=True) + NORM_EPS)
    if mode == "final_norm":
        o_ref[0] = y * rest[0][...]
    else:
        nw_ref, shn_ref, scn_ref, _, hout_ref = rest
        o_ref[0] = x_new
        hout_ref[0] = ((y * nw_ref[...]) * (1.0 + scn_ref[0]) + shn_ref[0]).astype(BF16)


def _merge(x, p, col0, ya, yf, yf_spec, mod3, mod_row, conv_w, wa, wc, wf, wo, layer, tm,
           final_norm_w=None, next_norm=None):
    bx, sx, d = x.shape
    nb = tm // SUBLANES
    last_halo = sx // SUBLANES - 1

    def whole(b, i):
        return (b, i, 0)

    def col(cb):
        return pl.BlockSpec((1, tm, d), lambda b, i: (b, i, cb - col0))

    def halo_prev(cb):
        return pl.BlockSpec((1, SUBLANES, d),
                            lambda b, i: (b, jnp.maximum(i * nb - 1, 0), cb - col0))

    def halo_next(cb):
        return pl.BlockSpec((1, SUBLANES, d),
                            lambda b, i: (b, jnp.minimum((i + 1) * nb, last_halo), cb - col0))

    def weight():
        return pl.BlockSpec((None, d, d), lambda b, i: (layer, 0, 0), pipeline_mode=pl.Buffered(1))

    in_specs = [
        pl.BlockSpec((1, tm, d), whole), yf_spec,
        col(COL_XIN), col(COL_BG), col(COL_CG), col(COL_ZC),
        col(COL_GL), col(COL_GL + 1), col(COL_GL + 2),
        halo_prev(COL_XIN), halo_prev(COL_CG), halo_next(COL_XIN), halo_next(COL_CG),
        pl.BlockSpec((1, tm, d), whole),
        pl.BlockSpec((1, 1, d), lambda b, i: (mod_row(b), 0, 2)),
        pl.BlockSpec((None, 3, d), lambda b, i: (layer, 0, 0)),
        weight(), weight(), weight(), weight(),
    ]
    args = [ya, yf, p, p, p, p, p, p, p, p, p, p, p, x, mod3, conv_w, wa, wc, wf, wo]
    out_specs = pl.BlockSpec((1, tm, d), whole)
    out_shape = jax.ShapeDtypeStruct((bx, sx, d), F32)
    mode = "plain"
    if final_norm_w is not None:
        mode = "final_norm"
        in_specs.append(pl.BlockSpec((1, d), lambda b, i: (0, 0)))
        args.append(final_norm_w)
    elif next_norm is not None:
        mode = "next_norm"
        norm_w, mod3_next = next_norm
        in_specs += [
            pl.BlockSpec((None, 1, d), lambda b, i: (layer + 1, 0, 0)),
            pl.BlockSpec((1, 1, d), lambda b, i: (mod_row(b), 0, 0)),
            pl.BlockSpec((1, 1, d), lambda b, i: (mod_row(b), 0, 1)),
        ]
        args += [norm_w, mod3_next, mod3_next]
        out_specs = [out_specs, pl.BlockSpec((1, tm, d), whole)]
        out_shape = [out_shape, jax.ShapeDtypeStruct((bx, sx, d), BF16)]
    return pl.pallas_call(
        functools.partial(_merge_kernel, mode=mode),
        grid=(bx, sx // tm),
        in_specs=in_specs,
        out_specs=out_specs,
        out_shape=out_shape,
        compiler_params=_cparams(("parallel", "parallel")),
        name="merge",
    )(*args)


def _rope_tables(n_tokens):
    axis_dim = HEAD_DIM // 2
    rows = n_tokens // GRID_W
    row = jnp.repeat(jnp.arange(rows), GRID_W).astype(F32)
    col = jnp.tile(jnp.arange(GRID_W), rows).astype(F32)
    inv_freq = ROPE_BASE ** (-jnp.arange(0, axis_dim, 2, dtype=F32) / axis_dim)
    ang_r = row[:, None] * inv_freq
    ang_c = col[:, None] * inv_freq
    ang = jnp.concatenate([ang_r, ang_c] * 4, axis=-1)
    sign = jnp.where(jnp.arange(V_DIM) < V_DIM // 2, -1.0, 1.0).astype(F32)
    cos = jnp.cos(ang)
    sin = jnp.sin(ang) * sign
    return jnp.stack([cos * Q_SCALE, cos]), jnp.stack([sin * Q_SCALE, sin])


def _head_layout(w_qk):
    depth, d, w = w_qk.shape
    t = w_qk.reshape(depth, d, w // V_DIM, 2, 2, 2, HEAD_DIM // 4)
    return t.transpose(0, 1, 2, 5, 3, 4, 6).reshape(depth, d, w)


def _fourier_tables(n):
    cn, sn = _dft_cos_sin(n)
    return jnp.concatenate([cn, -sn], axis=1).astype(BF16)


def _pick_tile(n, target):
    t = min(n, target)
    while n % t:
        t //= 2
    return t


def kernel(x, c, ctx, c_ctx, norm_w, w_mod, b_mod, w_in, lambda_qk, subln_w, conv_w,
           w_attn_o, w_conv_o, w_four_o, w_out, final_norm_w):
    b, s, d = x.shape
    lc = ctx.shape[1]
    depth = w_in.shape[0]
    assert d == N_HEADS * V_DIM and w_in.shape[2] == N_PROJ_BLOCKS * d
    assert s % GRID_W == 0 and s % LANES == 0 and lc % SUBLANES == 0

    pad = (-(b + 1)) % MOD_ROWS_PAD
    cond = jnp.concatenate([c, c_ctx[None, :], jnp.zeros((pad, d), F32)], axis=0)
    mod = _modulation(cond, w_mod, b_mod)
    n_rows = cond.shape[0]

    norm_w = norm_w.reshape(depth, 1, d)
    subln_w = subln_w.reshape(depth, 1, V_DIM)
    n_qk = 2 * d
    w_in_b = w_in.astype(BF16)
    w_in_b = w_in_b.at[:, :, :n_qk].set(_head_layout(w_in_b[:, :, :n_qk]))
    wa_b, wc_b, wf_b, wo_b = (w.astype(BF16) for w in (w_attn_o, w_conv_o, w_four_o, w_out))

    rope_tabs = _rope_tables(s)
    cc, sc = _dft_cos_sin(FGROUP_DIM)
    cs_chan = jnp.concatenate([cc, sc], axis=1).astype(BF16)
    w_pos_ctx = _fourier_tables(lc)

    ctx_flat = ctx.reshape(1, b * lc, d)
    lat_row = lambda bb: bb
    ctx_row = lambda bb: b

    tm_lat = _pick_tile(s, 2048)
    tm_ctx = _pick_tile(b * lc, 2048)
    tmm_lat = _pick_tile(s // 2, 512)
    tmm_ctx = _pick_tile(lc, 256)
    four_tabs = _fourier_half_tables(s, tmm_lat)
    n_ftiles = s // (2 * tmm_lat)
    yf_lat_spec = pl.BlockSpec(
        (None, None, None, tmm_lat, d),
        lambda bb, i: (bb, jnp.where(i < n_ftiles, i, 2 * n_ftiles - 1 - i),
                       jnp.where(i < n_ftiles, 0, 1), 0, 0))
    yf_ctx_spec = pl.BlockSpec((None, tmm_ctx, d), lambda bb, i: (bb, i, 0))

    h_lat = None
    for l in range(depth):
        last = l == depth - 1
        lam_init = 0.8 - 0.6 * math.exp(-0.3 * l)
        mod3 = mod[l].reshape(n_rows, 1, 3 * d)

        if h_lat is None:
            p_lat, h_lat = _proj(x, mod3, lat_row, norm_w, w_in_b, l, COL_V, N_PROJ_BLOCKS - COL_V,
                                 tm_lat, emit_h=True)
        else:
            p_lat = _proj_rest(h_lat, w_in_b, l, COL_V, N_PROJ_BLOCKS - COL_V, tm_lat)
        p_qk = _proj_qk(h_lat, w_in_b, l, rope_tabs, tm_lat)
        c0, cn = (COL_K, 2) if last else (COL_Q, N_PROJ_BLOCKS)
        p_ctx = _proj(ctx_flat, mod3, ctx_row, norm_w, w_in_b, l, c0, cn, tm_ctx).reshape(b, lc, -1)

        ya = _attention(lambda_qk, subln_w, l, lam_init, (p_qk, COL_Q), (p_qk, COL_K),
                        (p_lat, 0), (p_lat, COL_ZA - COL_V),
                        (p_ctx, COL_K - c0), (p_ctx, COL_V - c0))
        yf = _fourier_half(p_lat, COL_V, cs_chan, four_tabs, tmm_lat)
        if last:
            x_new = _merge(x, p_lat, COL_V, ya, yf, yf_lat_spec, mod3, lat_row, conv_w, wa_b, wc_b,
                           wf_b, wo_b, l, tmm_lat, final_norm_w=final_norm_w.reshape(1, d))
        else:
            mod3_next = mod[l + 1].reshape(n_rows, 1, 3 * d)
            x_new, h_lat = _merge(x, p_lat, COL_V, ya, yf, yf_lat_spec, mod3, lat_row, conv_w, wa_b,
                                  wc_b, wf_b, wo_b, l, tmm_lat, next_norm=(norm_w, mod3_next))

        if not last:
            yac = _self_attention(lambda_qk, subln_w, l, lam_init, p_ctx, COL_Q)
            yfc = _fourier(p_ctx, COL_Q, cs_chan, w_pos_ctx, lc)
            ctx3 = ctx_flat.reshape(b, lc, d)
            ctx_flat = _merge(ctx3, p_ctx, COL_Q, yac, yfc, yf_ctx_spec, mod3, ctx_row, conv_w, wa_b, wc_b, wf_b,
                              wo_b, l, tmm_ctx).reshape(1, b * lc, d)
        x = x_new

    return x
```

```python
import functools
import math

import jax
import jax.numpy as jnp
from jax import lax
from jax.experimental import pallas as pl
from jax.experimental.pallas import tpu as pltpu

F32 = jnp.float32
BF16 = jnp.bfloat16

N_HEADS = 8
HEAD_DIM = 64
V_DIM = 2 * HEAD_DIM
FGROUP_DIM = 128
GRID_W = 64
ROPE_BASE = 10000.0
NORM_EPS = 1e-6
SUBLN_EPS = 1e-5
N_PROJ_BLOCKS = 13
COL_Q, COL_K, COL_V, COL_ZA, COL_XIN, COL_BG, COL_CG, COL_ZC, COL_UF, COL_ZF, COL_GL = range(11)

LANES = 128
SUBLANES = 8
VMEM_LIMIT_BYTES = 56 * 1024 * 1024
MOD_ROWS_PAD = 8
ATTN_SUB_ROWS = 256
ATTN_CARRY_BLOCKS = 2
ATTN_HEADS_PER_STEP = 2
Q_SCALE = math.log2(math.e) / math.sqrt(HEAD_DIM)


def _cparams(sem):
    return pltpu.CompilerParams(dimension_semantics=sem, vmem_limit_bytes=VMEM_LIMIT_BYTES)


def _mod_kernel(cond_ref, w_ref, b_ref, o_ref):
    cond = cond_ref[...]
    a = cond * jax.nn.sigmoid(cond)
    o_ref[...] = jnp.dot(a, w_ref[...], preferred_element_type=F32,
                         precision=lax.Precision.HIGHEST) + b_ref[...]


def _modulation(cond, w_mod, b_mod):
    depth, d, w3 = w_mod.shape
    rows = cond.shape[0]
    tn = d
    return pl.pallas_call(
        _mod_kernel,
        grid=(depth, w3 // tn),
        in_specs=[
            pl.BlockSpec((rows, d), lambda l, j: (0, 0)),
            pl.BlockSpec((None, d, tn), lambda l, j: (l, 0, j)),
            pl.BlockSpec((None, 1, tn), lambda l, j: (l, 0, j)),
        ],
        out_specs=pl.BlockSpec((None, rows, tn), lambda l, j: (l, 0, j)),
        out_shape=jax.ShapeDtypeStruct((depth, rows, w3), F32),
        compiler_params=_cparams(("parallel", "parallel")),
        name="modulation",
    )(cond, w_mod, b_mod.reshape(depth, 1, w3))


def _store_block(acc, o_ref, j, rope_refs, scale_first):
    if rope_refs is not None:
        cos = rope_refs[0][...]
        sin = rope_refs[1][...]
        for hh in range(acc.shape[1] // LANES):
            a = acc[:, hh * LANES:(hh + 1) * LANES]
            r = a * cos + pltpu.roll(a, LANES // 2, axis=1) * sin
            o_ref[0, :, hh * LANES:(hh + 1) * LANES] = r.astype(BF16)
    elif scale_first:
        scale = jnp.where(j == 0, Q_SCALE, 1.0).astype(F32)
        o_ref[0] = (acc * scale).astype(BF16)
    else:
        o_ref[0] = acc.astype(BF16)


def _proj_kernel(x_ref, sh_ref, sc_ref, nw_ref, w_ref, o_ref, *rest, scale_first):
    h_ref = rest[-1]
    j = pl.program_id(2)

    @pl.when(j == 0)
    def _():
        x = x_ref[0]
        y = x * lax.rsqrt(jnp.mean(x * x, axis=-1, keepdims=True) + NORM_EPS)
        h = ((y * nw_ref[...]) * (1.0 + sc_ref[0]) + sh_ref[0]).astype(BF16)
        h_ref[...] = h
        if len(rest) == 2:
            rest[0][0] = h

    acc = jnp.dot(h_ref[...], w_ref[...], preferred_element_type=F32)
    _store_block(acc, o_ref, j, None, scale_first)


def _proj(x, mod3, mod_row, norm_w, w_in_b, layer, col0, ncols, tm, emit_h=False):
    bx, sx, d = x.shape
    tn = d
    in_specs = [
        pl.BlockSpec((1, tm, d), lambda b, i, j: (b, i, 0)),
        pl.BlockSpec((1, 1, d), lambda b, i, j: (mod_row(b), 0, 0)),
        pl.BlockSpec((1, 1, d), lambda b, i, j: (mod_row(b), 0, 1)),
        pl.BlockSpec((None, 1, d), lambda b, i, j: (layer, 0, 0)),
        pl.BlockSpec((None, d, tn), lambda b, i, j: (layer, 0, col0 + j)),
    ]
    out_specs = pl.BlockSpec((1, tm, tn), lambda b, i, j: (b, i, j))
    out_shape = jax.ShapeDtypeStruct((bx, sx, ncols * tn), BF16)
    if emit_h:
        out_specs = [out_specs, pl.BlockSpec((1, tm, d), lambda b, i, j: (b, i, 0))]
        out_shape = [out_shape, jax.ShapeDtypeStruct((bx, sx, d), BF16)]
    return pl.pallas_call(
        functools.partial(_proj_kernel, scale_first=(col0 == COL_Q)),
        grid=(bx, sx // tm, ncols),
        in_specs=in_specs,
        out_specs=out_specs,
        out_shape=out_shape,
        scratch_shapes=[pltpu.VMEM((tm, d), BF16)],
        compiler_params=_cparams(("parallel", "parallel", "arbitrary")),
        name="proj",
    )(x, mod3, mod3, norm_w, w_in_b)


def _proj_rest_kernel(h_ref, w_ref, o_ref, *, scale_first):
    acc = jnp.dot(h_ref[0], w_ref[...], preferred_element_type=F32)
    _store_block(acc, o_ref, pl.program_id(2), None, scale_first)


def _proj_rest(h, w_in_b, layer, col0, ncols, tm):
    bx, sx, d = h.shape
    tn = d
    return pl.pallas_call(
        functools.partial(_proj_rest_kernel, scale_first=(col0 == COL_Q)),
        grid=(bx, sx // tm, ncols),
        in_specs=[
            pl.BlockSpec((1, tm, d), lambda b, i, j: (b, i, 0)),
            pl.BlockSpec((None, d, tn), lambda b, i, j: (layer, 0, col0 + j)),
        ],
        out_specs=pl.BlockSpec((1, tm, tn), lambda b, i, j: (b, i, j)),
        out_shape=jax.ShapeDtypeStruct((bx, sx, ncols * tn), BF16),
        compiler_params=_cparams(("parallel", "parallel", "arbitrary")),
        name="proj_rest",
    )(h, w_in_b)


def _proj_qk_kernel(h_ref, w_ref, cos_ref, sin_ref, o_ref):
    acc = jnp.dot(h_ref[0], w_ref[...], preferred_element_type=F32)
    _store_block(acc, o_ref, pl.program_id(2), (cos_ref, sin_ref), False)


def _proj_qk(h, w_in_b, layer, rope_tabs, tm):
    bx, sx, d = h.shape
    tn = d
    cos_t, sin_t = rope_tabs
    tab_spec = pl.BlockSpec((None, tm, LANES), lambda b, i, j: (j, i, 0))
    return pl.pallas_call(
        _proj_qk_kernel,
        grid=(bx, sx // tm, 2),
        in_specs=[
            pl.BlockSpec((1, tm, d), lambda b, i, j: (b, i, 0)),
            pl.BlockSpec((None, d, tn), lambda b, i, j: (layer, 0, COL_Q + j)),
            tab_spec, tab_spec,
        ],
        out_specs=pl.BlockSpec((1, tm, tn), lambda b, i, j: (b, i, j)),
        out_shape=jax.ShapeDtypeStruct((bx, sx, 2 * tn), BF16),
        compiler_params=_cparams(("parallel", "parallel", "arbitrary")),
        name="proj_rope",
    )(h, w_in_b, cos_t, sin_t)


def _attn_kernel(lq_ref, sw_ref, q_ref, k_ref, v_ref, z_ref, kc_ref, vc_ref, qn_ref, kn_ref, kcn_ref,
                 o_ref, kall_ref, vext_ref, kalln_ref, scarry_ref, *, lam_init):
    n_lat = k_ref.shape[1]
    n_heads, n_all = kall_ref.shape[0], kall_ref.shape[1]

    def head(ref, g, row_slice=slice(None)):
        return ref[0, row_slice, g * V_DIM:(g + 1) * V_DIM]

    ones_lane = lax.broadcasted_iota(jnp.int32, (n_all, V_DIM), 1) == 0
    for g in range(n_heads):
        kall_ref[g, 0:n_lat, :] = head(k_ref, g)
        kall_ref[g, n_lat:n_all, :] = head(kc_ref, g)
        vext_ref[g, 0:n_lat, 0:V_DIM] = head(v_ref, g)
        vext_ref[g, n_lat:n_all, 0:V_DIM] = head(vc_ref, g)
        vext_ref[g, :, V_DIM:2 * V_DIM] = jnp.where(ones_lane, 1.0, 0.0).astype(BF16)
    kalln_ref[0:n_lat, :] = kn_ref[0]
    kalln_ref[n_lat:n_all, :] = kcn_ref[0]

    lq = lq_ref[...].astype(F32)
    lam = (jnp.exp(jnp.sum(lq[0:1] * lq[1:2], axis=-1, keepdims=True))
           - jnp.exp(jnp.sum(lq[2:3] * lq[3:4], axis=-1, keepdims=True)) + lam_init)

    tq = q_ref.shape[1]
    sub = min(tq, ATTN_SUB_ROWS)
    n_sub = tq // sub
    lane = lax.broadcasted_iota(jnp.int32, (sub, V_DIM), 1)

    def scores(q, keys):
        zero = jnp.zeros_like(q)
        map1 = (lane % HEAD_DIM) < HEAD_DIM // 2
        qq = jnp.concatenate([jnp.where(map1, q, zero), jnp.where(map1, zero, q)], axis=0)
        return lax.dot_general(qq, keys, (((1,), (1,)), ((), ())), preferred_element_type=F32)

    def block_scores(t):
        g, j = divmod(t, n_sub)
        rows = slice(j * sub, (j + 1) * sub)
        if g < n_heads:
            return scores(head(q_ref, g, rows), kall_ref[g])
        return scores(qn_ref[0, rows, :], kalln_ref[...])

    def finish(t, get_s):
        g, j = divmod(t, n_sub)
        rows = slice(j * sub, (j + 1) * sub)
        m = jnp.max(get_s(), axis=-1, keepdims=True)
        p = jnp.exp2(get_s() - m).astype(BF16)
        nd = jnp.dot(p, vext_ref[g], preferred_element_type=F32)
        o = nd[:, 0:V_DIM] / nd[:, V_DIM:V_DIM + 1]
        o = o[:sub] - lam * o[sub:]
        y = o * lax.rsqrt(jnp.mean(o * o, axis=-1, keepdims=True) + SUBLN_EPS)
        y = (y * sw_ref[...]) * (1.0 - lam_init)
        z = head(z_ref, g, rows).astype(F32)
        o_ref[0, rows, g * V_DIM:(g + 1) * V_DIM] = (y * _silu(z)).astype(BF16)

    def value(v):
        return lambda: v

    depth = scarry_ref.shape[0]

    @pl.when((pl.program_id(0) == 0) & (pl.program_id(1) == 0))
    def _():
        for d in range(depth):
            scarry_ref[d] = block_scores(d)

    pending = [(lambda d=d: scarry_ref[d]) for d in range(depth)]

    for t in range(n_heads * n_sub):
        pending.append(value(block_scores(t + depth)))
        finish(t, pending.pop(0))
    for d in range(depth):
        scarry_ref[d] = pending[d]()


def _self_attn_kernel(lq_ref, sw_ref, q_ref, k_ref, v_ref, z_ref, o_ref, *, lam_init):
    n = q_ref.shape[1]
    lq = lq_ref[...].astype(F32)
    lam = (jnp.exp(jnp.sum(lq[0:1] * lq[1:2], axis=-1, keepdims=True))
           - jnp.exp(jnp.sum(lq[2:3] * lq[3:4], axis=-1, keepdims=True)) + lam_init)
    lane = lax.broadcasted_iota(jnp.int32, (n, V_DIM), 1)
    map1 = (lane % HEAD_DIM) < HEAD_DIM // 2
    ones_cols = jnp.where(lane == 0, 1.0, 0.0).astype(BF16)
    heads = [slice(h * V_DIM, (h + 1) * V_DIM) for h in range(q_ref.shape[2] // V_DIM)]

    def scores(cols):
        q = q_ref[0, :, cols]
        zero = jnp.zeros_like(q)
        qq = jnp.concatenate([jnp.where(map1, q, zero), jnp.where(map1, zero, q)], axis=0)
        return lax.dot_general(qq, k_ref[0, :, cols], (((1,), (1,)), ((), ())),
                               preferred_element_type=F32)

    s_all = [scores(cols) for cols in heads]
    p_all = [jnp.exp2(s - jnp.max(s, axis=-1, keepdims=True)).astype(BF16) for s in s_all]
    nd_all = [jnp.dot(p, jnp.concatenate([v_ref[0, :, cols], ones_cols], axis=1),
                      preferred_element_type=F32) for p, cols in zip(p_all, heads)]
    for nd, cols in zip(nd_all, heads):
        o = nd[:, 0:V_DIM] / nd[:, V_DIM:V_DIM + 1]
        o = o[:n] - lam * o[n:]
        y = o * lax.rsqrt(jnp.mean(o * o, axis=-1, keepdims=True) + SUBLN_EPS)
        y = (y * sw_ref[...]) * (1.0 - lam_init)
        o_ref[0, :, cols] = (y * _silu(z_ref[0, :, cols].astype(F32))).astype(BF16)


def _self_attention(lambda_qk, subln_w, layer, lam_init, p, col0):
    b, n, _ = p.shape
    d = N_HEADS * V_DIM

    def col(cb):
        return pl.BlockSpec((1, n, d), lambda bb: (bb, 0, cb - col0))

    return pl.pallas_call(
        functools.partial(_self_attn_kernel, lam_init=lam_init),
        grid=(b,),
        in_specs=[
            pl.BlockSpec((None, 4, HEAD_DIM), lambda bb: (layer, 0, 0)),
            pl.BlockSpec((None, 1, V_DIM), lambda bb: (layer, 0, 0)),
            col(COL_Q), col(COL_K), col(COL_V), col(COL_ZA),
        ],
        out_specs=pl.BlockSpec((1, n, d), lambda bb: (bb, 0, 0)),
        out_shape=jax.ShapeDtypeStruct((b, n, d), BF16),
        compiler_params=_cparams(("parallel",)),
        name="attn_self",
    )(lambda_qk, subln_w, p, p, p, p)


def _attention(lambda_qk, subln_w, layer, lam_init, q, k, v, z, kc, vc):
    b, s, _ = q[0].shape
    lc = kc[0].shape[1]
    n_keys = s + lc
    nh = N_HEADS
    gh = ATTN_HEADS_PER_STEP
    n_groups = nh // gh
    sub = min(s, ATTN_SUB_ROWS)
    depth = min(ATTN_CARRY_BLOCKS, s // sub)

    def group_cols(rows, col):
        return pl.BlockSpec((1, rows, gh * V_DIM), lambda bb, g: (bb, 0, col * n_groups + g))

    def next_head_cols(rows, col):
        def index(bb, g):
            flat = jnp.minimum(bb * n_groups + g + 1, b * n_groups - 1)
            return (flat // n_groups, 0, col * nh + (flat % n_groups) * gh)
        return pl.BlockSpec((1, rows, V_DIM), index)

    in_specs = [
        pl.BlockSpec((None, 4, HEAD_DIM), lambda bb, g: (layer, 0, 0)),
        pl.BlockSpec((None, 1, V_DIM), lambda bb, g: (layer, 0, 0)),
        group_cols(s, q[1]), group_cols(s, k[1]), group_cols(s, v[1]), group_cols(s, z[1]),
        group_cols(lc, kc[1]), group_cols(lc, vc[1]),
        next_head_cols(depth * sub, q[1]), next_head_cols(s, k[1]), next_head_cols(lc, kc[1]),
    ]
    return pl.pallas_call(
        functools.partial(_attn_kernel, lam_init=lam_init),
        grid=(b, n_groups),
        in_specs=in_specs,
        out_specs=group_cols(s, 0),
        out_shape=jax.ShapeDtypeStruct((b, s, nh * V_DIM), BF16),
        scratch_shapes=[
            pltpu.VMEM((gh, n_keys, V_DIM), BF16),
            pltpu.VMEM((gh, n_keys, 2 * V_DIM), BF16),
            pltpu.VMEM((n_keys, V_DIM), BF16),
            pltpu.VMEM((depth, 2 * sub, n_keys), F32),
        ],
        compiler_params=_cparams(("arbitrary", "arbitrary")),
        name="attn_latent",
    )(lambda_qk, subln_w, q[0], k[0], v[0], z[0], kc[0], vc[0], q[0], k[0], kc[0])


def _fourier_kernel(u_ref, cs_ref, w_ref, z_ref, o_ref, ab_ref):
    r = pl.program_id(1)
    n = u_ref.shape[1]

    @pl.when(r == 0)
    def _():
        for g in range(u_ref.shape[2] // FGROUP_DIM):
            cols = slice(g * FGROUP_DIM, (g + 1) * FGROUP_DIM)
            ab = jnp.dot(u_ref[0, :, cols], cs_ref[...], preferred_element_type=F32)
            ab_ref[0:n, cols] = ab[:, :FGROUP_DIM].astype(BF16)
            ab_ref[n:2 * n, cols] = ab[:, FGROUP_DIM:].astype(BF16)

    y = jnp.dot(w_ref[...], ab_ref[...], preferred_element_type=F32)
    o_ref[0] = (y * _silu(z_ref[0].astype(F32))).astype(BF16)


def _dft_cos_sin(n):
    k = jnp.arange(n, dtype=jnp.int32)
    ang = ((k[:, None] * k[None, :]) % n).astype(F32) * (2.0 * math.pi / n)
    scale = 1.0 / math.sqrt(n)
    return jnp.cos(ang) * scale, jnp.sin(ang) * scale


def _fourier(p, col0, cs_chan, w_pos, tr):
    b, n, _ = p.shape
    d = N_HEADS * V_DIM
    return pl.pallas_call(
        _fourier_kernel,
        grid=(b, n // tr),
        in_specs=[
            pl.BlockSpec((1, n, d), lambda bb, r: (bb, 0, COL_UF - col0)),
            pl.BlockSpec((FGROUP_DIM, 2 * FGROUP_DIM), lambda bb, r: (0, 0)),
            pl.BlockSpec((tr, 2 * n), lambda bb, r: (r, 0)),
            pl.BlockSpec((1, tr, d), lambda bb, r: (bb, r, COL_ZF - col0)),
        ],
        out_specs=pl.BlockSpec((1, tr, d), lambda bb, r: (bb, r, 0)),
        out_shape=jax.ShapeDtypeStruct((b, n, d), BF16),
        scratch_shapes=[pltpu.VMEM((2 * n, d), BF16)],
        compiler_params=_cparams(("parallel", "arbitrary")),
        name="fourier",
    )(p, cs_chan, w_pos, p)


def _fourier_half_kernel(u_ref, cs_ref, c_ref, s_ref, rev_ref, zlo_ref, zhi_ref, o_ref, a_ref, b_ref):
    r = pl.program_id(1)
    tr = o_ref.shape[3]

    @pl.when(r == 0)
    def _():
        for g in range(u_ref.shape[2] // FGROUP_DIM):
            cols = slice(g * FGROUP_DIM, (g + 1) * FGROUP_DIM)
            ab = jnp.dot(u_ref[0, :, cols], cs_ref[...], preferred_element_type=F32)
            a_ref[:, cols] = ab[:, :FGROUP_DIM].astype(BF16)
            b_ref[:, cols] = ab[:, FGROUP_DIM:].astype(BF16)

    p = jnp.dot(c_ref[...], a_ref[...], preferred_element_type=F32)
    q = jnp.dot(s_ref[...], b_ref[...], preferred_element_type=F32)
    o_ref[0, 0, 0] = ((p[:tr] - q[:tr]) * _silu(zlo_ref[0].astype(F32))).astype(BF16)
    src = (p + q).astype(BF16)
    hi = jnp.dot(rev_ref[...], src[:tr], preferred_element_type=F32)
    first = lax.broadcasted_iota(jnp.int32, (tr, 1), 0) == 0
    hi = jnp.where(first, src[tr:tr + 1].astype(F32), hi)
    o_ref[0, 0, 1] = (hi * _silu(zhi_ref[0].astype(F32))).astype(BF16)


def _fourier_half_tables(n, tr):
    ext = 2 * SUBLANES
    n_tiles = n // (2 * tr)
    rows = (jnp.arange(n_tiles, dtype=jnp.int32)[:, None] * tr
            + jnp.arange(tr + ext, dtype=jnp.int32)[None, :])
    k = jnp.arange(n, dtype=jnp.int32)
    ang = ((rows[:, :, None] * k[None, None, :]) % n).astype(F32) * (2.0 * math.pi / n)
    scale = 1.0 / math.sqrt(n)
    i = jnp.arange(tr, dtype=jnp.int32)[:, None]
    j = jnp.arange(tr, dtype=jnp.int32)[None, :]
    rev = (j == tr - i).astype(BF16)
    return (jnp.cos(ang) * scale).astype(BF16), (jnp.sin(ang) * scale).astype(BF16), rev


def _fourier_half(p, col0, cs_chan, tabs, tr):
    b, n, _ = p.shape
    d = N_HEADS * V_DIM
    c_t, s_t, rev = tabs
    n_tiles, rows_ext, _ = c_t.shape
    return pl.pallas_call(
        _fourier_half_kernel,
        grid=(b, n_tiles),
        in_specs=[
            pl.BlockSpec((1, n, d), lambda bb, r: (bb, 0, COL_UF - col0)),
            pl.BlockSpec((FGROUP_DIM, 2 * FGROUP_DIM), lambda bb, r: (0, 0)),
            pl.BlockSpec((None, rows_ext, n), lambda bb, r: (r, 0, 0)),
            pl.BlockSpec((None, rows_ext, n), lambda bb, r: (r, 0, 0)),
            pl.BlockSpec((tr, tr), lambda bb, r: (0, 0)),
            pl.BlockSpec((1, tr, d), lambda bb, r: (bb, r, COL_ZF - col0)),
            pl.BlockSpec((1, tr, d), lambda bb, r: (bb, 2 * n_tiles - 1 - r, COL_ZF - col0)),
        ],
        out_specs=pl.BlockSpec((1, 1, 2, tr, d), lambda bb, r: (bb, r, 0, 0, 0)),
        out_shape=jax.ShapeDtypeStruct((b, n_tiles, 2, tr, d), BF16),
        scratch_shapes=[pltpu.VMEM((n, d), BF16), pltpu.VMEM((n, d), BF16)],
        compiler_params=_cparams(("parallel", "arbitrary")),
        name="fourier_half",
    )(p, cs_chan, c_t, s_t, rev, p, p)


def _sigmoid(z):
    return 0.5 * jnp.tanh(0.5 * z) + 0.5


def _silu(z):
    h = 0.5 * z
    return h * jnp.tanh(h) + h


def _merge_kernel(ya_ref, yf_ref, xin_ref, bg_ref, cg_ref, zc_ref,
                  ga_ref, gc_ref, gf_ref, xin_p_ref, cg_p_ref, xin_n_ref, cg_n_ref,
                  x_ref, gate_ref, cw_ref, wa_ref, wc_ref, wf_ref, wo_ref, *rest, mode):
    o_ref = rest[-2] if mode == "next_norm" else rest[-1]
    i = pl.program_id(1)
    tm = x_ref.shape[1]
    pad = SUBLANES

    def branch(y_gated, w_ref, g_ref):
        t = jnp.dot(y_gated, w_ref[...], preferred_element_type=F32)
        return _sigmoid(g_ref[0].astype(F32)) * t

    t_a = jnp.dot(ya_ref[0], wa_ref[...], preferred_element_type=F32)
    merged_af = (_sigmoid(ga_ref[0].astype(F32)) * t_a
                 + branch(yf_ref[...], wf_ref, gf_ref))
    bits = pltpu.bitcast(t_a[0:SUBLANES, 0:LANES], jnp.uint32)
    zero = pltpu.bitcast((bits >> 16) >> 16, F32)[0:1, 0:1]

    u = cg_ref[0].astype(F32) * xin_ref[0].astype(F32)
    up = cg_p_ref[0, pad - 1:pad, :].astype(F32) * xin_p_ref[0, pad - 1:pad, :].astype(F32)
    un = cg_n_ref[0, 0:1, :].astype(F32) * xin_n_ref[0, 0:1, :].astype(F32)
    up = jnp.where(i == 0, 0.0, up)
    un = jnp.where(i == pl.num_programs(1) - 1, 0.0, un)
    row = lax.broadcasted_iota(jnp.int32, (tm, 1), 0)
    u_prev = jnp.where(row == 0, up, pltpu.roll(u, 1, axis=0))
    u_next = jnp.where(row == tm - 1, un, pltpu.roll(u, tm - 1, axis=0))
    cw = cw_ref[...] + zero
    conv = cw[0:1] * u_prev + cw[1:2] * u + cw[2:3] * u_next
    y_c = bg_ref[0].astype(F32) * conv
    yc_gated = (y_c * _silu(zc_ref[0].astype(F32))).astype(BF16)
    merged = merged_af + branch(yc_gated, wc_ref, gc_ref)
    out = jnp.dot(merged.astype(BF16), wo_ref[...], preferred_element_type=F32)
    x_new = x_ref[0] + gate_ref[0] * out
    if mode == "plain":
        o_ref[0] = x_new
        return
    y = x_new * lax.rsqrt(jnp.mean(x_new * x_new, axis=-1, keepdims=True) + NORM_EPS)
    if mode == "final_norm":
        o_ref[0] = y * rest[0][...]
    else:
        nw_ref, shn_ref, scn_ref, _, hout_ref = rest
        o_ref[0] = x_new
        hout_ref[0] = ((y * nw_ref[...]) * (1.0 + scn_ref[0]) + shn_ref[0]).astype(BF16)


def _merge(x, p, col0, ya, yf, yf_spec, mod3, mod_row, conv_w, wa, wc, wf, wo, layer, tm,
           final_norm_w=None, next_norm=None):
    bx, sx, d = x.shape
    nb = tm // SUBLANES
    last_halo = sx // SUBLANES - 1

    def whole(b, i):
        return (b, i, 0)

    def col(cb):
        return pl.BlockSpec((1, tm, d), lambda b, i: (b, i, cb - col0))

    def halo_prev(cb):
        return pl.BlockSpec((1, SUBLANES, d),
                            lambda b, i: (b, jnp.maximum(i * nb - 1, 0), cb - col0))

    def halo_next(cb):
        return pl.BlockSpec((1, SUBLANES, d),
                            lambda b, i: (b, jnp.minimum((i + 1) * nb, last_halo), cb - col0))

    def weight():
        return pl.BlockSpec((None, d, d), lambda b, i: (layer, 0, 0), pipeline_mode=pl.Buffered(1))

    in_specs = [
        pl.BlockSpec((1, tm, d), whole), yf_spec,
        col(COL_XIN), col(COL_BG), col(COL_CG), col(COL_ZC),
        col(COL_GL), col(COL_GL + 1), col(COL_GL + 2),
        halo_prev(COL_XIN), halo_prev(COL_CG), halo_next(COL_XIN), halo_next(COL_CG),
        pl.BlockSpec((1, tm, d), whole),
        pl.BlockSpec((1, 1, d), lambda b, i: (mod_row(b), 0, 2)),
        pl.BlockSpec((None, 3, d), lambda b, i: (layer, 0, 0)),
        weight(), weight(), weight(), weight(),
    ]
    args = [ya, yf, p, p, p, p, p, p, p, p, p, p, p, x, mod3, conv_w, wa, wc, wf, wo]
    out_specs = pl.BlockSpec((1, tm, d), whole)
    out_shape = jax.ShapeDtypeStruct((bx, sx, d), F32)
    mode = "plain"
    if final_norm_w is not None:
        mode = "final_norm"
        in_specs.append(pl.BlockSpec((1, d), lambda b, i: (0, 0)))
        args.append(final_norm_w)
    elif next_norm is not None:
        mode = "next_norm"
        norm_w, mod3_next = next_norm
        in_specs += [
            pl.BlockSpec((None, 1, d), lambda b, i: (layer + 1, 0, 0)),
            pl.BlockSpec((1, 1, d), lambda b, i: (mod_row(b), 0, 0)),
            pl.BlockSpec((1, 1, d), lambda b, i: (mod_row(b), 0, 1)),
        ]
        args += [norm_w, mod3_next, mod3_next]
        out_specs = [out_specs, pl.BlockSpec((1, tm, d), whole)]
        out_shape = [out_shape, jax.ShapeDtypeStruct((bx, sx, d), BF16)]
    return pl.pallas_call(
        functools.partial(_merge_kernel, mode=mode),
        grid=(bx, sx // tm),
        in_specs=in_specs,
        out_specs=out_specs,
        out_shape=out_shape,
        compiler_params=_cparams(("parallel", "parallel")),
        name="merge",
    )(*args)


def _rope_tables(n_tokens):
    axis_dim = HEAD_DIM // 2
    rows = n_tokens // GRID_W
    row = jnp.repeat(jnp.arange(rows), GRID_W).astype(F32)
    col = jnp.tile(jnp.arange(GRID_W), rows).astype(F32)
    inv_freq = ROPE_BASE ** (-jnp.arange(0, axis_dim, 2, dtype=F32) / axis_dim)
    ang_r = row[:, None] * inv_freq
    ang_c = col[:, None] * inv_freq
    ang = jnp.concatenate([ang_r, ang_c] * 4, axis=-1)
    sign = jnp.where(jnp.arange(V_DIM) < V_DIM // 2, -1.0, 1.0).astype(F32)
    cos = jnp.cos(ang)
    sin = jnp.sin(ang) * sign
    return jnp.stack([cos * Q_SCALE, cos]), jnp.stack([sin * Q_SCALE, sin])


def _head_layout(w_qk):
    depth, d, w = w_qk.shape
    t = w_qk.reshape(depth, d, w // V_DIM, 2, 2, 2, HEAD_DIM // 4)
    return t.transpose(0, 1, 2, 5, 3, 4, 6).reshape(depth, d, w)


def _fourier_tables(n):
    cn, sn = _dft_cos_sin(n)
    return jnp.concatenate([cn, -sn], axis=1).astype(BF16)


def _pick_tile(n, target):
    t = min(n, target)
    while n % t:
        t //= 2
    return t


def kernel(x, c, ctx, c_ctx, norm_w, w_mod, b_mod, w_in, lambda_qk, subln_w, conv_w,
           w_attn_o, w_conv_o, w_four_o, w_out, final_norm_w):
    b, s, d = x.shape
    lc = ctx.shape[1]
    depth = w_in.shape[0]
    assert d == N_HEADS * V_DIM and w_in.shape[2] == N_PROJ_BLOCKS * d
    assert s % GRID_W == 0 and s % LANES == 0 and lc % SUBLANES == 0

    pad = (-(b + 1)) % MOD_ROWS_PAD
    cond = jnp.concatenate([c, c_ctx[None, :], jnp.zeros((pad, d), F32)], axis=0)
    mod = _modulation(cond, w_mod, b_mod)
    n_rows = cond.shape[0]

    norm_w = norm_w.reshape(depth, 1, d)
    subln_w = subln_w.reshape(depth, 1, V_DIM)
    n_qk = 2 * d
    w_in_b = w_in.astype(BF16)
    w_in_b = w_in_b.at[:, :, :n_qk].set(_head_layout(w_in_b[:, :, :n_qk]))
    wa_b, wc_b, wf_b, wo_b = (w.astype(BF16) for w in (w_attn_o, w_conv_o, w_four_o, w_out))

    rope_tabs = _rope_tables(s)
    cc, sc = _dft_cos_sin(FGROUP_DIM)
    cs_chan = jnp.concatenate([cc, sc], axis=1).astype(BF16)
    w_pos_ctx = _fourier_tables(lc)

    ctx_flat = ctx.reshape(1, b * lc, d)
    lat_row = lambda bb: bb
    ctx_row = lambda bb: b

    tm_lat = _pick_tile(s, 2048)
    tm_ctx = _pick_tile(b * lc, 2048)
    tmm_lat = _pick_tile(s // 2, 512)
    tmm_ctx = _pick_tile(lc, 256)
    four_tabs = _fourier_half_tables(s, tmm_lat)
    n_ftiles = s // (2 * tmm_lat)
    yf_lat_spec = pl.BlockSpec(
        (None, None, None, tmm_lat, d),
        lambda bb, i: (bb, jnp.where(i < n_ftiles, i, 2 * n_ftiles - 1 - i),
                       jnp.where(i < n_ftiles, 0, 1), 0, 0))
    yf_ctx_spec = pl.BlockSpec((None, tmm_ctx, d), lambda bb, i: (bb, i, 0))

    h_lat = None
    h_ctx = None
    for l in range(depth):
        last = l == depth - 1
        lam_init = 0.8 - 0.6 * math.exp(-0.3 * l)
        mod3 = mod[l].reshape(n_rows, 1, 3 * d)

        if h_lat is None:
            p_lat, h_lat = _proj(x, mod3, lat_row, norm_w, w_in_b, l, COL_V, N_PROJ_BLOCKS - COL_V,
                                 tm_lat, emit_h=True)
        else:
            p_lat = _proj_rest(h_lat, w_in_b, l, COL_V, N_PROJ_BLOCKS - COL_V, tm_lat)
        p_qk = _proj_qk(h_lat, w_in_b, l, rope_tabs, tm_lat)
        c0, cn = (COL_K, 2) if last else (COL_Q, N_PROJ_BLOCKS)
        if h_ctx is None:
            p_ctx = _proj(ctx_flat, mod3, ctx_row, norm_w, w_in_b, l, c0, cn, tm_ctx)
        else:
            p_ctx = _proj_rest(h_ctx, w_in_b, l, c0, cn, tm_ctx)
        p_ctx = p_ctx.reshape(b, lc, -1)

        ya = _attention(lambda_qk, subln_w, l, lam_init, (p_qk, COL_Q), (p_qk, COL_K),
                        (p_lat, 0), (p_lat, COL_ZA - COL_V),
                        (p_ctx, COL_K - c0), (p_ctx, COL_V - c0))
        yf = _fourier_half(p_lat, COL_V, cs_chan, four_tabs, tmm_lat)
        if last:
            x_new = _merge(x, p_lat, COL_V, ya, yf, yf_lat_spec, mod3, lat_row, conv_w, wa_b, wc_b,
                           wf_b, wo_b, l, tmm_lat, final_norm_w=final_norm_w.reshape(1, d))
        else:
            mod3_next = mod[l + 1].reshape(n_rows, 1, 3 * d)
            x_new, h_lat = _merge(x, p_lat, COL_V, ya, yf, yf_lat_spec, mod3, lat_row, conv_w, wa_b,
                                  wc_b, wf_b, wo_b, l, tmm_lat, next_norm=(norm_w, mod3_next))

        if not last:
            yac = _self_attention(lambda_qk, subln_w, l, lam_init, p_ctx, COL_Q)
            yfc = _fourier(p_ctx, COL_Q, cs_chan, w_pos_ctx, lc)
            ctx3 = ctx_flat.reshape(b, lc, d)
            ctx_new, h_next = _merge(ctx3, p_ctx, COL_Q, yac, yfc, yf_ctx_spec, mod3, ctx_row, conv_w,
                                     wa_b, wc_b, wf_b, wo_b, l, tmm_ctx, next_norm=(norm_w, mod3_next))
            ctx_flat = ctx_new.reshape(1, b * lc, d)
            h_ctx = h_next.reshape(1, b * lc, d)
        x = x_new

    return x
```
